```python
import jax, jax.numpy as jnp
from jax import lax
import numpy as np

D_MODEL = 1024
BATCH = 2
SEQ = 8192
DEPTH = 1

HEAD_DIM = 64
RWKV_HEADS = 16
RWKV_DIM = RWKV_HEADS * HEAD_DIM
DECAY_LORA = 64
AAA_LORA = 64
GATE_LORA = 160
GN_EPS = HEAD_DIM * 1e-5
RWKV_IN = 3 * RWKV_DIM + DECAY_LORA + AAA_LORA + GATE_LORA
DIL_GROUPS = ((128, 1), (512, 4), (2048, 16))
HEADS_PER_GROUP = 4
ATTN_HEADS = 3 * HEADS_PER_GROUP
ATTN_DIM = ATTN_HEADS * HEAD_DIM
ATTN_OUT_DIM = HEADS_PER_GROUP * HEAD_DIM
IN_SPLITS = (RWKV_DIM, RWKV_DIM, RWKV_DIM, DECAY_LORA, AAA_LORA, GATE_LORA,
             ATTN_DIM, ATTN_DIM, ATTN_DIM, D_MODEL, D_MODEL)
N_IN = RWKV_IN + 3 * ATTN_DIM + 2 * D_MODEL
MEM_LEN = 256
XATTN_HEADS = 4
XATTN_HEAD_DIM = D_MODEL // XATTN_HEADS
FFN_DIM = 4 * D_MODEL
NORM_EPS = 1e-6

kernel_name = 'hybrid_rwkv7_dilated_alibi_gated'


def rmsnorm(x, g):
    xf = x.astype(jnp.float32)
    y = xf * lax.rsqrt(jnp.mean(xf * xf, axis=-1, keepdims=True) + NORM_EPS) * g.astype(jnp.float32)
    return y.astype(x.dtype)


def alibi_slopes(n_heads):
    return jnp.exp2(-8.0 * (jnp.arange(n_heads, dtype=jnp.float32) + 1.0) / n_heads)


def rwkv7_scan(r, w, k, v, a_, b_):
    B, S, H, N = r.shape

    def step(state, inp):
        r_t, w_t, k_t, v_t, a_t, b_t = inp
        sa = jnp.einsum('bhij,bhj->bhi', state, a_t)
        state = state * w_t[:, :, None, :] + sa[..., None] * b_t[:, :, None, :] + v_t[..., None] * k_t[:, :, None, :]
        return state, jnp.einsum('bhij,bhj->bhi', state, r_t)

    xs = tuple(jnp.moveaxis(t, 1, 0) for t in (r, w, k, v, a_, b_))
    _, ys = lax.scan(step, jnp.zeros((B, H, N, N), jnp.float32), xs)
    return jnp.moveaxis(ys, 0, 1)


def rwkv7_mix(p_r, p_k, p_v, p_wd, p_ad, p_gd, w0, w2, a0, a2, g2, k_k, k_a, r_k, gn_w, gn_b):
    B, S, _ = p_r.shape
    heads = lambda t: t.reshape(B, S, RWKV_HEADS, HEAD_DIM)
    w_log = -jax.nn.softplus(-(w0 + jnp.tanh(p_wd) @ w2)) - 0.5
    decay = jnp.exp(-jnp.exp(w_log))
    a = jax.nn.sigmoid(a0 + p_ad @ a2)
    g = jax.nn.sigmoid(p_gd) @ g2
    kk = heads(p_k * k_k)
    kk = kk / jnp.maximum(jnp.sqrt(jnp.sum(kk * kk, axis=-1, keepdims=True)), 1e-12)
    k = p_k * (1.0 + (a - 1.0) * k_a)
    r, k, v, decay, a = heads(p_r), heads(k), heads(p_v), heads(decay), heads(a)
    y = rwkv7_scan(r, decay, k, v, -kk, kk * a)
    mean = jnp.mean(y, axis=-1, keepdims=True)
    var = jnp.mean(jnp.square(y - mean), axis=-1, keepdims=True)
    yn = ((y - mean) * lax.rsqrt(var + GN_EPS)).reshape(B, S, RWKV_DIM) * gn_w + gn_b
    bonus = (jnp.sum(r * k * r_k, axis=-1, keepdims=True) * v).reshape(B, S, RWKV_DIM)
    return (yn + bonus) * g


def dilated_attention(q, k, v, window, dilation, slopes):
    B, S, H, Dh = q.shape
    blk = window // dilation
    span = blk * dilation
    Sp = -(-S // span) * span
    nb = Sp // span
    padw = ((0, 0), (0, Sp - S), (0, 0), (0, 0))

    def blocks(t):
        return jnp.pad(t, padw).astype(jnp.float32).reshape(B, nb, blk, dilation, H, Dh)

    def with_prev(t):
        prev = jnp.pad(t, ((0, 0), (1, 0), (0, 0), (0, 0), (0, 0), (0, 0)))[:, :-1]
        return jnp.concatenate([prev, t], axis=2)

    qb = blocks(q) * (Dh ** -0.5)
    kc, vc = with_prev(blocks(k)), with_prev(blocks(v))
    s = jnp.einsum('bnqrhc,bnkrhc->bnrhqk', qb, kc)
    qi = jnp.arange(blk)[:, None]
    ki = jnp.arange(2 * blk)[None, :]
    steps = qi + blk - ki
    in_band = (steps >= 0) & (steps <= blk)
    exists = (jnp.arange(nb)[:, None, None] > 0) | (ki >= blk)[None]
    valid = in_band[None] & exists
    bias = -slopes[:, None, None] * (steps * dilation).astype(jnp.float32)[None]
    s = jnp.where(valid[None, :, None, None], s + bias[None, None, None], -jnp.inf)
    lse = jax.nn.logsumexp(s, axis=-1)
    p = jnp.exp(s - lse[..., None])
    o = jnp.einsum('bnrhqk,bnkrhc->bnqrhc', p, vc).reshape(B, Sp, H, Dh)[:, :S]
    lse = jnp.transpose(lse, (0, 1, 4, 2, 3)).reshape(B, Sp, H)[:, :S]
    return o, lse


def dilated_mixture(q, k, v):
    B, S = q.shape[:2]
    slopes = alibi_slopes(ATTN_HEADS)
    outs, lses = [], []
    for gi, (window, dilation) in enumerate(DIL_GROUPS):
        hs = slice(gi * HEADS_PER_GROUP, (gi + 1) * HEADS_PER_GROUP)
        o, l = dilated_attention(q[:, :, hs], k[:, :, hs], v[:, :, hs], window, dilation, slopes[hs])
        outs.append(o)
        lses.append(l)
    wts = jax.nn.softmax(jnp.stack(lses, axis=0), axis=0)
    y = jnp.sum(wts[..., None] * jnp.stack(outs, axis=0), axis=0)
    return y.reshape(B, S, ATTN_OUT_DIM)


def memory_cross_attention(xn, memn, wq, w_kv, wo):
    B, S, _ = xn.shape
    q = (xn @ wq).astype(jnp.float32).reshape(B, S, XATTN_HEADS, XATTN_HEAD_DIM)
    kv = (memn @ w_kv).astype(jnp.float32).reshape(B, MEM_LEN, 2, XATTN_HEADS, XATTN_HEAD_DIM)
    s = jnp.einsum('bshc,bmhc->bhsm', q, kv[:, :, 0]) * (XATTN_HEAD_DIM ** -0.5)
    p = jax.nn.softmax(s, axis=-1)
    o = jnp.einsum('bhsm,bmhc->bshc', p, kv[:, :, 1]).reshape(B, S, D_MODEL)
    return (o @ wo.astype(jnp.float32)).astype(xn.dtype)


def setup_inputs(seed: int = 0) -> dict:
    key = jax.random.key(seed)
    ks = jax.random.split(key, 32)
    f32 = jnp.float32
    nrm = lambda k, shape, scale: jax.random.normal(k, shape, f32) * scale
    L = DEPTH
    return {
        'x': jax.random.normal(ks[0], (BATCH, SEQ, D_MODEL), f32),
        'mem': jax.random.normal(ks[1], (BATCH, MEM_LEN, D_MODEL), f32),
        'norm_mix_g': 1.0 + nrm(ks[2], (L, D_MODEL), 0.02),
        'w_in': nrm(ks[3], (L, D_MODEL, N_IN), D_MODEL ** -0.5),
        'shift_mu': jax.random.uniform(ks[4], (L, RWKV_IN), f32),
        'w0': jax.random.uniform(ks[5], (L, RWKV_DIM), f32, -5.0, 1.0),
        'w2': nrm(ks[6], (L, DECAY_LORA, RWKV_DIM), 0.1),
        'a0': nrm(ks[7], (L, RWKV_DIM), 0.1),
        'a2': nrm(ks[8], (L, AAA_LORA, RWKV_DIM), 0.1),
        'g2': nrm(ks[9], (L, GATE_LORA, RWKV_DIM), GATE_LORA ** -0.5),
        'k_k': 0.85 + nrm(ks[10], (L, RWKV_DIM), 0.05),
        'k_a': 1.0 + nrm(ks[11], (L, RWKV_DIM), 0.05),
        'r_k': nrm(ks[12], (L, RWKV_HEADS, HEAD_DIM), 0.1),
        'gn_w': 1.0 + nrm(ks[13], (L, RWKV_DIM), 0.02),
        'gn_b': nrm(ks[14], (L, RWKV_DIM), 0.02),
        'p_rwkv': nrm(ks[15], (L, RWKV_DIM, D_MODEL), RWKV_DIM ** -0.5),
        'p_attn': nrm(ks[16], (L, ATTN_OUT_DIM, D_MODEL), ATTN_OUT_DIM ** -0.5),
        'w_out': nrm(ks[17], (L, D_MODEL, D_MODEL), D_MODEL ** -0.5),
        'norm_x_g': 1.0 + nrm(ks[18], (L, D_MODEL), 0.02),
        'norm_mem_g': 1.0 + nrm(ks[19], (L, D_MODEL), 0.02),
        'xa_wq': nrm(ks[20], (L, D_MODEL, D_MODEL), D_MODEL ** -0.5),
        'xa_wkv': nrm(ks[21], (L, D_MODEL, 2 * D_MODEL), D_MODEL ** -0.5),
        'xa_wo': nrm(ks[22], (L, D_MODEL, D_MODEL), D_MODEL ** -0.5),
        'norm_ffn_g': 1.0 + nrm(ks[23], (L, D_MODEL), 0.02),
        'ffn_w1': nrm(ks[24], (L, D_MODEL, FFN_DIM), D_MODEL ** -0.5),
        'ffn_w2': nrm(ks[25], (L, FFN_DIM, D_MODEL), FFN_DIM ** -0.5),
        'norm_final_g': 1.0 + nrm(ks[26], (D_MODEL,), 0.02),
    }


def reference(x, mem, norm_mix_g, w_in, shift_mu, w0, w2, a0, a2, g2, k_k, k_a, r_k, gn_w, gn_b,
              p_rwkv, p_attn, w_out, norm_x_g, norm_mem_g, xa_wq, xa_wkv, xa_wo,
              norm_ffn_g, ffn_w1, ffn_w2, norm_final_g):
    f32 = jnp.float32
    B, S, _ = x.shape
    bounds = list(np.cumsum(IN_SPLITS)[:-1])
    h = x
    for l in range(DEPTH):
        n = rmsnorm(h, norm_mix_g[l])
        proj = (n @ w_in[l]).astype(f32)
        rw = proj[..., :RWKV_IN]
        rw_prev = jnp.pad(rw, ((0, 0), (1, 0), (0, 0)))[:, :-1]
        proj = jnp.concatenate([rw + shift_mu[l].astype(f32) * (rw_prev - rw), proj[..., RWKV_IN:]], axis=-1)
        (p_r, p_k, p_v, p_wd, p_ad, p_gd, q, k, v, gate_r, gate_a) = jnp.split(proj, bounds, axis=-1)
        y_rwkv = rwkv7_mix(p_r, p_k, p_v, p_wd, p_ad, p_gd,
                           w0[l].astype(f32), w2[l].astype(f32), a0[l].astype(f32), a2[l].astype(f32),
                           g2[l].astype(f32), k_k[l].astype(f32), k_a[l].astype(f32), r_k[l].astype(f32),
                           gn_w[l].astype(f32), gn_b[l].astype(f32))
        hd = lambda t: t.reshape(B, S, ATTN_HEADS, HEAD_DIM)
        y_attn = dilated_mixture(hd(q), hd(k), hd(v))
        merged = (jax.nn.sigmoid(gate_r) * (y_rwkv @ p_rwkv[l].astype(f32))
                  + jax.nn.sigmoid(gate_a) * (y_attn @ p_attn[l].astype(f32)))
        h = h + (merged @ w_out[l].astype(f32)).astype(h.dtype)
        h = h + memory_cross_attention(rmsnorm(h, norm_x_g[l]), rmsnorm(mem, norm_mem_g[l]),
                                       xa_wq[l], xa_wkv[l], xa_wo[l])
        u = rmsnorm(h, norm_ffn_g[l]) @ ffn_w1[l]
        h = h + (jnp.square(jax.nn.relu(u)) @ ffn_w2[l]).astype(h.dtype)
    return rmsnorm(h, norm_final_g)
```

```python
import functools

import jax
import jax.numpy as jnp
from jax import lax
from jax.experimental import pallas as pl
from jax.experimental.pallas import tpu as pltpu

F32 = jnp.float32
BF16 = jnp.bfloat16

HEAD_DIM = 64
PAIR = 2 * HEAD_DIM
CHUNK = 64
RWKV_DIM = 1024
N_PAIRS = RWKV_DIM // PAIR
DECAY_LORA = 64
AAA_LORA = 64
GATE_LORA = 160
LORA_WD_OFF = 3 * RWKV_DIM
LORA_AD_OFF = LORA_WD_OFF + 128
LORA_GD_OFF = LORA_AD_OFF + 128
RW_WIDTH = LORA_GD_OFF + 256
GN_EPS = HEAD_DIM * 1e-5
NORM_EPS = 1e-6
DIL_GROUPS = ((128, 1), (512, 4), (2048, 16))
HEADS_PER_GROUP = 4
ATTN_HEADS = 12
ATTN_GROUP_DIM = HEADS_PER_GROUP * HEAD_DIM
ATTN_DIM = ATTN_HEADS * HEAD_DIM
ATTN_BLK = 128
XATTN_HEADS = 4
VMEM_LIMIT = 56 * 1024 * 1024


def _dot(a, b):
    return jnp.dot(a.astype(BF16), b.astype(BF16), preferred_element_type=F32)


def _dot_nt(a, b):
    return lax.dot_general(a.astype(BF16), b.astype(BF16), (((1,), (1,)), ((), ())),
                           preferred_element_type=F32)


def _dot_tn(a, b):
    return lax.dot_general(a.astype(BF16), b.astype(BF16), (((0,), (0,)), ((), ())),
                           preferred_element_type=F32)


def _params(*sem):
    return pltpu.CompilerParams(dimension_semantics=sem, vmem_limit_bytes=VMEM_LIMIT)


def _rmsnorm_f32(x, g):
    return x * lax.rsqrt(jnp.mean(x * x, axis=-1, keepdims=True) + NORM_EPS) * g


def _rmsnorm_kernel(x_ref, g_ref, o_ref):
    o_ref[...] = _rmsnorm_f32(x_ref[...], g_ref[...]).astype(o_ref.dtype)


def _rmsnorm(x, g, out_dtype, tm):
    m, d = x.shape
    return pl.pallas_call(
        _rmsnorm_kernel,
        out_shape=jax.ShapeDtypeStruct((m, d), out_dtype),
        grid=(m // tm,),
        in_specs=[pl.BlockSpec((tm, d), lambda i: (i, 0)), pl.BlockSpec((1, d), lambda i: (0, 0))],
        out_specs=pl.BlockSpec((tm, d), lambda i: (i, 0)),
        compiler_params=_params("parallel"),
        name="rmsnorm",
    )(x, g.reshape(1, d))


def _matmul_kernel(a_ref, w_ref, o_ref):
    o_ref[...] = jnp.dot(a_ref[...], w_ref[...], preferred_element_type=F32).astype(o_ref.dtype)


def _matmul(a, w, out_dtype, tm, tn, name):
    m, k = a.shape
    n = w.shape[1]
    return pl.pallas_call(
        _matmul_kernel,
        out_shape=jax.ShapeDtypeStruct((m, n), out_dtype),
        grid=(m // tm, n // tn),
        in_specs=[pl.BlockSpec((tm, k), lambda i, j: (i, 0)), pl.BlockSpec((k, tn), lambda i, j: (0, j))],
        out_specs=pl.BlockSpec((tm, tn), lambda i, j: (i, j)),
        compiler_params=_params("parallel", "parallel"),
        name=name,
    )(a, w)


PREV_ROWS = 16


def _proj_shift_kernel(n_ref, nprev_ref, w_ref, mu_ref, o_ref, *, tiles_per_seq):
    i = pl.program_id(0)
    p = jnp.dot(n_ref[...], w_ref[...], preferred_element_type=F32)
    pp = jnp.dot(nprev_ref[...], w_ref[...], preferred_element_type=F32)
    first = (i % tiles_per_seq) == 0
    before = jnp.where(first, 0.0, pp[PREV_ROWS - 1:PREV_ROWS, :])
    row = lax.broadcasted_iota(jnp.int32, p.shape, 0)
    prev = jnp.where(row == 0, before, pltpu.roll(p, 1, 0))
    o_ref[...] = p + mu_ref[...] * (prev - p)


def _proj_shift(n, w, mu, seq, tm, tn):
    m, k = n.shape
    width = w.shape[1]
    blocks = tm // PREV_ROWS
    return pl.pallas_call(
        functools.partial(_proj_shift_kernel, tiles_per_seq=seq // tm),
        out_shape=jax.ShapeDtypeStruct((m, width), F32),
        grid=(m // tm, width // tn),
        in_specs=[
            pl.BlockSpec((tm, k), lambda i, j: (i, 0)),
            pl.BlockSpec((PREV_ROWS, k), lambda i, j: (jnp.maximum(i * blocks - 1, 0), 0)),
            pl.BlockSpec((k, tn), lambda i, j: (0, j)),
            pl.BlockSpec((1, tn), lambda i, j: (0, j)),
        ],
        out_specs=pl.BlockSpec((tm, tn), lambda i, j: (i, j)),
        compiler_params=_params("parallel", "parallel"),
        name="proj_shift",
    )(n, n, w, mu)


VEC_W0, VEC_A0, VEC_KK, VEC_KA, VEC_RK, VEC_GNW, VEC_GNB = range(7)


def _rwkv_kernel(p_ref, vec_ref, w2_ref, a2_ref, g2_ref, y_ref, st_ref):
    L = CHUNK

    @pl.when(pl.program_id(1) == 0)
    def _init():
        st_ref[...] = jnp.zeros_like(st_ref)

    wd = jnp.tanh(p_ref[:, LORA_WD_OFF:LORA_WD_OFF + 128]).astype(BF16)
    ad = p_ref[:, LORA_AD_OFF:LORA_AD_OFF + 128].astype(BF16)
    gd = jax.nn.sigmoid(p_ref[:, LORA_GD_OFF:LORA_GD_OFF + 256]).astype(BF16)

    head_a = lax.broadcasted_iota(jnp.int32, (L, PAIR), 1) < HEAD_DIM
    row2 = lax.broadcasted_iota(jnp.int32, (PAIR, PAIR), 0)
    col2 = lax.broadcasted_iota(jnp.int32, (PAIR, PAIR), 1)
    strict = row2 > col2
    incl = row2 >= col2
    eye = row2 == col2
    r3 = lax.broadcasted_iota(jnp.int32, (L, 3 * L), 0)
    c3 = lax.broadcasted_iota(jnp.int32, (L, 3 * L), 1)
    c3 = c3 - jnp.where(c3 >= L, L, 0) - jnp.where(c3 >= 2 * L, L, 0)
    tri3 = jnp.where(r3 >= c3, 1.0, 0.0).astype(BF16)

    def hsum(x):
        sa = jnp.sum(jnp.where(head_a, x, 0.0), axis=-1, keepdims=True)
        sb = jnp.sum(jnp.where(head_a, 0.0, x), axis=-1, keepdims=True)
        return jnp.where(head_a, sa, sb)

    def stack(x):
        return jnp.concatenate([jnp.where(head_a, x, 0.0), jnp.where(head_a, 0.0, x)], axis=0).astype(BF16)

    for p in range(N_PAIRS):
        cs = slice(p * PAIR, (p + 1) * PAIR)
        vec = lambda i: vec_ref[i:i + 1, cs]
        pr = p_ref[:, cs]
        pk = p_ref[:, RWKV_DIM + p * PAIR:RWKV_DIM + (p + 1) * PAIR]
        pv = p_ref[:, 2 * RWKV_DIM + p * PAIR:2 * RWKV_DIM + (p + 1) * PAIR]

        z = -(vec(VEC_W0) + _dot(wd, w2_ref[:, cs]))
        softplus = jnp.maximum(z, 0.0) + jnp.log1p(jnp.exp(-jnp.abs(z)))
        lw = -jnp.exp(-softplus - 0.5)
        a_lr = jax.nn.sigmoid(vec(VEC_A0) + _dot(ad, a2_ref[:, cs]))
        gate = _dot(gd, g2_ref[:, cs])
        kk = pk * vec(VEC_KK)
        kkn = kk / jnp.maximum(jnp.sqrt(hsum(kk * kk)), 1e-12)
        kbar = pk * (1.0 + (a_lr - 1.0) * vec(VEC_KA))
        avec = -kkn
        bvec = kkn * a_lr

        h1 = lw.astype(BF16)
        r1 = lw - h1.astype(F32)
        h2 = r1.astype(BF16)
        h3 = (r1 - h2.astype(F32)).astype(BF16)
        c = jnp.dot(tri3, jnp.concatenate([h1, h2, h3], axis=0), preferred_element_type=F32)
        cl = c[L - 1:L, :]
        einv = jnp.exp(-c)
        edec = jnp.exp(cl - c)
        s_rt = stack(pr * jnp.exp(c))
        s_at = stack(avec * jnp.exp(c - lw))
        s_bt = stack(bvec * einv)
        s_kt = stack(kbar * einv)
        s_bh = stack(bvec * edec)
        s_kh = stack(kbar * edec)
        s_v = stack(pv)

        scores = _dot_nt(jnp.concatenate([s_at, s_rt], axis=0), jnp.concatenate([s_bt, s_kt], axis=0))
        n_ab = jnp.where(strict, scores[:PAIR, :PAIR], 0.0)
        a_ak = jnp.where(strict, scores[:PAIR, PAIR:], 0.0)
        m_rb = jnp.where(incl, scores[PAIR:, :PAIR], 0.0)
        m_rk = jnp.where(incl, scores[PAIR:, PAIR:], 0.0)

        tinv = jnp.where(eye, 1.0, 0.0) + n_ab
        npow = n_ab
        for _ in range(5):
            npow = _dot(npow, npow)
            tinv = tinv + _dot(npow, tinv)
        pq = _dot(tinv, jnp.concatenate([s_at.astype(F32), _dot(a_ak, s_v)], axis=1))

        rhs = jnp.concatenate(
            [pq, jnp.concatenate([jnp.zeros((PAIR, PAIR), F32), s_v.astype(F32)], axis=1)], axis=0)
        top = _dot(jnp.concatenate([m_rb, m_rk], axis=1), rhs)
        bot = _dot_tn(jnp.concatenate([s_bh, s_kh], axis=0), rhs)
        ry = s_rt.astype(F32) + top[:, :PAIR]
        g_mat = jnp.where(eye, jnp.exp(cl), 0.0) + bot[:, :PAIR]

        st = st_ref[p]
        ys = _dot(jnp.concatenate([ry, g_mat], axis=0), st)
        y_st = ys[:PAIR] + top[:, PAIR:]
        st_ref[p] = ys[PAIR:] + bot[:, PAIR:]
        y = y_st[:L] + y_st[L:]

        mean = hsum(y) * (1.0 / HEAD_DIM)
        d = y - mean
        var = hsum(d * d) * (1.0 / HEAD_DIM)
        yn = d * lax.rsqrt(var + GN_EPS) * vec(VEC_GNW) + vec(VEC_GNB)
        bonus = hsum(pr * kbar * vec(VEC_RK)) * pv
        y_ref[:, cs] = ((yn + bonus) * gate).astype(y_ref.dtype)


def _rwkv(p_rw, vecs, w2p, a2p, g2p, batch, seq):
    m = p_rw.shape[0]
    nt = seq // CHUNK
    return pl.pallas_call(
        _rwkv_kernel,
        out_shape=jax.ShapeDtypeStruct((m, RWKV_DIM), BF16),
        grid=(batch, nt),
        in_specs=[
            pl.BlockSpec((CHUNK, RW_WIDTH), lambda b, t: (b * nt + t, 0)),
            pl.BlockSpec(vecs.shape, lambda b, t: (0, 0)),
            pl.BlockSpec(w2p.shape, lambda b, t: (0, 0)),
            pl.BlockSpec(a2p.shape, lambda b, t: (0, 0)),
            pl.BlockSpec(g2p.shape, lambda b, t: (0, 0)),
        ],
        out_specs=pl.BlockSpec((CHUNK, RWKV_DIM), lambda b, t: (b * nt + t, 0)),
        scratch_shapes=[pltpu.VMEM((N_PAIRS, PAIR, PAIR), F32)],
        compiler_params=_params("parallel", "arbitrary"),
        name="rwkv7_chunk_scan",
    )(p_rw, vecs, w2p, a2p, g2p)


def _dil_attn_kernel(q_ref, kp_ref, kc_ref, vp_ref, vc_ref, o_ref, l_ref, *, dilation, slopes):
    blk = ATTN_BLK
    width = ATTN_GROUP_DIM
    first_key = jnp.where(pl.program_id(2) == 0, blk, 0)
    q = q_ref[...]
    k = jnp.concatenate([kp_ref[...], kc_ref[...]], axis=0)
    v = jnp.concatenate([vp_ref[...], vc_ref[...]], axis=0)
    lane_head = lax.broadcasted_iota(jnp.int32, (blk, width), 1) >> 6
    qs = jnp.concatenate([jnp.where(lane_head == h, q, jnp.zeros_like(q)) for h in range(HEADS_PER_GROUP)],
                         axis=0)
    s = lax.dot_general(qs, k, (((1,), (1,)), ((), ())), preferred_element_type=F32) * (HEAD_DIM ** -0.5)
    shape = (HEADS_PER_GROUP * blk, 2 * blk)
    rowi = lax.broadcasted_iota(jnp.int32, shape, 0)
    ki = lax.broadcasted_iota(jnp.int32, shape, 1)
    steps = (rowi & (blk - 1)) + blk - ki
    valid = (steps >= 0) & (steps <= blk) & (ki >= first_key)
    slope = jnp.full(shape, slopes[0], F32)
    for h in range(1, HEADS_PER_GROUP):
        slope = jnp.where(rowi >= h * blk, slopes[h], slope)
    bias = -slope * (steps * dilation).astype(F32)
    s = jnp.where(valid, s + bias, -jnp.inf)
    mx = jnp.max(s, axis=-1, keepdims=True)
    e = jnp.exp(s - mx)
    den = jnp.sum(e, axis=-1, keepdims=True)
    lse = mx + jnp.log(den)
    prob = (e / den).astype(BF16)
    o = jnp.zeros((blk, width), F32)
    lb = jnp.zeros((blk, width), F32)
    for h in range(HEADS_PER_GROUP):
        oh = jnp.dot(prob[h * blk:(h + 1) * blk], v, preferred_element_type=F32)
        o = jnp.where(lane_head == h, oh, o)
        lb = jnp.where(lane_head == h, lse[h * blk:(h + 1) * blk], lb)
    o_ref[...] = o
    l_ref[...] = lb


def _alibi_slopes(n_heads):
    return [2.0 ** (-8.0 * (h + 1.0) / n_heads) for h in range(n_heads)]


def _dil_attention(qkv, group, batch, seq):
    window, dilation = DIL_GROUPS[group]
    assert window // dilation == ATTN_BLK and seq % window == 0
    rows = seq // dilation
    nblk = rows // ATTN_BLK
    per_row = 3 * ATTN_DIM // ATTN_GROUP_DIM
    view = qkv.reshape(batch, rows, dilation * 3 * ATTN_DIM)
    slopes = tuple(_alibi_slopes(ATTN_HEADS)[group * HEADS_PER_GROUP:(group + 1) * HEADS_PER_GROUP])
    kcol = ATTN_DIM // ATTN_GROUP_DIM + group
    vcol = 2 * ATTN_DIM // ATTN_GROUP_DIM + group
    blk = (None, ATTN_BLK, ATTN_GROUP_DIM)
    prev = lambda i: jnp.maximum(i - 1, 0)
    out_sds = jax.ShapeDtypeStruct((batch, rows, dilation * ATTN_GROUP_DIM), F32)
    o, lse = pl.pallas_call(
        functools.partial(_dil_attn_kernel, dilation=dilation, slopes=slopes),
        out_shape=(out_sds, out_sds),
        grid=(batch, dilation, nblk),
        in_specs=[
            pl.BlockSpec(blk, lambda b, r, i: (b, i, r * per_row + group)),
            pl.BlockSpec(blk, lambda b, r, i: (b, prev(i), r * per_row + kcol)),
            pl.BlockSpec(blk, lambda b, r, i: (b, i, r * per_row + kcol)),
            pl.BlockSpec(blk, lambda b, r, i: (b, prev(i), r * per_row + vcol)),
            pl.BlockSpec(blk, lambda b, r, i: (b, i, r * per_row + vcol)),
        ],
        out_specs=(pl.BlockSpec(blk, lambda b, r, i: (b, i, r)), pl.BlockSpec(blk, lambda b, r, i: (b, i, r))),
        compiler_params=_params("parallel", "parallel", "arbitrary"),
        name=f"dilated_attn_g{group}",
    )(view, view, view, view, view)
    m = batch * seq
    return o.reshape(m, ATTN_GROUP_DIM), lse.reshape(m, ATTN_GROUP_DIM)


def _merge_kernel(n_ref, y_ref, o0_ref, o1_ref, o2_ref, l0_ref, l1_ref, l2_ref, x_ref,
                  wg_ref, pr_ref, pa_ref, wo_ref, h_ref):
    d = x_ref.shape[1]
    gates = jnp.dot(n_ref[...], wg_ref[...], preferred_element_type=F32)
    t_rwkv = jnp.dot(y_ref[...], pr_ref[...], preferred_element_type=F32)
    l0, l1, l2 = l0_ref[...], l1_ref[...], l2_ref[...]
    mx = jnp.maximum(jnp.maximum(l0, l1), l2)
    e0, e1, e2 = jnp.exp(l0 - mx), jnp.exp(l1 - mx), jnp.exp(l2 - mx)
    y_attn = (e0 * o0_ref[...] + e1 * o1_ref[...] + e2 * o2_ref[...]) / (e0 + e1 + e2)
    t_attn = _dot(y_attn, pa_ref[...])
    merged = jax.nn.sigmoid(gates[:, :d]) * t_rwkv + jax.nn.sigmoid(gates[:, d:]) * t_attn
    h_ref[...] = x_ref[...] + _dot(merged, wo_ref[...])


def _merge(n, y_rwkv, attn, x, wg, p_rwkv, p_attn, w_out, tm):
    m, d = x.shape
    row = lambda w: pl.BlockSpec((tm, w), lambda i: (i, 0))
    full = lambda a: pl.BlockSpec(a.shape, lambda i: (0, 0))
    (o0, l0), (o1, l1), (o2, l2) = attn
    return pl.pallas_call(
        _merge_kernel,
        out_shape=jax.ShapeDtypeStruct((m, d), F32),
        grid=(m // tm,),
        in_specs=[row(d), row(d)] + [row(ATTN_GROUP_DIM)] * 6 + [row(d),
                  full(wg), full(p_rwkv), full(p_attn), full(w_out)],
        out_specs=row(d),
        compiler_params=_params("parallel"),
        name="gated_merge",
    )(n, y_rwkv, o0, o1, o2, l0, l1, l2, x, wg, p_rwkv, p_attn, w_out)


def _mem_kv_kernel(mem_ref, g_ref, w_ref, o_ref):
    memn = _rmsnorm_f32(mem_ref[...], g_ref[...])
    o_ref[...] = _dot(memn, w_ref[...]).astype(o_ref.dtype)


def _mem_kv(mem2d, g, w_kv, mem_len):
    m, d = mem2d.shape
    n = w_kv.shape[1]
    return pl.pallas_call(
        _mem_kv_kernel,
        out_shape=jax.ShapeDtypeStruct((m, n), BF16),
        grid=(m // mem_len,),
        in_specs=[pl.BlockSpec((mem_len, d), lambda i: (i, 0)), pl.BlockSpec((1, d), lambda i: (0, 0)),
                  pl.BlockSpec((d, n), lambda i: (0, 0))],
        out_specs=pl.BlockSpec((mem_len, n), lambda i: (i, 0)),
        compiler_params=_params("parallel"),
        name="mem_kv_proj",
    )(mem2d, g.reshape(1, d), w_kv)


def _xattn_kernel(h_ref, g_ref, wq_ref, kv_ref, wo_ref, o_ref):
    d = h_ref.shape[1]
    hd = d // XATTN_HEADS
    h = h_ref[...]
    q = _dot(_rmsnorm_f32(h, g_ref[...]), wq_ref[...])
    outs = []
    for hh in range(XATTN_HEADS):
        cs = slice(hh * hd, (hh + 1) * hd)
        s = _dot_nt(q[:, cs], kv_ref[:, cs]) * (hd ** -0.5)
        e = jnp.exp(s - jnp.max(s, axis=-1, keepdims=True))
        prob = e / jnp.sum(e, axis=-1, keepdims=True)
        outs.append(_dot(prob, kv_ref[:, d + hh * hd:d + (hh + 1) * hd]))
    o_ref[...] = h + _dot(jnp.concatenate(outs, axis=-1), wo_ref[...])


def _xattn(h, g, wq, kv, wo, seq, mem_len, tm):
    m, d = h.shape
    tiles_per_seq = seq // tm
    full = lambda a: pl.BlockSpec(a.shape, lambda i: (0, 0))
    return pl.pallas_call(
        _xattn_kernel,
        out_shape=jax.ShapeDtypeStruct((m, d), F32),
        grid=(m // tm,),
        in_specs=[pl.BlockSpec((tm, d), lambda i: (i, 0)), pl.BlockSpec((1, d), lambda i: (0, 0)), full(wq),
                  pl.BlockSpec((mem_len, 2 * d), lambda i: (i // tiles_per_seq, 0)), full(wo)],
        out_specs=pl.BlockSpec((tm, d), lambda i: (i, 0)),
        compiler_params=_params("parallel"),
        name="mem_cross_attn",
    )(h, g.reshape(1, d), wq, kv, wo)


def _ffn_kernel(h_ref, g_ref, w1_ref, w2_ref, gf_ref, o_ref, xn_ref, acc_ref):
    j = pl.program_id(1)

    @pl.when(j == 0)
    def _first():
        xn_ref[...] = _rmsnorm_f32(h_ref[...], g_ref[...]).astype(BF16)
        acc_ref[...] = jnp.zeros_like(acc_ref)

    u = jnp.dot(xn_ref[...], w1_ref[...], preferred_element_type=F32)
    act = jnp.square(jnp.maximum(u, 0.0)).astype(BF16)
    acc_ref[...] += jnp.dot(act, w2_ref[...], preferred_element_type=F32)

    @pl.when(j == pl.num_programs(1) - 1)
    def _last():
        o_ref[...] = _rmsnorm_f32(h_ref[...] + acc_ref[...], gf_ref[...])


def _ffn(h, g, w1, w2, g_final, tm, tf):
    m, d = h.shape
    f = w1.shape[1]
    return pl.pallas_call(
        _ffn_kernel,
        out_shape=jax.ShapeDtypeStruct((m, d), F32),
        grid=(m // tm, f // tf),
        in_specs=[pl.BlockSpec((tm, d), lambda i, j: (i, 0)), pl.BlockSpec((1, d), lambda i, j: (0, 0)),
                  pl.BlockSpec((d, tf), lambda i, j: (0, j)), pl.BlockSpec((tf, d), lambda i, j: (j, 0)),
                  pl.BlockSpec((1, d), lambda i, j: (0, 0))],
        out_specs=pl.BlockSpec((tm, d), lambda i, j: (i, 0)),
        scratch_shapes=[pltpu.VMEM((tm, d), BF16), pltpu.VMEM((tm, d), F32)],
        compiler_params=_params("parallel", "arbitrary"),
        name="relu2_mlp_final_norm",
    )(h, g.reshape(1, d), w1, w2, g_final.reshape(1, d))


def _pad_cols(a, width):
    return jnp.pad(a, ((0, 0), (0, width - a.shape[1])))


def _pad_rows(a, rows):
    return jnp.pad(a, ((0, rows - a.shape[0]), (0, 0)))


def kernel(x, mem, norm_mix_g, w_in, shift_mu, w0, w2, a0, a2, g2, k_k, k_a, r_k, gn_w, gn_b, p_rwkv, p_attn, w_out, norm_x_g, norm_mem_g, xa_wq, xa_wkv, xa_wo, norm_ffn_g, ffn_w1, ffn_w2, norm_final_g):
    batch, seq, d = x.shape
    mem_len = mem.shape[1]
    assert w_in.shape[0] == 1 and d == RWKV_DIM
    m = batch * seq
    x2 = x.reshape(m, d)

    w = w_in[0]
    c_wd, c_ad, c_gd = 3 * RWKV_DIM, 3 * RWKV_DIM + DECAY_LORA, 3 * RWKV_DIM + DECAY_LORA + AAA_LORA
    c_q = c_gd + GATE_LORA
    c_gate = c_q + 3 * ATTN_DIM
    lora_cols = lambda a: jnp.concatenate(
        [a[:, :c_wd], _pad_cols(a[:, c_wd:c_ad], 128), _pad_cols(a[:, c_ad:c_gd], 128),
         _pad_cols(a[:, c_gd:c_q], 256)], axis=1)
    w_rw = lora_cols(w).astype(BF16)
    mu_rw = lora_cols(shift_mu[0].reshape(1, -1)).astype(F32)
    w_qkv = w[:, c_q:c_gate].astype(BF16)
    w_gate = w[:, c_gate:].astype(BF16)
    vecs = jnp.stack([w0[0], a0[0], k_k[0], k_a[0], r_k[0].reshape(-1), gn_w[0], gn_b[0],
                      jnp.zeros((RWKV_DIM,), F32)]).astype(F32)
    w2p = _pad_rows(w2[0], 128).astype(BF16)
    a2p = _pad_rows(a2[0], 128).astype(BF16)
    g2p = _pad_rows(g2[0], 256).astype(BF16)

    n = _rmsnorm(x2, norm_mix_g[0], BF16, 512)
    p_rw = _proj_shift(n, w_rw, mu_rw, seq, 512, 512)
    qkv = _matmul(n, w_qkv, BF16, 512, 3 * ATTN_DIM // 3, "qkv_proj")
    y_rwkv = _rwkv(p_rw, vecs, w2p, a2p, g2p, batch, seq)
    attn = [_dil_attention(qkv, g, batch, seq) for g in range(len(DIL_GROUPS))]
    h1 = _merge(n, y_rwkv, attn, x2, w_gate, p_rwkv[0].astype(BF16), p_attn[0].astype(BF16),
                w_out[0].astype(BF16), 256)
    kv = _mem_kv(mem.reshape(batch * mem_len, d), norm_mem_g[0], xa_wkv[0].astype(BF16), mem_len)
    h2 = _xattn(h1, norm_x_g[0], xa_wq[0].astype(BF16), kv, xa_wo[0].astype(BF16), seq, mem_len, 512)
    out = _ffn(h2, norm_ffn_g[0], ffn_w1[0].astype(BF16), ffn_w2[0].astype(BF16), norm_final_g, 512, 512)
    return out.reshape(batch, seq, d)
```

```python
import functools

import jax
import jax.numpy as jnp
from jax import lax
from jax.experimental import pallas as pl
from jax.experimental.pallas import tpu as pltpu

F32 = jnp.float32
BF16 = jnp.bfloat16

LANES = 128
HEAD_DIM = 64
PAIR = 2 * HEAD_DIM
CHUNK = 64
RWKV_DIM = 1024
N_PAIRS = RWKV_DIM // PAIR
DECAY_LORA = 64
AAA_LORA = 64
GATE_LORA = 160
LORA_WD_OFF = 3 * RWKV_DIM
LORA_AD_OFF = LORA_WD_OFF + 128
LORA_GD_OFF = LORA_AD_OFF + 128
RW_WIDTH = LORA_GD_OFF + 256
RW_TM = 512
RW_TN = 512
GN_EPS = HEAD_DIM * 1e-5
NORM_EPS = 1e-6
DIL_GROUPS = ((128, 1), (512, 4), (2048, 16))
HEADS_PER_GROUP = 4
ATTN_HEADS = 12
ATTN_GROUP_DIM = HEADS_PER_GROUP * HEAD_DIM
ATTN_GROUP_SLABS = ATTN_GROUP_DIM // LANES
ATTN_DIM = ATTN_HEADS * HEAD_DIM
ATTN_BLK = 128
XATTN_HEADS = 4
VMEM_LIMIT = 56 * 1024 * 1024


def _dot(a, b):
    return jnp.dot(a.astype(BF16), b.astype(BF16), preferred_element_type=F32)


def _dot_nt(a, b):
    return lax.dot_general(a.astype(BF16), b.astype(BF16), (((1,), (1,)), ((), ())),
                           preferred_element_type=F32)


def _dot_tn(a, b):
    return lax.dot_general(a.astype(BF16), b.astype(BF16), (((0,), (0,)), ((), ())),
                           preferred_element_type=F32)


def _params(*sem):
    return pltpu.CompilerParams(dimension_semantics=sem, vmem_limit_bytes=VMEM_LIMIT)


def _rmsnorm_f32(x, g):
    return x * lax.rsqrt(jnp.mean(x * x, axis=-1, keepdims=True) + NORM_EPS) * g


def _rmsnorm_kernel(x_ref, g_ref, o_ref):
    o_ref[...] = _rmsnorm_f32(x_ref[...], g_ref[...]).astype(o_ref.dtype)


def _rmsnorm(x, g, out_dtype, tm):
    m, d = x.shape
    return pl.pallas_call(
        _rmsnorm_kernel,
        out_shape=jax.ShapeDtypeStruct((m, d), out_dtype),
        grid=(m // tm,),
        in_specs=[pl.BlockSpec((tm, d), lambda i: (i, 0)), pl.BlockSpec((1, d), lambda i: (0, 0))],
        out_specs=pl.BlockSpec((tm, d), lambda i: (i, 0)),
        compiler_params=_params("parallel"),
        name="rmsnorm",
    )(x, g.reshape(1, d))


def _slab_proj_kernel(a_ref, w_ref, o_ref):
    res = jnp.dot(a_ref[...], w_ref[...], preferred_element_type=F32)
    for s in range(o_ref.shape[0]):
        o_ref[s] = res[:, s * LANES:(s + 1) * LANES]


def _slab_proj(a, w, tm, tn):
    m, k = a.shape
    n = w.shape[1]
    return pl.pallas_call(
        _slab_proj_kernel,
        out_shape=jax.ShapeDtypeStruct((n // LANES, m, LANES), F32),
        grid=(m // tm, n // tn),
        in_specs=[pl.BlockSpec((tm, k), lambda i, j: (i, 0)), pl.BlockSpec((k, tn), lambda i, j: (0, j))],
        out_specs=pl.BlockSpec((tn // LANES, tm, LANES), lambda i, j: (j, i, 0)),
        compiler_params=_params("parallel", "parallel"),
        name="qkv_proj",
    )(a, w)


VEC_W0, VEC_A0, VEC_KK, VEC_KA, VEC_RK, VEC_GNW, VEC_GNB = range(7)


def _rwkv_chunk(p_ref, vec_ref, w2_ref, a2_ref, g2_ref, y_ref, st_ref):
    L = CHUNK
    pairs = range(N_PAIRS)

    head_a = lax.broadcasted_iota(jnp.int32, (L, PAIR), 1) < HEAD_DIM
    row2 = lax.broadcasted_iota(jnp.int32, (PAIR, PAIR), 0)
    col2 = lax.broadcasted_iota(jnp.int32, (PAIR, PAIR), 1)
    strict = row2 > col2
    incl = row2 >= col2
    eye = row2 == col2
    r3 = lax.broadcasted_iota(jnp.int32, (L, 3 * L), 0)
    c3 = lax.broadcasted_iota(jnp.int32, (L, 3 * L), 1)
    c3 = c3 - jnp.where(c3 >= L, L, 0) - jnp.where(c3 >= 2 * L, L, 0)
    tri3 = jnp.where(r3 >= c3, 1.0, 0.0).astype(BF16)

    def hsum(x):
        sa = jnp.sum(jnp.where(head_a, x, 0.0), axis=-1, keepdims=True)
        sb = jnp.sum(jnp.where(head_a, 0.0, x), axis=-1, keepdims=True)
        return jnp.where(head_a, sa, sb)

    def stack(x):
        return jnp.concatenate([jnp.where(head_a, x, 0.0), jnp.where(head_a, 0.0, x)], axis=0).astype(BF16)

    def sl(x, p):
        return x[:, p * PAIR:(p + 1) * PAIR]

    vec = lambda i: vec_ref[i:i + 1, :]
    pr = p_ref[:, 0:RWKV_DIM]
    pk = p_ref[:, RWKV_DIM:2 * RWKV_DIM]
    pv = p_ref[:, 2 * RWKV_DIM:3 * RWKV_DIM]
    wd = jnp.tanh(p_ref[:, LORA_WD_OFF:LORA_WD_OFF + 128])
    ad = p_ref[:, LORA_AD_OFF:LORA_AD_OFF + 128]
    gd = jax.nn.sigmoid(p_ref[:, LORA_GD_OFF:LORA_GD_OFF + 256])
    z = -(vec(VEC_W0) + _dot(wd, w2_ref[...]))
    softplus = jnp.maximum(z, 0.0) + jnp.log1p(jnp.exp(-jnp.abs(z)))
    lw = -jnp.exp(-softplus - 0.5)
    a_lr = jax.nn.sigmoid(vec(VEC_A0) + _dot(ad, a2_ref[...]))
    gate = _dot(gd, g2_ref[...])
    kk = pk * vec(VEC_KK)
    kbar = pk * (1.0 + (a_lr - 1.0) * vec(VEC_KA))
    h1 = lw.astype(BF16)
    r1 = lw - h1.astype(F32)
    h2 = r1.astype(BF16)
    h3 = (r1 - h2.astype(F32)).astype(BF16)
    c = jnp.dot(tri3, jnp.concatenate([h1, h2, h3], axis=0), preferred_element_type=F32)
    cl = c[L - 1:L, :]
    einv = jnp.exp(-c)
    edec = jnp.exp(cl - c)
    rt = pr * jnp.exp(c)
    eprev = jnp.exp(c - lw)
    gl = jnp.exp(cl)
    rkr = pr * kbar * vec(VEC_RK)

    kkn = [sl(kk, p) / jnp.maximum(jnp.sqrt(hsum(sl(kk, p) * sl(kk, p))), 1e-12) for p in pairs]
    bvec = [kkn[p] * sl(a_lr, p) for p in pairs]
    s_at = [stack(-kkn[p] * sl(eprev, p)) for p in pairs]
    s_rt = [stack(sl(rt, p)) for p in pairs]
    s_bt = [stack(bvec[p] * sl(einv, p)) for p in pairs]
    s_kt = [stack(sl(kbar, p) * sl(einv, p)) for p in pairs]
    s_bh = [stack(bvec[p] * sl(edec, p)) for p in pairs]
    s_kh = [stack(sl(kbar, p) * sl(edec, p)) for p in pairs]
    s_v = [stack(sl(pv, p)) for p in pairs]

    scores = [_dot_nt(jnp.concatenate([s_at[p], s_rt[p]], axis=0), jnp.concatenate([s_bt[p], s_kt[p]], axis=0))
              for p in pairs]
    n_ab = [jnp.where(strict, scores[p][:PAIR, :PAIR], 0.0) for p in pairs]
    a_ak = [jnp.where(strict, scores[p][:PAIR, PAIR:], 0.0) for p in pairs]
    m_rbk = [jnp.concatenate([jnp.where(incl, scores[p][PAIR:, :PAIR], 0.0),
                              jnp.where(incl, scores[p][PAIR:, PAIR:], 0.0)], axis=1) for p in pairs]

    rhs0 = [jnp.concatenate([s_at[p].astype(F32), _dot(a_ak[p], s_v[p])], axis=1) for p in pairs]
    tinv = [jnp.where(eye, 1.0, 0.0) + n_ab[p] for p in pairs]
    npow = [_dot(n_ab[p], n_ab[p]) for p in pairs]
    for _ in range(4):
        prod = [_dot(npow[p], jnp.concatenate([npow[p], tinv[p]], axis=1)) for p in pairs]
        npow = [prod[p][:, :PAIR] for p in pairs]
        tinv = [tinv[p] + prod[p][:, PAIR:] for p in pairs]
    half = [_dot(tinv[p], rhs0[p]) for p in pairs]
    pq = [half[p] + _dot(npow[p], half[p]) for p in pairs]

    rhs = [jnp.concatenate([pq[p], jnp.concatenate([jnp.zeros((PAIR, PAIR), F32), s_v[p].astype(F32)], axis=1)],
                           axis=0) for p in pairs]
    top = [_dot(m_rbk[p], rhs[p]) for p in pairs]
    bot = [_dot_tn(jnp.concatenate([s_bh[p], s_kh[p]], axis=0), rhs[p]) for p in pairs]
    ys = [_dot(jnp.concatenate([s_rt[p].astype(F32) + top[p][:, :PAIR],
                                jnp.where(eye, sl(gl, p), 0.0) + bot[p][:, :PAIR]], axis=0), st_ref[p])
          for p in pairs]
    for p in pairs:
        st_ref[p] = ys[p][PAIR:] + bot[p][:, PAIR:]
    y_st = [ys[p][:PAIR] + top[p][:, PAIR:] for p in pairs]
    y = [y_st[p][:L] + y_st[p][L:] for p in pairs]

    mean = [hsum(y[p]) * (1.0 / HEAD_DIM) for p in pairs]
    dev = [y[p] - mean[p] for p in pairs]
    var = [hsum(dev[p] * dev[p]) * (1.0 / HEAD_DIM) for p in pairs]
    bonus = [hsum(sl(rkr, p)) * sl(pv, p) for p in pairs]
    for p in pairs:
        cs = slice(p * PAIR, (p + 1) * PAIR)
        yn = dev[p] * lax.rsqrt(var[p] + GN_EPS) * vec_ref[VEC_GNW:VEC_GNW + 1, cs] + vec_ref[VEC_GNB:VEC_GNB + 1, cs]
        y_ref[:, cs] = ((yn + bonus[p]) * sl(gate, p)).astype(y_ref.dtype)


def _rwkv_kernel(n_ref, w_ref, mu_ref, vec_ref, w2_ref, a2_ref, g2_ref, y_ref, p_scr, last_ref, st_ref):
    tm = n_ref.shape[0]

    @pl.when(pl.program_id(1) == 0)
    def _init():
        st_ref[...] = jnp.zeros_like(st_ref)
        last_ref[...] = jnp.zeros_like(last_ref)

    n = n_ref[...]
    row = lax.broadcasted_iota(jnp.int32, (tm, RW_TN), 0)
    for j in range(RW_WIDTH // RW_TN):
        cs = slice(j * RW_TN, (j + 1) * RW_TN)
        p = jnp.dot(n, w_ref[:, cs], preferred_element_type=F32)
        prev = jnp.where(row == 0, last_ref[:, cs], pltpu.roll(p, 1, 0))
        last_ref[:, cs] = p[tm - 1:tm, :]
        p_scr[:, cs] = p + mu_ref[:, cs] * (prev - p)

    def chunk(c, carry):
        rows = pl.ds(pl.multiple_of(c * CHUNK, CHUNK), CHUNK)
        _rwkv_chunk(p_scr.at[rows], vec_ref, w2_ref, a2_ref, g2_ref, y_ref.at[rows], st_ref)
        return carry

    lax.fori_loop(0, tm // CHUNK, chunk, 0)


def _rwkv(n, w_rw, mu_rw, vecs, w2p, a2p, g2p, batch, seq):
    m, d = n.shape
    nt = seq // RW_TM
    full = lambda a: pl.BlockSpec(a.shape, lambda b, t: (0, 0))
    return pl.pallas_call(
        _rwkv_kernel,
        out_shape=jax.ShapeDtypeStruct((m, RWKV_DIM), BF16),
        grid=(batch, nt),
        in_specs=[pl.BlockSpec((RW_TM, d), lambda b, t: (b * nt + t, 0)),
                  full(w_rw), full(mu_rw), full(vecs), full(w2p), full(a2p), full(g2p)],
        out_specs=pl.BlockSpec((RW_TM, RWKV_DIM), lambda b, t: (b * nt + t, 0)),
        scratch_shapes=[pltpu.VMEM((RW_TM, RW_WIDTH), F32), pltpu.VMEM((1, RW_WIDTH), F32),
                        pltpu.VMEM((N_PAIRS, PAIR, PAIR), F32)],
        compiler_params=_params("parallel", "arbitrary"),
        name="rwkv7_proj_chunk_scan",
    )(n, w_rw, mu_rw, vecs, w2p, a2p, g2p)


def _dil_attn_kernel(q_ref, kp_ref, kc_ref, vp_ref, vc_ref, o_ref, l_ref, *, dilation, slopes):
    blk = ATTN_BLK
    width = ATTN_GROUP_DIM
    first_key = jnp.where(pl.program_id(1) == 0, blk, 0)
    lane_head = lax.broadcasted_iota(jnp.int32, (blk, width), 1) >> 6
    shape = (HEADS_PER_GROUP * blk, 2 * blk)
    rowi = lax.broadcasted_iota(jnp.int32, shape, 0)
    ki = lax.broadcasted_iota(jnp.int32, shape, 1)
    steps = (rowi & (blk - 1)) + blk - ki
    valid = (steps >= 0) & (steps <= blk) & (ki >= first_key)
    slope = jnp.full(shape, slopes[0], F32)
    for h in range(1, HEADS_PER_GROUP):
        slope = jnp.where(rowi >= h * blk, slopes[h], slope)
    bias = jnp.where(valid, -slope * (steps * dilation).astype(F32), -jnp.inf)

    def residue(r, carry):
        rows = pl.ds(r, blk, stride=dilation)
        load = lambda ref: jnp.concatenate([ref[s, rows, :] for s in range(ATTN_GROUP_SLABS)], axis=1).astype(BF16)
        q = load(q_ref)
        k = jnp.concatenate([load(kp_ref), load(kc_ref)], axis=0)
        v = jnp.concatenate([load(vp_ref), load(vc_ref)], axis=0)
        qs = jnp.concatenate([jnp.where(lane_head == h, q, jnp.zeros_like(q)) for h in range(HEADS_PER_GROUP)],
                             axis=0)
        s = lax.dot_general(qs, k, (((1,), (1,)), ((), ())), preferred_element_type=F32) * (HEAD_DIM ** -0.5)
        s = s + bias
        mx = jnp.max(s, axis=-1, keepdims=True)
        e = jnp.exp(s - mx)
        den = jnp.sum(e, axis=-1, keepdims=True)
        lse = mx + jnp.log(den)
        prob = (e / den).astype(BF16)
        o = jnp.zeros((blk, width), F32)
        lb = jnp.zeros((blk, width), F32)
        for h in range(HEADS_PER_GROUP):
            oh = jnp.dot(prob[h * blk:(h + 1) * blk], v, preferred_element_type=F32)
            o = jnp.where(lane_head == h, oh, o)
            lb = jnp.where(lane_head == h, lse[h * blk:(h + 1) * blk], lb)
        for s_ in range(ATTN_GROUP_SLABS):
            o_ref[s_, rows, :] = o[:, s_ * LANES:(s_ + 1) * LANES]
            l_ref[s_, rows, :] = lb[:, s_ * LANES:(s_ + 1) * LANES]
        return carry

    lax.fori_loop(0, dilation, residue, 0)


def _alibi_slopes(n_heads):
    return [2.0 ** (-8.0 * (h + 1.0) / n_heads) for h in range(n_heads)]


def _dil_attention(qkv, group, batch, seq):
    window, dilation = DIL_GROUPS[group]
    assert window // dilation == ATTN_BLK and seq % window == 0
    m = qkv.shape[1]
    nspan = seq // window
    slopes = tuple(_alibi_slopes(ATTN_HEADS)[group * HEADS_PER_GROUP:(group + 1) * HEADS_PER_GROUP])
    groups = ATTN_DIM // ATTN_GROUP_DIM
    blk = (ATTN_GROUP_SLABS, window, LANES)
    cur = lambda col: pl.BlockSpec(blk, lambda b, i: (col, b * nspan + i, 0))
    prev = lambda col: pl.BlockSpec(blk, lambda b, i: (col, b * nspan + jnp.maximum(i - 1, 0), 0))
    out_sds = jax.ShapeDtypeStruct((ATTN_GROUP_SLABS, m, LANES), F32)
    out_spec = pl.BlockSpec(blk, lambda b, i: (0, b * nspan + i, 0))
    return pl.pallas_call(
        functools.partial(_dil_attn_kernel, dilation=dilation, slopes=slopes),
        out_shape=(out_sds, out_sds),
        grid=(batch, nspan),
        in_specs=[cur(group), prev(groups + group), cur(groups + group),
                  prev(2 * groups + group), cur(2 * groups + group)],
        out_specs=(out_spec, out_spec),
        compiler_params=_params("parallel", "arbitrary"),
        name=f"dilated_attn_g{group}",
    )(qkv, qkv, qkv, qkv, qkv)


def _merge_kernel(n_ref, y_ref, o0_ref, o1_ref, o2_ref, l0_ref, l1_ref, l2_ref, x_ref,
                  wg_ref, pr_ref, pa_ref, wo_ref, h_ref):
    d = x_ref.shape[1]
    gates = jnp.dot(n_ref[...], wg_ref[...], preferred_element_type=F32)
    t_rwkv = jnp.dot(y_ref[...], pr_ref[...], preferred_element_type=F32)
    wide = lambda ref: jnp.concatenate([ref[s] for s in range(ATTN_GROUP_SLABS)], axis=1)
    l0, l1, l2 = wide(l0_ref), wide(l1_ref), wide(l2_ref)
    mx = jnp.maximum(jnp.maximum(l0, l1), l2)
    e0, e1, e2 = jnp.exp(l0 - mx), jnp.exp(l1 - mx), jnp.exp(l2 - mx)
    y_attn = (e0 * wide(o0_ref) + e1 * wide(o1_ref) + e2 * wide(o2_ref)) / (e0 + e1 + e2)
    t_attn = _dot(y_attn, pa_ref[...])
    merged = jax.nn.sigmoid(gates[:, :d]) * t_rwkv + jax.nn.sigmoid(gates[:, d:]) * t_attn
    h_ref[...] = x_ref[...] + _dot(merged, wo_ref[...])


def _merge(n, y_rwkv, attn, x, wg, p_rwkv, p_attn, w_out, tm):
    m, d = x.shape
    row = lambda w: pl.BlockSpec((tm, w), lambda i: (i, 0))
    slab = pl.BlockSpec((ATTN_GROUP_SLABS, tm, LANES), lambda i: (0, i, 0))
    full = lambda a: pl.BlockSpec(a.shape, lambda i: (0, 0))
    (o0, l0), (o1, l1), (o2, l2) = attn
    return pl.pallas_call(
        _merge_kernel,
        out_shape=jax.ShapeDtypeStruct((m, d), F32),
        grid=(m // tm,),
        in_specs=[row(d), row(d)] + [slab] * 6 + [row(d), full(wg), full(p_rwkv), full(p_attn), full(w_out)],
        out_specs=row(d),
        compiler_params=_params("parallel"),
        name="gated_merge",
    )(n, y_rwkv, o0, o1, o2, l0, l1, l2, x, wg, p_rwkv, p_attn, w_out)


def _mem_kv_kernel(mem_ref, g_ref, w_ref, o_ref):
    memn = _rmsnorm_f32(mem_ref[...], g_ref[...])
    o_ref[...] = _dot(memn, w_ref[...]).astype(o_ref.dtype)


def _mem_kv(mem2d, g, w_kv, mem_len):
    m, d = mem2d.shape
    n = w_kv.shape[1]
    return pl.pallas_call(
        _mem_kv_kernel,
        out_shape=jax.ShapeDtypeStruct((m, n), BF16),
        grid=(m // mem_len,),
        in_specs=[pl.BlockSpec((mem_len, d), lambda i: (i, 0)), pl.BlockSpec((1, d), lambda i: (0, 0)),
                  pl.BlockSpec((d, n), lambda i: (0, 0))],
        out_specs=pl.BlockSpec((mem_len, n), lambda i: (i, 0)),
        compiler_params=_params("parallel"),
        name="mem_kv_proj",
    )(mem2d, g.reshape(1, d), w_kv)


def _xattn_kernel(h_ref, g_ref, wq_ref, kv_ref, wo_ref, o_ref):
    d = h_ref.shape[1]
    hd = d // XATTN_HEADS
    h = h_ref[...]
    q = _dot(_rmsnorm_f32(h, g_ref[...]), wq_ref[...])
    outs = []
    for hh in range(XATTN_HEADS):
        cs = slice(hh * hd, (hh + 1) * hd)
        s = _dot_nt(q[:, cs], kv_ref[:, cs]) * (hd ** -0.5)
        e = jnp.exp(s - jnp.max(s, axis=-1, keepdims=True))
        prob = e / jnp.sum(e, axis=-1, keepdims=True)
        outs.append(_dot(prob, kv_ref[:, d + hh * hd:d + (hh + 1) * hd]))
    o_ref[...] = h + _dot(jnp.concatenate(outs, axis=-1), wo_ref[...])


def _xattn(h, g, wq, kv, wo, seq, mem_len, tm):
    m, d = h.shape
    tiles_per_seq = seq // tm
    full = lambda a: pl.BlockSpec(a.shape, lambda i: (0, 0))
    return pl.pallas_call(
        _xattn_kernel,
        out_shape=jax.ShapeDtypeStruct((m, d), F32),
        grid=(m // tm,),
        in_specs=[pl.BlockSpec((tm, d), lambda i: (i, 0)), pl.BlockSpec((1, d), lambda i: (0, 0)), full(wq),
                  pl.BlockSpec((mem_len, 2 * d), lambda i: (i // tiles_per_seq, 0)), full(wo)],
        out_specs=pl.BlockSpec((tm, d), lambda i: (i, 0)),
        compiler_params=_params("parallel"),
        name="mem_cross_attn",
    )(h, g.reshape(1, d), wq, kv, wo)


def _ffn_kernel(h_ref, g_ref, w1_ref, w2_ref, gf_ref, o_ref, xn_ref, acc_ref):
    j = pl.program_id(1)

    @pl.when(j == 0)
    def _first():
        xn_ref[...] = _rmsnorm_f32(h_ref[...], g_ref[...]).astype(BF16)
        acc_ref[...] = jnp.zeros_like(acc_ref)

    u = jnp.dot(xn_ref[...], w1_ref[...], preferred_element_type=F32)
    act = jnp.square(jnp.maximum(u, 0.0)).astype(BF16)
    acc_ref[...] += jnp.dot(act, w2_ref[...], preferred_element_type=F32)

    @pl.when(j == pl.num_programs(1) - 1)
    def _last():
        o_ref[...] = _rmsnorm_f32(h_ref[...] + acc_ref[...], gf_ref[...])


def _ffn(h, g, w1, w2, g_final, tm, tf):
    m, d = h.shape
    f = w1.shape[1]
    return pl.pallas_call(
        _ffn_kernel,
        out_shape=jax.ShapeDtypeStruct((m, d), F32),
        grid=(m // tm, f // tf),
        in_specs=[pl.BlockSpec((tm, d), lambda i, j: (i, 0)), pl.BlockSpec((1, d), lambda i, j: (0, 0)),
                  pl.BlockSpec((d, tf), lambda i, j: (0, j)), pl.BlockSpec((tf, d), lambda i, j: (j, 0)),
                  pl.BlockSpec((1, d), lambda i, j: (0, 0))],
        out_specs=pl.BlockSpec((tm, d), lambda i, j: (i, 0)),
        scratch_shapes=[pltpu.VMEM((tm, d), BF16), pltpu.VMEM((tm, d), F32)],
        compiler_params=_params("parallel", "arbitrary"),
        name="relu2_mlp_final_norm",
    )(h, g.reshape(1, d), w1, w2, g_final.reshape(1, d))


def _pad_cols(a, width):
    return jnp.pad(a, ((0, 0), (0, width - a.shape[1])))


def _pad_rows(a, rows):
    return jnp.pad(a, ((0, rows - a.shape[0]), (0, 0)))


def kernel(x, mem, norm_mix_g, w_in, shift_mu, w0, w2, a0, a2, g2, k_k, k_a, r_k, gn_w, gn_b, p_rwkv, p_attn, w_out, norm_x_g, norm_mem_g, xa_wq, xa_wkv, xa_wo, norm_ffn_g, ffn_w1, ffn_w2, norm_final_g):
    batch, seq, d = x.shape
    mem_len = mem.shape[1]
    assert w_in.shape[0] == 1 and d == RWKV_DIM and seq % RW_TM == 0
    m = batch * seq
    x2 = x.reshape(m, d)

    w = w_in[0]
    c_wd, c_ad, c_gd = 3 * RWKV_DIM, 3 * RWKV_DIM + DECAY_LORA, 3 * RWKV_DIM + DECAY_LORA + AAA_LORA
    c_q = c_gd + GATE_LORA
    c_gate = c_q + 3 * ATTN_DIM
    lora_cols = lambda a: jnp.concatenate(
        [a[:, :c_wd], _pad_cols(a[:, c_wd:c_ad], 128), _pad_cols(a[:, c_ad:c_gd], 128),
         _pad_cols(a[:, c_gd:c_q], 256)], axis=1)
    w_rw = lora_cols(w).astype(BF16)
    mu_rw = lora_cols(shift_mu[0].reshape(1, -1)).astype(F32)
    w_qkv = w[:, c_q:c_gate].astype(BF16)
    w_gate = w[:, c_gate:].astype(BF16)
    vecs = jnp.stack([w0[0], a0[0], k_k[0], k_a[0], r_k[0].reshape(-1), gn_w[0], gn_b[0],
                      jnp.zeros((RWKV_DIM,), F32)]).astype(F32)
    w2p = _pad_rows(w2[0], 128).astype(BF16)
    a2p = _pad_rows(a2[0], 128).astype(BF16)
    g2p = _pad_rows(g2[0], 256).astype(BF16)

    n = _rmsnorm(x2, norm_mix_g[0], BF16, 512)
    y_rwkv = _rwkv(n, w_rw, mu_rw, vecs, w2p, a2p, g2p, batch, seq)
    qkv = _slab_proj(n, w_qkv, 512, ATTN_DIM)
    attn = [_dil_attention(qkv, g, batch, seq) for g in range(len(DIL_GROUPS))]
    h1 = _merge(n, y_rwkv, attn, x2, w_gate, p_rwkv[0].astype(BF16), p_attn[0].astype(BF16),
                w_out[0].astype(BF16), 256)
    kv = _mem_kv(mem.reshape(batch * mem_len, d), norm_mem_g[0], xa_wkv[0].astype(BF16), mem_len)
    h2 = _xattn(h1, norm_x_g[0], xa_wq[0].astype(BF16), kv, xa_wo[0].astype(BF16), seq, mem_len, 512)
    out = _ffn(h2, norm_ffn_g[0], ffn_w1[0].astype(BF16), ffn_w2[0].astype(BF16), norm_final_g, 512, 512)
    return out.reshape(batch, seq, d)
```

```python
import functools

import jax
import jax.numpy as jnp
from jax import lax
from jax.experimental import pallas as pl
from jax.experimental.pallas import tpu as pltpu

F32 = jnp.float32
BF16 = jnp.bfloat16

LANES = 128
HEAD_DIM = 64
PAIR = 2 * HEAD_DIM
CHUNK = 64
RWKV_DIM = 1024
N_PAIRS = RWKV_DIM // PAIR
DECAY_LORA = 64
AAA_LORA = 64
GATE_LORA = 160
LORA_WD_OFF = 3 * RWKV_DIM
LORA_AD_OFF = LORA_WD_OFF + 128
LORA_GD_OFF = LORA_AD_OFF + 128
RW_WIDTH = LORA_GD_OFF + 256
RW_TM = 512
RW_TN = 512
GN_EPS = HEAD_DIM * 1e-5
NORM_EPS = 1e-6
DIL_GROUPS = ((128, 1), (512, 4), (2048, 16))
HEADS_PER_GROUP = 4
ATTN_HEADS = 12
ATTN_GROUP_DIM = HEADS_PER_GROUP * HEAD_DIM
ATTN_GROUP_SLABS = ATTN_GROUP_DIM // LANES
ATTN_DIM = ATTN_HEADS * HEAD_DIM
ATTN_BLK = 128
ATTN_TILE = 1024
XATTN_HEADS = 4
VMEM_LIMIT = 56 * 1024 * 1024


def _dot(a, b):
    return jnp.dot(a.astype(BF16), b.astype(BF16), preferred_element_type=F32)


def _dot_nt(a, b):
    return lax.dot_general(a.astype(BF16), b.astype(BF16), (((1,), (1,)), ((), ())),
                           preferred_element_type=F32)


def _dot_tn(a, b):
    return lax.dot_general(a.astype(BF16), b.astype(BF16), (((0,), (0,)), ((), ())),
                           preferred_element_type=F32)


def _params(*sem):
    return pltpu.CompilerParams(dimension_semantics=sem, vmem_limit_bytes=VMEM_LIMIT)


def _rmsnorm_f32(x, g):
    return x * lax.rsqrt(jnp.mean(x * x, axis=-1, keepdims=True) + NORM_EPS) * g


def _rmsnorm_kernel(x_ref, g_ref, o_ref):
    o_ref[...] = _rmsnorm_f32(x_ref[...], g_ref[...]).astype(o_ref.dtype)


def _rmsnorm(x, g, out_dtype, tm):
    m, d = x.shape
    return pl.pallas_call(
        _rmsnorm_kernel,
        out_shape=jax.ShapeDtypeStruct((m, d), out_dtype),
        grid=(m // tm,),
        in_specs=[pl.BlockSpec((tm, d), lambda i: (i, 0)), pl.BlockSpec((1, d), lambda i: (0, 0))],
        out_specs=pl.BlockSpec((tm, d), lambda i: (i, 0)),
        compiler_params=_params("parallel"),
        name="rmsnorm",
    )(x, g.reshape(1, d))


VEC_W0, VEC_A0, VEC_KK, VEC_KA, VEC_RK, VEC_GNW, VEC_GNB = range(7)


def _rwkv_chunk(p_ref, vec_ref, w2_ref, a2_ref, g2_ref, y_ref, st_ref):
    L = CHUNK
    pairs = range(N_PAIRS)

    head_a = lax.broadcasted_iota(jnp.int32, (L, PAIR), 1) < HEAD_DIM
    row2 = lax.broadcasted_iota(jnp.int32, (PAIR, PAIR), 0)
    col2 = lax.broadcasted_iota(jnp.int32, (PAIR, PAIR), 1)
    strict = row2 > col2
    incl = row2 >= col2
    eye = row2 == col2
    r3 = lax.broadcasted_iota(jnp.int32, (L, 3 * L), 0)
    c3 = lax.broadcasted_iota(jnp.int32, (L, 3 * L), 1)
    c3 = c3 - jnp.where(c3 >= L, L, 0) - jnp.where(c3 >= 2 * L, L, 0)
    tri3 = jnp.where(r3 >= c3, 1.0, 0.0).astype(BF16)

    def hsum(x):
        sa = jnp.sum(jnp.where(head_a, x, 0.0), axis=-1, keepdims=True)
        sb = jnp.sum(jnp.where(head_a, 0.0, x), axis=-1, keepdims=True)
        return jnp.where(head_a, sa, sb)

    def stack(x):
        return jnp.concatenate([jnp.where(head_a, x, 0.0), jnp.where(head_a, 0.0, x)], axis=0).astype(BF16)

    def sl(x, p):
        return x[:, p * PAIR:(p + 1) * PAIR]

    vec = lambda i: vec_ref[i:i + 1, :]
    pr = p_ref[:, 0:RWKV_DIM]
    pk = p_ref[:, RWKV_DIM:2 * RWKV_DIM]
    pv = p_ref[:, 2 * RWKV_DIM:3 * RWKV_DIM]
    wd = jnp.tanh(p_ref[:, LORA_WD_OFF:LORA_WD_OFF + 128])
    ad = p_ref[:, LORA_AD_OFF:LORA_AD_OFF + 128]
    gd = jax.nn.sigmoid(p_ref[:, LORA_GD_OFF:LORA_GD_OFF + 256])
    z = -(vec(VEC_W0) + _dot(wd, w2_ref[...]))
    softplus = jnp.maximum(z, 0.0) + jnp.log1p(jnp.exp(-jnp.abs(z)))
    lw = -jnp.exp(-softplus - 0.5)
    a_lr = jax.nn.sigmoid(vec(VEC_A0) + _dot(ad, a2_ref[...]))
    gate = _dot(gd, g2_ref[...])
    kk = pk * vec(VEC_KK)
    kbar = pk * (1.0 + (a_lr - 1.0) * vec(VEC_KA))
    h1 = lw.astype(BF16)
    r1 = lw - h1.astype(F32)
    h2 = r1.astype(BF16)
    h3 = (r1 - h2.astype(F32)).astype(BF16)
    c = jnp.dot(tri3, jnp.concatenate([h1, h2, h3], axis=0), preferred_element_type=F32)
    cl = c[L - 1:L, :]
    einv = jnp.exp(-c)
    edec = jnp.exp(cl - c)
    rt = pr * jnp.exp(c)
    eprev = jnp.exp(c - lw)
    gl = jnp.exp(cl)
    rkr = pr * kbar * vec(VEC_RK)

    kkn = [sl(kk, p) / jnp.maximum(jnp.sqrt(hsum(sl(kk, p) * sl(kk, p))), 1e-12) for p in pairs]
    bvec = [kkn[p] * sl(a_lr, p) for p in pairs]
    s_at = [stack(-kkn[p] * sl(eprev, p)) for p in pairs]
    s_rt = [stack(sl(rt, p)) for p in pairs]
    s_bt = [stack(bvec[p] * sl(einv, p)) for p in pairs]
    s_kt = [stack(sl(kbar, p) * sl(einv, p)) for p in pairs]
    s_bh = [stack(bvec[p] * sl(edec, p)) for p in pairs]
    s_kh = [stack(sl(kbar, p) * sl(edec, p)) for p in pairs]
    s_v = [stack(sl(pv, p)) for p in pairs]

    scores = [_dot_nt(jnp.concatenate([s_at[p], s_rt[p]], axis=0), jnp.concatenate([s_bt[p], s_kt[p]], axis=0))
              for p in pairs]
    n_ab = [jnp.where(strict, scores[p][:PAIR, :PAIR], 0.0) for p in pairs]
    a_ak = [jnp.where(strict, scores[p][:PAIR, PAIR:], 0.0) for p in pairs]
    m_rbk = [jnp.concatenate([jnp.where(incl, scores[p][PAIR:, :PAIR], 0.0),
                              jnp.where(incl, scores[p][PAIR:, PAIR:], 0.0)], axis=1) for p in pairs]

    rhs0 = [jnp.concatenate([s_at[p].astype(F32), _dot(a_ak[p], s_v[p])], axis=1) for p in pairs]
    tinv = [jnp.where(eye, 1.0, 0.0) + n_ab[p] for p in pairs]
    npow = [_dot(n_ab[p], n_ab[p]) for p in pairs]
    for _ in range(4):
        prod = [_dot(npow[p], jnp.concatenate([npow[p], tinv[p]], axis=1)) for p in pairs]
        npow = [prod[p][:, :PAIR] for p in pairs]
        tinv = [tinv[p] + prod[p][:, PAIR:] for p in pairs]
    half = [_dot(tinv[p], rhs0[p]) for p in pairs]
    pq = [half[p] + _dot(npow[p], half[p]) for p in pairs]

    rhs = [jnp.concatenate([pq[p], jnp.concatenate([jnp.zeros((PAIR, PAIR), F32), s_v[p].astype(F32)], axis=1)],
                           axis=0) for p in pairs]
    top = [_dot(m_rbk[p], rhs[p]) for p in pairs]
    bot = [_dot_tn(jnp.concatenate([s_bh[p], s_kh[p]], axis=0), rhs[p]) for p in pairs]
    ys = [_dot(jnp.concatenate([s_rt[p].astype(F32) + top[p][:, :PAIR],
                                jnp.where(eye, sl(gl, p), 0.0) + bot[p][:, :PAIR]], axis=0), st_ref[p])
          for p in pairs]
    for p in pairs:
        st_ref[p] = ys[p][PAIR:] + bot[p][:, PAIR:]
    y_st = [ys[p][:PAIR] + top[p][:, PAIR:] for p in pairs]
    y = [y_st[p][:L] + y_st[p][L:] for p in pairs]

    mean = [hsum(y[p]) * (1.0 / HEAD_DIM) for p in pairs]
    dev = [y[p] - mean[p] for p in pairs]
    var = [hsum(dev[p] * dev[p]) * (1.0 / HEAD_DIM) for p in pairs]
    bonus = [hsum(sl(rkr, p)) * sl(pv, p) for p in pairs]
    for p in pairs:
        cs = slice(p * PAIR, (p + 1) * PAIR)
        yn = dev[p] * lax.rsqrt(var[p] + GN_EPS) * vec_ref[VEC_GNW:VEC_GNW + 1, cs] + vec_ref[VEC_GNB:VEC_GNB + 1, cs]
        y_ref[:, cs] = ((yn + bonus[p]) * sl(gate, p)).astype(y_ref.dtype)


def _rwkv_kernel(n_ref, w_ref, mu_ref, wqkv_ref, vec_ref, w2_ref, a2_ref, g2_ref, y_ref, qkv_ref,
                 p_scr, last_ref, st_ref):
    tm = n_ref.shape[0]

    @pl.when(pl.program_id(1) == 0)
    def _init():
        st_ref[...] = jnp.zeros_like(st_ref)
        last_ref[...] = jnp.zeros_like(last_ref)

    n = n_ref[...]
    for j in range(wqkv_ref.shape[1] // ATTN_DIM):
        res = jnp.dot(n, wqkv_ref[:, j * ATTN_DIM:(j + 1) * ATTN_DIM], preferred_element_type=F32)
        for s in range(ATTN_DIM // LANES):
            qkv_ref[j * (ATTN_DIM // LANES) + s] = res[:, s * LANES:(s + 1) * LANES]
    row = lax.broadcasted_iota(jnp.int32, (tm, RW_TN), 0)
    for j in range(RW_WIDTH // RW_TN):
        cs = slice(j * RW_TN, (j + 1) * RW_TN)
        p = jnp.dot(n, w_ref[:, cs], preferred_element_type=F32)
        prev = jnp.where(row == 0, last_ref[:, cs], pltpu.roll(p, 1, 0))
        last_ref[:, cs] = p[tm - 1:tm, :]
        p_scr[:, cs] = p + mu_ref[:, cs] * (prev - p)

    def chunk(c, carry):
        rows = pl.ds(pl.multiple_of(c * CHUNK, CHUNK), CHUNK)
        _rwkv_chunk(p_scr.at[rows], vec_ref, w2_ref, a2_ref, g2_ref, y_ref.at[rows], st_ref)
        return carry

    lax.fori_loop(0, tm // CHUNK, chunk, 0, unroll=2)


def _rwkv(n, w_rw, mu_rw, w_qkv, vecs, w2p, a2p, g2p, batch, seq):
    m, d = n.shape
    nt = seq // RW_TM
    slabs = w_qkv.shape[1] // LANES
    full = lambda a: pl.BlockSpec(a.shape, lambda b, t: (0, 0), pipeline_mode=pl.Buffered(1))
    return pl.pallas_call(
        _rwkv_kernel,
        out_shape=(jax.ShapeDtypeStruct((m, RWKV_DIM), BF16), jax.ShapeDtypeStruct((slabs, m, LANES), F32)),
        grid=(batch, nt),
        in_specs=[pl.BlockSpec((RW_TM, d), lambda b, t: (b * nt + t, 0)),
                  full(w_rw), full(mu_rw), full(w_qkv), full(vecs), full(w2p), full(a2p), full(g2p)],
        out_specs=(pl.BlockSpec((RW_TM, RWKV_DIM), lambda b, t: (b * nt + t, 0)),
                   pl.BlockSpec((slabs, RW_TM, LANES), lambda b, t: (0, b * nt + t, 0))),
        scratch_shapes=[pltpu.VMEM((RW_TM, RW_WIDTH), F32), pltpu.VMEM((1, RW_WIDTH), F32),
                        pltpu.VMEM((N_PAIRS, PAIR, PAIR), F32)],
        compiler_params=_params("parallel", "arbitrary"),
        name="rwkv7_proj_chunk_scan",
    )(n, w_rw, mu_rw, w_qkv, vecs, w2p, a2p, g2p)


def _dil_attn_kernel(q_ref, kp_ref, k_ref, vp_ref, v_ref, o_ref, l_ref, *, window, dilation, slopes):
    blk = ATTN_BLK
    width = ATTN_GROUP_DIM
    n_win = q_ref.shape[1] // window
    lane_head = lax.broadcasted_iota(jnp.int32, (blk, width), 1) >> 6
    shape = (HEADS_PER_GROUP * blk, 2 * blk)
    rowi = lax.broadcasted_iota(jnp.int32, shape, 0)
    ki = lax.broadcasted_iota(jnp.int32, shape, 1)
    steps = (rowi & (blk - 1)) + blk - ki
    in_band = (steps >= 0) & (steps <= blk)
    slope = jnp.full(shape, slopes[0], F32)
    for h in range(1, HEADS_PER_GROUP):
        slope = jnp.where(rowi >= h * blk, slopes[h], slope)
    alibi = -slope * (steps * dilation).astype(F32)
    bias = jnp.where(in_band, alibi, -jnp.inf)
    first_key = jnp.where(pl.program_id(1) == 0, blk, 0)
    bias_w0 = jnp.where(in_band & (ki >= first_key), alibi, -jnp.inf)

    def load(ref, rows):
        return jnp.concatenate([ref[s, rows, :] for s in range(ATTN_GROUP_SLABS)], axis=1)

    def attend(q, k, v, b):
        qs = jnp.concatenate([jnp.where(lane_head == h, q, jnp.zeros_like(q)) for h in range(HEADS_PER_GROUP)],
                             axis=0)
        s = lax.dot_general(qs, k, (((1,), (1,)), ((), ())), preferred_element_type=F32) + b
        mx = jnp.max(s, axis=-1, keepdims=True)
        e = jnp.exp(s - mx)
        den = jnp.sum(e, axis=-1, keepdims=True)
        lse = mx + jnp.log(den)
        prob = (e / den).astype(BF16)
        o = jnp.zeros((blk, width), F32)
        lb = jnp.zeros((blk, width), F32)
        for h in range(HEADS_PER_GROUP):
            oh = jnp.dot(prob[h * blk:(h + 1) * blk], v, preferred_element_type=F32)
            o = jnp.where(lane_head == h, oh, o)
            lb = jnp.where(lane_head == h, lse[h * blk:(h + 1) * blk], lb)
        return o, lb

    def residue(r, carry):
        for j in range(n_win):
            rows = pl.ds(j * window + r, blk, stride=dilation)
            q = (load(q_ref, rows) * (HEAD_DIM ** -0.5)).astype(BF16)
            if j == 0:
                prev = pl.ds(r, blk, stride=dilation)
                k = jnp.concatenate([load(kp_ref, prev), load(k_ref, rows)], axis=0).astype(BF16)
                v = jnp.concatenate([load(vp_ref, prev), load(v_ref, rows)], axis=0).astype(BF16)
                o, lb = attend(q, k, v, bias_w0)
            else:
                both = pl.ds((j - 1) * window + r, 2 * blk, stride=dilation)
                o, lb = attend(q, load(k_ref, both).astype(BF16), load(v_ref, both).astype(BF16), bias)
            for s in range(ATTN_GROUP_SLABS):
                o_ref[s, rows, :] = o[:, s * LANES:(s + 1) * LANES]
                l_ref[s, rows, :] = lb[:, s * LANES:(s + 1) * LANES]
        return carry

    if dilation == 1:
        residue(0, 0)
    else:
        lax.fori_loop(0, dilation, residue, 0, unroll=2 if n_win == 1 else 1)


def _alibi_slopes(n_heads):
    return [2.0 ** (-8.0 * (h + 1.0) / n_heads) for h in range(n_heads)]


def _dil_attention(qkv, group, batch, seq):
    window, dilation = DIL_GROUPS[group]
    assert window // dilation == ATTN_BLK and seq % window == 0
    m = qkv.shape[1]
    tile = max(window, ATTN_TILE)
    assert seq % tile == 0
    n_win = tile // window
    ntile = seq // tile
    slopes = tuple(_alibi_slopes(ATTN_HEADS)[group * HEADS_PER_GROUP:(group + 1) * HEADS_PER_GROUP])
    groups = ATTN_DIM // ATTN_GROUP_DIM
    cur = lambda col: pl.BlockSpec((ATTN_GROUP_SLABS, tile, LANES), lambda b, i: (col, b * ntile + i, 0))
    prev = lambda col: pl.BlockSpec((ATTN_GROUP_SLABS, window, LANES),
                                    lambda b, i: (col, jnp.maximum((b * ntile + i) * n_win - 1, 0), 0))
    out_sds = jax.ShapeDtypeStruct((ATTN_GROUP_SLABS, m, LANES), F32)
    out_spec = pl.BlockSpec((ATTN_GROUP_SLABS, tile, LANES), lambda b, i: (0, b * ntile + i, 0))
    return pl.pallas_call(
        functools.partial(_dil_attn_kernel, window=window, dilation=dilation, slopes=slopes),
        out_shape=(out_sds, out_sds),
        grid=(batch, ntile),
        in_specs=[cur(group), prev(groups + group), cur(groups + group),
                  prev(2 * groups + group), cur(2 * groups + group)],
        out_specs=(out_spec, out_spec),
        compiler_params=_params("parallel", "arbitrary"),
        name=f"dilated_attn_g{group}",
    )(qkv, qkv, qkv, qkv, qkv)


def _merge_kernel(n_ref, y_ref, o0_ref, o1_ref, o2_ref, l0_ref, l1_ref, l2_ref, x_ref,
                  wg_ref, pr_ref, pa_ref, wo_ref, h_ref):
    d = x_ref.shape[1]
    gates = jnp.dot(n_ref[...], wg_ref[...], preferred_element_type=F32)
    t_rwkv = jnp.dot(y_ref[...], pr_ref[...], preferred_element_type=F32)
    wide = lambda ref: jnp.concatenate([ref[s] for s in range(ATTN_GROUP_SLABS)], axis=1)
    l0, l1, l2 = wide(l0_ref), wide(l1_ref), wide(l2_ref)
    mx = jnp.maximum(jnp.maximum(l0, l1), l2)
    e0, e1, e2 = jnp.exp(l0 - mx), jnp.exp(l1 - mx), jnp.exp(l2 - mx)
    y_attn = (e0 * wide(o0_ref) + e1 * wide(o1_ref) + e2 * wide(o2_ref)) / (e0 + e1 + e2)
    t_attn = _dot(y_attn, pa_ref[...])
    merged = jax.nn.sigmoid(gates[:, :d]) * t_rwkv + jax.nn.sigmoid(gates[:, d:]) * t_attn
    h_ref[...] = x_ref[...] + _dot(merged, wo_ref[...])


def _merge(n, y_rwkv, attn, x, wg, p_rwkv, p_attn, w_out, tm):
    m, d = x.shape
    row = lambda w: pl.BlockSpec((tm, w), lambda i: (i, 0))
    slab = pl.BlockSpec((ATTN_GROUP_SLABS, tm, LANES), lambda i: (0, i, 0))
    full = lambda a: pl.BlockSpec(a.shape, lambda i: (0, 0))
    (o0, l0), (o1, l1), (o2, l2) = attn
    return pl.pallas_call(
        _merge_kernel,
        out_shape=jax.ShapeDtypeStruct((m, d), F32),
        grid=(m // tm,),
        in_specs=[row(d), row(d)] + [slab] * 6 + [row(d), full(wg), full(p_rwkv), full(p_attn), full(w_out)],
        out_specs=row(d),
        compiler_params=_params("parallel"),
        name="gated_merge",
    )(n, y_rwkv, o0, o1, o2, l0, l1, l2, x, wg, p_rwkv, p_attn, w_out)


def _mem_kv_kernel(mem_ref, g_ref, w_ref, o_ref):
    memn = _rmsnorm_f32(mem_ref[...], g_ref[...])
    o_ref[...] = _dot(memn, w_ref[...]).astype(o_ref.dtype)


def _mem_kv(mem2d, g, w_kv, mem_len):
    m, d = mem2d.shape
    n = w_kv.shape[1]
    return pl.pallas_call(
        _mem_kv_kernel,
        out_shape=jax.ShapeDtypeStruct((m, n), BF16),
        grid=(m // mem_len,),
        in_specs=[pl.BlockSpec((mem_len, d), lambda i: (i, 0)), pl.BlockSpec((1, d), lambda i: (0, 0)),
                  pl.BlockSpec((d, n), lambda i: (0, 0))],
        out_specs=pl.BlockSpec((mem_len, n), lambda i: (i, 0)),
        compiler_params=_params("parallel"),
        name="mem_kv_proj",
    )(mem2d, g.reshape(1, d), w_kv)


def _xattn_kernel(h_ref, g_ref, wq_ref, kv_ref, wo_ref, o_ref):
    d = h_ref.shape[1]
    hd = d // XATTN_HEADS
    h = h_ref[...]
    q = _dot(_rmsnorm_f32(h, g_ref[...]), wq_ref[...])
    outs = []
    for hh in range(XATTN_HEADS):
        cs = slice(hh * hd, (hh + 1) * hd)
        s = _dot_nt(q[:, cs], kv_ref[:, cs]) * (hd ** -0.5)
        e = jnp.exp(s - jnp.max(s, axis=-1, keepdims=True))
        prob = e / jnp.sum(e, axis=-1, keepdims=True)
        outs.append(_dot(prob, kv_ref[:, d + hh * hd:d + (hh + 1) * hd]))
    o_ref[...] = h + _dot(jnp.concatenate(outs, axis=-1), wo_ref[...])


def _xattn(h, g, wq, kv, wo, seq, mem_len, tm):
    m, d = h.shape
    tiles_per_seq = seq // tm
    full = lambda a: pl.BlockSpec(a.shape, lambda i: (0, 0))
    return pl.pallas_call(
        _xattn_kernel,
        out_shape=jax.ShapeDtypeStruct((m, d), F32),
        grid=(m // tm,),
        in_specs=[pl.BlockSpec((tm, d), lambda i: (i, 0)), pl.BlockSpec((1, d), lambda i: (0, 0)), full(wq),
                  pl.BlockSpec((mem_len, 2 * d), lambda i: (i // tiles_per_seq, 0)), full(wo)],
        out_specs=pl.BlockSpec((tm, d), lambda i: (i, 0)),
        compiler_params=_params("parallel"),
        name="mem_cross_attn",
    )(h, g.reshape(1, d), wq, kv, wo)


def _ffn_kernel(h_ref, g_ref, w1_ref, w2_ref, gf_ref, o_ref, xn_ref, acc_ref):
    j = pl.program_id(1)

    @pl.when(j == 0)
    def _first():
        xn_ref[...] = _rmsnorm_f32(h_ref[...], g_ref[...]).astype(BF16)
        acc_ref[...] = jnp.zeros_like(acc_ref)

    u = jnp.dot(xn_ref[...], w1_ref[...], preferred_element_type=F32)
    act = jnp.square(jnp.maximum(u, 0.0)).astype(BF16)
    acc_ref[...] += jnp.dot(act, w2_ref[...], preferred_element_type=F32)

    @pl.when(j == pl.num_programs(1) - 1)
    def _last():
        o_ref[...] = _rmsnorm_f32(h_ref[...] + acc_ref[...], gf_ref[...])


def _ffn(h, g, w1, w2, g_final, tm, tf):
    m, d = h.shape
    f = w1.shape[1]
    return pl.pallas_call(
        _ffn_kernel,
        out_shape=jax.ShapeDtypeStruct((m, d), F32),
        grid=(m // tm, f // tf),
        in_specs=[pl.BlockSpec((tm, d), lambda i, j: (i, 0)), pl.BlockSpec((1, d), lambda i, j: (0, 0)),
                  pl.BlockSpec((d, tf), lambda i, j: (0, j)), pl.BlockSpec((tf, d), lambda i, j: (j, 0)),
                  pl.BlockSpec((1, d), lambda i, j: (0, 0))],
        out_specs=pl.BlockSpec((tm, d), lambda i, j: (i, 0)),
        scratch_shapes=[pltpu.VMEM((tm, d), BF16), pltpu.VMEM((tm, d), F32)],
        compiler_params=_params("parallel", "arbitrary"),
        name="relu2_mlp_final_norm",
    )(h, g.reshape(1, d), w1, w2, g_final.reshape(1, d))


def _pad_cols(a, width):
    return jnp.pad(a, ((0, 0), (0, width - a.shape[1])))


def _pad_rows(a, rows):
    return jnp.pad(a, ((0, rows - a.shape[0]), (0, 0)))


def kernel(x, mem, norm_mix_g, w_in, shift_mu, w0, w2, a0, a2, g2, k_k, k_a, r_k, gn_w, gn_b, p_rwkv, p_attn, w_out, norm_x_g, norm_mem_g, xa_wq, xa_wkv, xa_wo, norm_ffn_g, ffn_w1, ffn_w2, norm_final_g):
    batch, seq, d = x.shape
    mem_len = mem.shape[1]
    assert w_in.shape[0] == 1 and d == RWKV_DIM and seq % RW_TM == 0
    m = batch * seq
    x2 = x.reshape(m, d)

    w = w_in[0]
    c_wd, c_ad, c_gd = 3 * RWKV_DIM, 3 * RWKV_DIM + DECAY_LORA, 3 * RWKV_DIM + DECAY_LORA + AAA_LORA
    c_q = c_gd + GATE_LORA
    c_gate = c_q + 3 * ATTN_DIM
    lora_cols = lambda a: jnp.concatenate(
        [a[:, :c_wd], _pad_cols(a[:, c_wd:c_ad], 128), _pad_cols(a[:, c_ad:c_gd], 128),
         _pad_cols(a[:, c_gd:c_q], 256)], axis=1)
    w_rw = lora_cols(w).astype(BF16)
    mu_rw = lora_cols(shift_mu[0].reshape(1, -1)).astype(F32)
    w_qkv = w[:, c_q:c_gate].astype(BF16)
    w_gate = w[:, c_gate:].astype(BF16)
    vecs = jnp.stack([w0[0], a0[0], k_k[0], k_a[0], r_k[0].reshape(-1), gn_w[0], gn_b[0],
                      jnp.zeros((RWKV_DIM,), F32)]).astype(F32)
    w2p = _pad_rows(w2[0], 128).astype(BF16)
    a2p = _pad_rows(a2[0], 128).astype(BF16)
    g2p = _pad_rows(g2[0], 256).astype(BF16)

    n = _rmsnorm(x2, norm_mix_g[0], BF16, 512)
    y_rwkv, qkv = _rwkv(n, w_rw, mu_rw, w_qkv, vecs, w2p, a2p, g2p, batch, seq)
    attn = [_dil_attention(qkv, g, batch, seq) for g in range(len(DIL_GROUPS))]
    h1 = _merge(n, y_rwkv, attn, x2, w_gate, p_rwkv[0].astype(BF16), p_attn[0].astype(BF16),
                w_out[0].astype(BF16), 256)
    kv = _mem_kv(mem.reshape(batch * mem_len, d), norm_mem_g[0], xa_wkv[0].astype(BF16), mem_len)
    h2 = _xattn(h1, norm_x_g[0], xa_wq[0].astype(BF16), kv, xa_wo[0].astype(BF16), seq, mem_len, 512)
    out = _ffn(h2, norm_ffn_g[0], ffn_w1[0].astype(BF16), ffn_w2[0].astype(BF16), norm_final_g, 1024, 1024)
    return out.reshape(batch, seq, d)
```

```python
import functools
import math

import jax
import jax.numpy as jnp
from jax import lax
from jax.experimental import pallas as pl
from jax.experimental.pallas import tpu as pltpu

F32 = jnp.float32
BF16 = jnp.bfloat16

LANES = 128
HEAD_DIM = 64
PAIR = 2 * HEAD_DIM
CHUNK = 64
RWKV_DIM = 1024
N_PAIRS = RWKV_DIM // PAIR
DECAY_LORA = 64
AAA_LORA = 64
GATE_LORA = 160
LORA_WD_OFF = 3 * RWKV_DIM
LORA_AD_OFF = LORA_WD_OFF + 128
LORA_GD_OFF = LORA_AD_OFF + 128
RW_WIDTH = LORA_GD_OFF + 256
RW_TM = 512
RW_TN = 512
RW_GROUP = 4
GN_EPS = HEAD_DIM * 1e-5
DECAY_SCALE = math.exp(-0.5)
NORM_EPS = 1e-6
DIL_GROUPS = ((128, 1), (512, 4), (2048, 16))
HEADS_PER_GROUP = 4
ATTN_HEADS = 12
ATTN_GROUP_DIM = HEADS_PER_GROUP * HEAD_DIM
ATTN_GROUP_SLABS = ATTN_GROUP_DIM // LANES
ATTN_DIM = ATTN_HEADS * HEAD_DIM
ATTN_BLK = 128
ATTN_TILE = 1024
XATTN_HEADS = 4
VMEM_LIMIT = 56 * 1024 * 1024


def _dot(a, b):
    return jnp.dot(a.astype(BF16), b.astype(BF16), preferred_element_type=F32)


def _dot_nt(a, b):
    return lax.dot_general(a.astype(BF16), b.astype(BF16), (((1,), (1,)), ((), ())),
                           preferred_element_type=F32)


def _dot_tn(a, b):
    return lax.dot_general(a.astype(BF16), b.astype(BF16), (((0,), (0,)), ((), ())),
                           preferred_element_type=F32)


def _params(*sem):
    return pltpu.CompilerParams(dimension_semantics=sem, vmem_limit_bytes=VMEM_LIMIT)


def _rmsnorm_f32(x, g):
    return x * lax.rsqrt(jnp.mean(x * x, axis=-1, keepdims=True) + NORM_EPS) * g


def _rmsnorm_kernel(x_ref, g_ref, o_ref):
    o_ref[...] = _rmsnorm_f32(x_ref[...], g_ref[...]).astype(o_ref.dtype)


def _rmsnorm(x, g, out_dtype, tm):
    m, d = x.shape
    return pl.pallas_call(
        _rmsnorm_kernel,
        out_shape=jax.ShapeDtypeStruct((m, d), out_dtype),
        grid=(m // tm,),
        in_specs=[pl.BlockSpec((tm, d), lambda i: (i, 0)), pl.BlockSpec((1, d), lambda i: (0, 0))],
        out_specs=pl.BlockSpec((tm, d), lambda i: (i, 0)),
        compiler_params=_params("parallel"),
        name="rmsnorm",
    )(x, g.reshape(1, d))


VEC_W0, VEC_A0, VEC_KK, VEC_KA, VEC_RK, VEC_GNW, VEC_GNB = range(7)


def _interleave(*gens):
    live = list(gens)
    while live:
        for g in list(live):
            try:
                next(g)
            except StopIteration:
                live.remove(g)


def _rwkv_phases(vec_ref, w2_ref, a2_ref, g2_ref, st_ref):
    L = CHUNK
    pairs = range(N_PAIRS)
    cat = jnp.concatenate

    head_a = lax.broadcasted_iota(jnp.int32, (L, PAIR), 1) < HEAD_DIM
    eye = (lax.broadcasted_iota(jnp.int32, (PAIR, PAIR), 0) == lax.broadcasted_iota(jnp.int32, (PAIR, PAIR), 1))
    rowh = lax.broadcasted_iota(jnp.int32, (L, PAIR), 0)
    laneh = lax.broadcasted_iota(jnp.int32, (L, PAIR), 1)
    eye_pk = jnp.where((laneh == rowh) | (laneh == rowh + HEAD_DIM), 1.0, 0.0)
    rows_sc = lax.broadcasted_iota(jnp.int32, (2 * L, 2 * PAIR), 0)
    src_sc = lax.broadcasted_iota(jnp.int32, (2 * L, 2 * PAIR), 1) & (L - 1)
    tri_mask = ((rows_sc < L) & (rows_sc > src_sc)) | ((rows_sc >= L) & (rows_sc - L >= src_sc))
    r3 = lax.broadcasted_iota(jnp.int32, (L, 3 * L), 0)
    c3 = lax.broadcasted_iota(jnp.int32, (L, 3 * L), 1)
    c3 = c3 - jnp.where(c3 >= L, L, 0) - jnp.where(c3 >= 2 * L, L, 0)
    tri3 = jnp.where(r3 >= c3, 1.0, 0.0).astype(BF16)
    zb = jnp.zeros((L, PAIR), BF16)
    vec = lambda i: vec_ref[i:i + 1, :]

    def hsum(x):
        sa = jnp.sum(jnp.where(head_a, x, 0.0), axis=-1, keepdims=True)
        sb = jnp.sum(jnp.where(head_a, 0.0, x), axis=-1, keepdims=True)
        return jnp.where(head_a, sa, sb)

    def sl(x, p):
        return x[:, p * PAIR:(p + 1) * PAIR]

    m_a = lambda x: jnp.where(head_a, x, jnp.zeros_like(x))
    m_b = lambda x: jnp.where(head_a, jnp.zeros_like(x), x)

    def heads_rows(*cols):
        return cat([cat([m_a(x) for x in cols], axis=1), cat([m_b(x) for x in cols], axis=1)], axis=0)

    def prep(p_ref, ops):
        pr = p_ref[:, 0:RWKV_DIM]
        pk = p_ref[:, RWKV_DIM:2 * RWKV_DIM]
        pv = p_ref[:, 2 * RWKV_DIM:3 * RWKV_DIM]
        wd = jnp.tanh(p_ref[:, LORA_WD_OFF:LORA_WD_OFF + 128])
        lw = -DECAY_SCALE * jax.nn.sigmoid(vec(VEC_W0) + _dot(wd, w2_ref[...]))
        yield
        a_lr = jax.nn.sigmoid(vec(VEC_A0) + _dot(p_ref[:, LORA_AD_OFF:LORA_AD_OFF + 128], a2_ref[...]))
        yield
        ops["gate"] = _dot(jax.nn.sigmoid(p_ref[:, LORA_GD_OFF:LORA_GD_OFF + 256]), g2_ref[...])
        yield
        kk = pk * vec(VEC_KK)
        kbar = pk * (1.0 + (a_lr - 1.0) * vec(VEC_KA))
        yield
        h1 = lw.astype(BF16)
        r1 = lw - h1.astype(F32)
        h2 = r1.astype(BF16)
        h3 = (r1 - h2.astype(F32)).astype(BF16)
        c = jnp.dot(tri3, cat([h1, h2, h3], axis=0), preferred_element_type=F32)
        yield
        cl = c[L - 1:L, :]
        einv = jnp.exp(-c)
        yield
        edec = jnp.exp(cl - c)
        yield
        rt = pr * jnp.exp(c)
        yield
        eprev = jnp.exp(c - lw)
        ops["gl"] = jnp.exp(cl)
        ops["rkr"] = pr * kbar * vec(VEC_RK)
        ops["rt"] = rt
        ops["pv"] = pv
        yield
        ss = [hsum(sl(kk, p) * sl(kk, p)) for p in pairs]
        yield
        kkn = [sl(kk, p) * lax.rsqrt(jnp.maximum(ss[p], 1e-24)) for p in pairs]
        bvec = [kkn[p] * sl(a_lr, p) for p in pairs]
        yield
        ops["at"] = [(-kkn[p] * sl(eprev, p)).astype(BF16) for p in pairs]
        ops["rtp"] = [sl(rt, p).astype(BF16) for p in pairs]
        yield
        ops["bt"] = [(bvec[p] * sl(einv, p)).astype(BF16) for p in pairs]
        ops["kt"] = [(sl(kbar, p) * sl(einv, p)).astype(BF16) for p in pairs]
        yield
        ops["bh"] = [(bvec[p] * sl(edec, p)).astype(BF16) for p in pairs]
        ops["kh"] = [(sl(kbar, p) * sl(edec, p)).astype(BF16) for p in pairs]
        ops["vp"] = [sl(pv, p).astype(BF16) for p in pairs]

    def main(ops):
        at, rtp, bt, kt, bh, kh, vp = (ops[k] for k in ("at", "rtp", "bt", "kt", "bh", "kh", "vp"))
        sc = [_dot_nt(cat([at[p], rtp[p]], axis=0), cat([m_a(bt[p]), m_b(kt[p]), m_a(kt[p]), m_b(bt[p])], axis=0))
              for p in pairs]
        yield
        sc = [jnp.where(tri_mask, sc[p], 0.0) for p in pairs]
        npk = [jnp.where(head_a, sc[p][:L, :PAIR], sc[p][:L, PAIR:]) for p in pairs]
        aak = [jnp.where(head_a, sc[p][:L, PAIR:], sc[p][:L, :PAIR]) for p in pairs]
        bot_sc = [sc[p][L:] for p in pairs]
        yield
        aakv = [_dot(aak[p], heads_rows(vp[p])).astype(BF16) for p in pairs]
        yield
        tpk = [eye_pk for p in pairs]
        for _ in range(6):
            res = [_dot(npk[p], heads_rows(npk[p].astype(BF16), tpk[p].astype(BF16))) for p in pairs]
            npk = [res[p][:, :PAIR] for p in pairs]
            tpk = [tpk[p] + res[p][:, PAIR:] for p in pairs]
            yield
        pq = [_dot(tpk[p], heads_rows(at[p], aakv[p])) for p in pairs]
        yield
        pb = [pq[p][:, :PAIR].astype(BF16) for p in pairs]
        qb = [pq[p][:, PAIR:].astype(BF16) for p in pairs]
        rhs = [cat([cat([m_a(pb[p]), m_a(qb[p])], axis=1), cat([zb, m_b(vp[p])], axis=1),
                    cat([zb, m_a(vp[p])], axis=1), cat([m_b(pb[p]), m_b(qb[p])], axis=1)], axis=0) for p in pairs]
        top = [_dot(bot_sc[p], rhs[p]) for p in pairs]
        yield
        bot = [_dot_tn(cat([m_a(bh[p]), m_b(kh[p]), m_a(kh[p]), m_b(bh[p])], axis=0), rhs[p]) for p in pairs]
        yield
        ys = [_dot(cat([sl(ops["rt"], p) + top[p][:, :PAIR],
                        jnp.where(eye, sl(ops["gl"], p), 0.0) + bot[p][:, :PAIR]], axis=0), st_ref[p])
              for p in pairs]
        yield
        for p in pairs:
            st_ref[p] = ys[p][L:] + bot[p][:, PAIR:]
        ops["y"] = [ys[p][:L] + top[p][:, PAIR:] for p in pairs]

    def norm(ops, y_ref):
        y = ops["y"]
        mean = [hsum(y[p]) * (1.0 / HEAD_DIM) for p in pairs]
        yield
        dev = [y[p] - mean[p] for p in pairs]
        var = [hsum(dev[p] * dev[p]) * (1.0 / HEAD_DIM) for p in pairs]
        yield
        bonus = [hsum(sl(ops["rkr"], p)) * sl(ops["pv"], p) for p in pairs]
        yield
        for p in pairs:
            cs = slice(p * PAIR, (p + 1) * PAIR)
            yn = (dev[p] * lax.rsqrt(var[p] + GN_EPS) * vec_ref[VEC_GNW:VEC_GNW + 1, cs]
                  + vec_ref[VEC_GNB:VEC_GNB + 1, cs])
            y_ref[:, cs] = ((yn + bonus[p]) * sl(ops["gate"], p)).astype(y_ref.dtype)
            if p % 2 == 1:
                yield

    return prep, main, norm


def _rwkv_kernel(n_ref, w_ref, mu_ref, wqkv_ref, vec_ref, w2_ref, a2_ref, g2_ref, y_ref, qkv_ref,
                 p_scr, last_ref, st_ref):
    tm = n_ref.shape[0]

    @pl.when(pl.program_id(1) == 0)
    def _init():
        st_ref[...] = jnp.zeros_like(st_ref)
        last_ref[...] = jnp.zeros_like(last_ref)

    n = n_ref[...]
    for j in range(wqkv_ref.shape[1] // ATTN_DIM):
        res = jnp.dot(n, wqkv_ref[:, j * ATTN_DIM:(j + 1) * ATTN_DIM], preferred_element_type=F32)
        for s in range(ATTN_DIM // LANES):
            qkv_ref[j * (ATTN_DIM // LANES) + s] = res[:, s * LANES:(s + 1) * LANES]
    row = lax.broadcasted_iota(jnp.int32, (tm, RW_TN), 0)
    for j in range(RW_WIDTH // RW_TN):
        cs = slice(j * RW_TN, (j + 1) * RW_TN)
        p = jnp.dot(n, w_ref[:, cs], preferred_element_type=F32)
        prev = jnp.where(row == 0, last_ref[:, cs], pltpu.roll(p, 1, 0))
        last_ref[:, cs] = p[tm - 1:tm, :]
        p_scr[:, cs] = p + mu_ref[:, cs] * (prev - p)

    prep, main, norm = _rwkv_phases(vec_ref, w2_ref, a2_ref, g2_ref, st_ref)

    def group(g, carry):
        base = g * (RW_GROUP * CHUNK)
        rows = [pl.ds(pl.multiple_of(base + u * CHUNK, CHUNK), CHUNK) for u in range(RW_GROUP)]
        ops = [{} for _ in range(RW_GROUP)]
        _interleave(prep(p_scr.at[rows[0]], ops[0]))
        for u in range(RW_GROUP):
            gens = [main(ops[u])]
            if u + 1 < RW_GROUP:
                gens.append(prep(p_scr.at[rows[u + 1]], ops[u + 1]))
            if u > 0:
                gens.append(norm(ops[u - 1], y_ref.at[rows[u - 1]]))
            _interleave(*gens)
        _interleave(norm(ops[RW_GROUP - 1], y_ref.at[rows[RW_GROUP - 1]]))
        return carry

    lax.fori_loop(0, tm // (RW_GROUP * CHUNK), group, 0)


def _rwkv(n, w_rw, mu_rw, w_qkv, vecs, w2p, a2p, g2p, batch, seq):
    m, d = n.shape
    nt = seq // RW_TM
    slabs = w_qkv.shape[1] // LANES
    full = lambda a: pl.BlockSpec(a.shape, lambda b, t: (0, 0), pipeline_mode=pl.Buffered(1))
    return pl.pallas_call(
        _rwkv_kernel,
        out_shape=(jax.ShapeDtypeStruct((m, RWKV_DIM), BF16), jax.ShapeDtypeStruct((slabs, m, LANES), F32)),
        grid=(batch, nt),
        in_specs=[pl.BlockSpec((RW_TM, d), lambda b, t: (b * nt + t, 0)),
                  full(w_rw), full(mu_rw), full(w_qkv), full(vecs), full(w2p), full(a2p), full(g2p)],
        out_specs=(pl.BlockSpec((RW_TM, RWKV_DIM), lambda b, t: (b * nt + t, 0)),
                   pl.BlockSpec((slabs, RW_TM, LANES), lambda b, t: (0, b * nt + t, 0))),
        scratch_shapes=[pltpu.VMEM((RW_TM, RW_WIDTH), F32), pltpu.VMEM((1, RW_WIDTH), F32),
                        pltpu.VMEM((N_PAIRS, PAIR, PAIR), F32)],
        compiler_params=_params("parallel", "arbitrary"),
        name="rwkv7_proj_chunk_scan",
    )(n, w_rw, mu_rw, w_qkv, vecs, w2p, a2p, g2p)


def _dil_attn_kernel(q_ref, kp_ref, k_ref, vp_ref, v_ref, o_ref, l_ref, *, window, dilation, slopes):
    blk = ATTN_BLK
    width = ATTN_GROUP_DIM
    n_win = q_ref.shape[1] // window
    lane_head = lax.broadcasted_iota(jnp.int32, (blk, width), 1) >> 6
    shape = (HEADS_PER_GROUP * blk, 2 * blk)
    rowi = lax.broadcasted_iota(jnp.int32, shape, 0)
    ki = lax.broadcasted_iota(jnp.int32, shape, 1)
    steps = (rowi & (blk - 1)) + blk - ki
    in_band = (steps >= 0) & (steps <= blk)
    slope = jnp.full(shape, slopes[0], F32)
    for h in range(1, HEADS_PER_GROUP):
        slope = jnp.where(rowi >= h * blk, slopes[h], slope)
    alibi = -slope * (steps * dilation).astype(F32)
    bias = jnp.where(in_band, alibi, -jnp.inf)
    first_key = jnp.where(pl.program_id(1) == 0, blk, 0)
    bias_w0 = jnp.where(in_band & (ki >= first_key), alibi, -jnp.inf)

    def load(ref, rows):
        return jnp.concatenate([ref[s, rows, :] for s in range(ATTN_GROUP_SLABS)], axis=1)

    def attend(q, k, v, b):
        qs = jnp.concatenate([jnp.where(lane_head == h, q, jnp.zeros_like(q)) for h in range(HEADS_PER_GROUP)],
                             axis=0)
        s = lax.dot_general(qs, k, (((1,), (1,)), ((), ())), preferred_element_type=F32) + b
        mx = jnp.max(s, axis=-1, keepdims=True)
        e = jnp.exp(s - mx)
        den = jnp.sum(e, axis=-1, keepdims=True)
        lse = mx + jnp.log(den)
        prob = (e / den).astype(BF16)
        o = jnp.zeros((blk, width), F32)
        lb = jnp.zeros((blk, width), F32)
        for h in range(HEADS_PER_GROUP):
            oh = jnp.dot(prob[h * blk:(h + 1) * blk], v, preferred_element_type=F32)
            o = jnp.where(lane_head == h, oh, o)
            lb = jnp.where(lane_head == h, lse[h * blk:(h + 1) * blk], lb)
        return o, lb

    def residue(r, carry):
        for j in range(n_win):
            rows = pl.ds(j * window + r, blk, stride=dilation)
            q = (load(q_ref, rows) * (HEAD_DIM ** -0.5)).astype(BF16)
            if j == 0:
                prev = pl.ds(r, blk, stride=dilation)
                k = jnp.concatenate([load(kp_ref, prev), load(k_ref, rows)], axis=0).astype(BF16)
                v = jnp.concatenate([load(vp_ref, prev), load(v_ref, rows)], axis=0).astype(BF16)
                o, lb = attend(q, k, v, bias_w0)
            else:
                both = pl.ds((j - 1) * window + r, 2 * blk, stride=dilation)
                o, lb = attend(q, load(k_ref, both).astype(BF16), load(v_ref, both).astype(BF16), bias)
            for s in range(ATTN_GROUP_SLABS):
                o_ref[s, rows, :] = o[:, s * LANES:(s + 1) * LANES]
                l_ref[s, rows, :] = lb[:, s * LANES:(s + 1) * LANES]
        return carry

    if dilation == 1:
        residue(0, 0)
    else:
        lax.fori_loop(0, dilation, residue, 0, unroll=2)


def _alibi_slopes(n_heads):
    return [2.0 ** (-8.0 * (h + 1.0) / n_heads) for h in range(n_heads)]


def _dil_attention(qkv, group, batch, seq):
    window, dilation = DIL_GROUPS[group]
    assert window // dilation == ATTN_BLK and seq % window == 0
    m = qkv.shape[1]
    tile = max(window, ATTN_TILE)
    assert seq % tile == 0
    n_win = tile // window
    ntile = seq // tile
    slopes = tuple(_alibi_slopes(ATTN_HEADS)[group * HEADS_PER_GROUP:(group + 1) * HEADS_PER_GROUP])
    groups = ATTN_DIM // ATTN_GROUP_DIM
    cur = lambda col: pl.BlockSpec((ATTN_GROUP_SLABS, tile, LANES), lambda b, i: (col, b * ntile + i, 0))
    prev = lambda col: pl.BlockSpec((ATTN_GROUP_SLABS, window, LANES),
                                    lambda b, i: (col, jnp.maximum((b * ntile + i) * n_win - 1, 0), 0))
    out_sds = jax.ShapeDtypeStruct((ATTN_GROUP_SLABS, m, LANES), F32)
    out_spec = pl.BlockSpec((ATTN_GROUP_SLABS, tile, LANES), lambda b, i: (0, b * ntile + i, 0))
    return pl.pallas_call(
        functools.partial(_dil_attn_kernel, window=window, dilation=dilation, slopes=slopes),
        out_shape=(out_sds, out_sds),
        grid=(batch, ntile),
        in_specs=[cur(group), prev(groups + group), cur(groups + group),
                  prev(2 * groups + group), cur(2 * groups + group)],
        out_specs=(out_spec, out_spec),
        compiler_params=_params("parallel", "arbitrary"),
        name=f"dilated_attn_g{group}",
    )(qkv, qkv, qkv, qkv, qkv)


def _merge_kernel(n_ref, y_ref, o0_ref, o1_ref, o2_ref, l0_ref, l1_ref, l2_ref, x_ref,
                  wg_ref, pr_ref, pa_ref, wo_ref, h_ref):
    d = x_ref.shape[1]
    gates = jnp.dot(n_ref[...], wg_ref[...], preferred_element_type=F32)
    t_rwkv = jnp.dot(y_ref[...], pr_ref[...], preferred_element_type=F32)
    wide = lambda ref: jnp.concatenate([ref[s] for s in range(ATTN_GROUP_SLABS)], axis=1)
    l0, l1, l2 = wide(l0_ref), wide(l1_ref), wide(l2_ref)
    mx = jnp.maximum(jnp.maximum(l0, l1), l2)
    e0, e1, e2 = jnp.exp(l0 - mx), jnp.exp(l1 - mx), jnp.exp(l2 - mx)
    y_attn = (e0 * wide(o0_ref) + e1 * wide(o1_ref) + e2 * wide(o2_ref)) / (e0 + e1 + e2)
    t_attn = _dot(y_attn, pa_ref[...])
    merged = jax.nn.sigmoid(gates[:, :d]) * t_rwkv + jax.nn.sigmoid(gates[:, d:]) * t_attn
    h_ref[...] = x_ref[...] + _dot(merged, wo_ref[...])


def _merge(n, y_rwkv, attn, x, wg, p_rwkv, p_attn, w_out, tm):
    m, d = x.shape
    row = lambda w: pl.BlockSpec((tm, w), lambda i: (i, 0))
    slab = pl.BlockSpec((ATTN_GROUP_SLABS, tm, LANES), lambda i: (0, i, 0))
    full = lambda a: pl.BlockSpec(a.shape, lambda i: (0, 0), pipeline_mode=pl.Buffered(1))
    (o0, l0), (o1, l1), (o2, l2) = attn
    return pl.pallas_call(
        _merge_kernel,
        out_shape=jax.ShapeDtypeStruct((m, d), F32),
        grid=(m // tm,),
        in_specs=[row(d), row(d)] + [slab] * 6 + [row(d), full(wg), full(p_rwkv), full(p_attn), full(w_out)],
        out_specs=row(d),
        compiler_params=_params("parallel"),
        name="gated_merge",
    )(n, y_rwkv, o0, o1, o2, l0, l1, l2, x, wg, p_rwkv, p_attn, w_out)


def _mem_kv_kernel(mem_ref, g_ref, w_ref, o_ref):
    memn = _rmsnorm_f32(mem_ref[...], g_ref[...])
    o_ref[...] = _dot(memn, w_ref[...]).astype(o_ref.dtype)


def _mem_kv(mem2d, g, w_kv, mem_len):
    m, d = mem2d.shape
    n = w_kv.shape[1]
    return pl.pallas_call(
        _mem_kv_kernel,
        out_shape=jax.ShapeDtypeStruct((m, n), BF16),
        grid=(m // mem_len,),
        in_specs=[pl.BlockSpec((mem_len, d), lambda i: (i, 0)), pl.BlockSpec((1, d), lambda i: (0, 0)),
                  pl.BlockSpec((d, n), lambda i: (0, 0))],
        out_specs=pl.BlockSpec((mem_len, n), lambda i: (i, 0)),
        compiler_params=_params("parallel"),
        name="mem_kv_proj",
    )(mem2d, g.reshape(1, d), w_kv)


def _xattn_kernel(h_ref, g_ref, wq_ref, kv_ref, wo_ref, o_ref):
    d = h_ref.shape[1]
    hd = d // XATTN_HEADS
    h = h_ref[...]
    q = _dot(_rmsnorm_f32(h, g_ref[...]), wq_ref[...])
    outs = []
    for hh in range(XATTN_HEADS):
        cs = slice(hh * hd, (hh + 1) * hd)
        s = _dot_nt(q[:, cs], kv_ref[:, cs]) * (hd ** -0.5)
        e = jnp.exp(s - jnp.max(s, axis=-1, keepdims=True))
        prob = e / jnp.sum(e, axis=-1, keepdims=True)
        outs.append(_dot(prob, kv_ref[:, d + hh * hd:d + (hh + 1) * hd]))
    o_ref[...] = h + _dot(jnp.concatenate(outs, axis=-1), wo_ref[...])


def _xattn(h, g, wq, kv, wo, seq, mem_len, tm):
    m, d = h.shape
    tiles_per_seq = seq // tm
    full = lambda a: pl.BlockSpec(a.shape, lambda i: (0, 0), pipeline_mode=pl.Buffered(1))
    return pl.pallas_call(
        _xattn_kernel,
        out_shape=jax.ShapeDtypeStruct((m, d), F32),
        grid=(m // tm,),
        in_specs=[pl.BlockSpec((tm, d), lambda i: (i, 0)), pl.BlockSpec((1, d), lambda i: (0, 0)), full(wq),
                  pl.BlockSpec((mem_len, 2 * d), lambda i: (i // tiles_per_seq, 0)), full(wo)],
        out_specs=pl.BlockSpec((tm, d), lambda i: (i, 0)),
        compiler_params=_params("parallel"),
        name="mem_cross_attn",
    )(h, g.reshape(1, d), wq, kv, wo)


def _ffn_kernel(h_ref, g_ref, w1_ref, w2_ref, gf_ref, o_ref, xn_ref, acc_ref):
    j = pl.program_id(1)

    @pl.when(j == 0)
    def _first():
        xn_ref[...] = _rmsnorm_f32(h_ref[...], g_ref[...]).astype(BF16)
        acc_ref[...] = jnp.zeros_like(acc_ref)

    u = jnp.dot(xn_ref[...], w1_ref[...], preferred_element_type=F32)
    act = jnp.square(jnp.maximum(u, 0.0)).astype(BF16)
    acc_ref[...] += jnp.dot(act, w2_ref[...], preferred_element_type=F32)

    @pl.when(j == pl.num_programs(1) - 1)
    def _last():
        o_ref[...] = _rmsnorm_f32(h_ref[...] + acc_ref[...], gf_ref[...])


def _ffn(h, g, w1, w2, g_final, tm, tf):
    m, d = h.shape
    f = w1.shape[1]
    return pl.pallas_call(
        _ffn_kernel,
        out_shape=jax.ShapeDtypeStruct((m, d), F32),
        grid=(m // tm, f // tf),
        in_specs=[pl.BlockSpec((tm, d), lambda i, j: (i, 0)), pl.BlockSpec((1, d), lambda i, j: (0, 0)),
                  pl.BlockSpec((d, tf), lambda i, j: (0, j)), pl.BlockSpec((tf, d), lambda i, j: (j, 0)),
                  pl.BlockSpec((1, d), lambda i, j: (0, 0))],
        out_specs=pl.BlockSpec((tm, d), lambda i, j: (i, 0)),
        scratch_shapes=[pltpu.VMEM((tm, d), BF16), pltpu.VMEM((tm, d), F32)],
        compiler_params=_params("parallel", "arbitrary"),
        name="relu2_mlp_final_norm",
    )(h, g.reshape(1, d), w1, w2, g_final.reshape(1, d))


def _pad_cols(a, width):
    return jnp.pad(a, ((0, 0), (0, width - a.shape[1])))


def _pad_rows(a, rows):
    return jnp.pad(a, ((0, rows - a.shape[0]), (0, 0)))


def kernel(x, mem, norm_mix_g, w_in, shift_mu, w0, w2, a0, a2, g2, k_k, k_a, r_k, gn_w, gn_b, p_rwkv, p_attn, w_out, norm_x_g, norm_mem_g, xa_wq, xa_wkv, xa_wo, norm_ffn_g, ffn_w1, ffn_w2, norm_final_g):
    batch, seq, d = x.shape
    mem_len = mem.shape[1]
    assert w_in.shape[0] == 1 and d == RWKV_DIM and seq % RW_TM == 0
    m = batch * seq
    x2 = x.reshape(m, d)

    w = w_in[0]
    c_wd, c_ad, c_gd = 3 * RWKV_DIM, 3 * RWKV_DIM + DECAY_LORA, 3 * RWKV_DIM + DECAY_LORA + AAA_LORA
    c_q = c_gd + GATE_LORA
    c_gate = c_q + 3 * ATTN_DIM
    lora_cols = lambda a: jnp.concatenate(
        [a[:, :c_wd], _pad_cols(a[:, c_wd:c_ad], 128), _pad_cols(a[:, c_ad:c_gd], 128),
         _pad_cols(a[:, c_gd:c_q], 256)], axis=1)
    w_rw = lora_cols(w).astype(BF16)
    mu_rw = lora_cols(shift_mu[0].reshape(1, -1)).astype(F32)
    w_qkv = w[:, c_q:c_gate].astype(BF16)
    w_gate = w[:, c_gate:].astype(BF16)
    vecs = jnp.stack([w0[0], a0[0], k_k[0], k_a[0], r_k[0].reshape(-1), gn_w[0], gn_b[0],
                      jnp.zeros((RWKV_DIM,), F32)]).astype(F32)
    w2p = _pad_rows(w2[0], 128).astype(BF16)
    a2p = _pad_rows(a2[0], 128).astype(BF16)
    g2p = _pad_rows(g2[0], 256).astype(BF16)

    n = _rmsnorm(x2, norm_mix_g[0], BF16, 512)
    y_rwkv, qkv = _rwkv(n, w_rw, mu_rw, w_qkv, vecs, w2p, a2p, g2p, batch, seq)
    attn = [_dil_attention(qkv, g, batch, seq) for g in range(len(DIL_GROUPS))]
    h1 = _merge(n, y_rwkv, attn, x2, w_gate, p_rwkv[0].astype(BF16), p_attn[0].astype(BF16),
                w_out[0].astype(BF16), 512)
    kv = _mem_kv(mem.reshape(batch * mem_len, d), norm_mem_g[0], xa_wkv[0].astype(BF16), mem_len)
    h2 = _xattn(h1, norm_x_g[0], xa_wq[0].astype(BF16), kv, xa_wo[0].astype(BF16), seq, mem_len, 512)
    out = _ffn(h2, norm_ffn_g[0], ffn_w1[0].astype(BF16), ffn_w2[0].astype(BF16), norm_final_g, 1024, 1024)
    return out.reshape(batch, seq, d)
```

```python
import functools
import math

import jax
import jax.numpy as jnp
from jax import lax
from jax.experimental import pallas as pl
from jax.experimental.pallas import tpu as pltpu

F32 = jnp.float32
BF16 = jnp.bfloat16

LANES = 128
HEAD_DIM = 64
PAIR = 2 * HEAD_DIM
CHUNK = 64
RWKV_DIM = 1024
N_PAIRS = RWKV_DIM // PAIR
DECAY_LORA = 64
AAA_LORA = 64
GATE_LORA = 160
LORA_WD_OFF = 3 * RWKV_DIM
LORA_AD_OFF = LORA_WD_OFF + 128
LORA_GD_OFF = LORA_AD_OFF + 128
RW_WIDTH = LORA_GD_OFF + 256
RW_TM = 256
RW_TN = 256
RW_BG_EVERY = 2
GN_EPS = HEAD_DIM * 1e-5
DECAY_SCALE = math.exp(-0.5)
NORM_EPS = 1e-6
DIL_GROUPS = ((128, 1), (512, 4), (2048, 16))
HEADS_PER_GROUP = 4
ATTN_HEADS = 12
ATTN_GROUP_DIM = HEADS_PER_GROUP * HEAD_DIM
ATTN_GROUP_SLABS = ATTN_GROUP_DIM // LANES
ATTN_DIM = ATTN_HEADS * HEAD_DIM
ATTN_BLK = 128
ATTN_TILE = 1024
XATTN_HEADS = 4
VMEM_LIMIT = 56 * 1024 * 1024


def _dot(a, b):
    return jnp.dot(a.astype(BF16), b.astype(BF16), preferred_element_type=F32)


def _dot_nt(a, b):
    return lax.dot_general(a.astype(BF16), b.astype(BF16), (((1,), (1,)), ((), ())),
                           preferred_element_type=F32)


def _dot_tn(a, b):
    return lax.dot_general(a.astype(BF16), b.astype(BF16), (((0,), (0,)), ((), ())),
                           preferred_element_type=F32)


def _params(*sem):
    return pltpu.CompilerParams(dimension_semantics=sem, vmem_limit_bytes=VMEM_LIMIT)


def _rmsnorm_f32(x, g):
    return x * lax.rsqrt(jnp.mean(x * x, axis=-1, keepdims=True) + NORM_EPS) * g


def _rmsnorm_kernel(x_ref, g_ref, o_ref):
    o_ref[...] = _rmsnorm_f32(x_ref[...], g_ref[...]).astype(o_ref.dtype)


def _rmsnorm(x, g, out_dtype, tm):
    m, d = x.shape
    return pl.pallas_call(
        _rmsnorm_kernel,
        out_shape=jax.ShapeDtypeStruct((m, d), out_dtype),
        grid=(m // tm,),
        in_specs=[pl.BlockSpec((tm, d), lambda i: (i, 0)), pl.BlockSpec((1, d), lambda i: (0, 0))],
        out_specs=pl.BlockSpec((tm, d), lambda i: (i, 0)),
        compiler_params=_params("parallel"),
        name="rmsnorm",
    )(x, g.reshape(1, d))


VEC_W0, VEC_A0, VEC_KK, VEC_KA, VEC_RK, VEC_GNW, VEC_GNB = range(7)


def _interleave(*gens, background=None):
    live = list(gens)
    while live:
        for g in list(live):
            try:
                next(g)
            except StopIteration:
                live.remove(g)
        if background is not None:
            next(background, None)


def _every(gen, n):
    for _ in gen:
        for _ in range(n):
            yield


def _delayed(gen, steps):
    for _ in range(steps):
        yield
    yield from gen


def _rwkv_phases(vec_ref, w2_ref, a2_ref, g2_ref, st_ref):
    L = CHUNK
    pairs = range(N_PAIRS)
    cat = jnp.concatenate

    head_a = lax.broadcasted_iota(jnp.int32, (L, PAIR), 1) < HEAD_DIM
    eye = (lax.broadcasted_iota(jnp.int32, (PAIR, PAIR), 0) == lax.broadcasted_iota(jnp.int32, (PAIR, PAIR), 1))
    rowh = lax.broadcasted_iota(jnp.int32, (L, PAIR), 0)
    laneh = lax.broadcasted_iota(jnp.int32, (L, PAIR), 1)
    eye_pk = jnp.where((laneh == rowh) | (laneh == rowh + HEAD_DIM), 1.0, 0.0)
    rows_sc = lax.broadcasted_iota(jnp.int32, (2 * L, 2 * PAIR), 0)
    src_sc = lax.broadcasted_iota(jnp.int32, (2 * L, 2 * PAIR), 1) & (L - 1)
    tri_mask = ((rows_sc < L) & (rows_sc > src_sc)) | ((rows_sc >= L) & (rows_sc - L >= src_sc))
    r3 = lax.broadcasted_iota(jnp.int32, (L, 3 * L), 0)
    c3 = lax.broadcasted_iota(jnp.int32, (L, 3 * L), 1)
    c3 = c3 - jnp.where(c3 >= L, L, 0) - jnp.where(c3 >= 2 * L, L, 0)
    tri3 = jnp.where(r3 >= c3, 1.0, 0.0).astype(BF16)
    zb = jnp.zeros((L, PAIR), BF16)
    vec = lambda i: vec_ref[i:i + 1, :]

    def hsum(x):
        sa = jnp.sum(jnp.where(head_a, x, 0.0), axis=-1, keepdims=True)
        sb = jnp.sum(jnp.where(head_a, 0.0, x), axis=-1, keepdims=True)
        return jnp.where(head_a, sa, sb)

    def sl(x, p):
        return x[:, p * PAIR:(p + 1) * PAIR]

    m_a = lambda x: jnp.where(head_a, x, jnp.zeros_like(x))
    m_b = lambda x: jnp.where(head_a, jnp.zeros_like(x), x)

    def heads_rows(*cols):
        return cat([cat([m_a(x) for x in cols], axis=1), cat([m_b(x) for x in cols], axis=1)], axis=0)

    def prep(p_ref, ops):
        pr = p_ref[:, 0:RWKV_DIM]
        pk = p_ref[:, RWKV_DIM:2 * RWKV_DIM]
        pv = p_ref[:, 2 * RWKV_DIM:3 * RWKV_DIM]
        wd = jnp.tanh(p_ref[:, LORA_WD_OFF:LORA_WD_OFF + 128])
        lw = -DECAY_SCALE * jax.nn.sigmoid(vec(VEC_W0) + _dot(wd, w2_ref[...]))
        yield
        a_lr = jax.nn.sigmoid(vec(VEC_A0) + _dot(p_ref[:, LORA_AD_OFF:LORA_AD_OFF + 128], a2_ref[...]))
        yield
        ops["gate"] = _dot(jax.nn.sigmoid(p_ref[:, LORA_GD_OFF:LORA_GD_OFF + 256]), g2_ref[...])
        yield
        kk = pk * vec(VEC_KK)
        kbar = pk * (1.0 + (a_lr - 1.0) * vec(VEC_KA))
        yield
        h1 = lw.astype(BF16)
        r1 = lw - h1.astype(F32)
        h2 = r1.astype(BF16)
        h3 = (r1 - h2.astype(F32)).astype(BF16)
        c = jnp.dot(tri3, cat([h1, h2, h3], axis=0), preferred_element_type=F32)
        yield
        cl = c[L - 1:L, :]
        einv = jnp.exp(-c)
        yield
        edec = jnp.exp(cl - c)
        yield
        rt = pr * jnp.exp(c)
        yield
        eprev = jnp.exp(c - lw)
        ops["gl"] = jnp.exp(cl)
        ops["rkr"] = pr * kbar * vec(VEC_RK)
        ops["rt"] = rt
        ops["pv"] = pv
        yield
        ss = [hsum(sl(kk, p) * sl(kk, p)) for p in pairs]
        yield
        kkn = [sl(kk, p) * lax.rsqrt(jnp.maximum(ss[p], 1e-24)) for p in pairs]
        bvec = [kkn[p] * sl(a_lr, p) for p in pairs]
        yield
        ops["at"] = [(-kkn[p] * sl(eprev, p)).astype(BF16) for p in pairs]
        ops["rtp"] = [sl(rt, p).astype(BF16) for p in pairs]
        yield
        ops["bt"] = [(bvec[p] * sl(einv, p)).astype(BF16) for p in pairs]
        ops["kt"] = [(sl(kbar, p) * sl(einv, p)).astype(BF16) for p in pairs]
        yield
        ops["bh"] = [(bvec[p] * sl(edec, p)).astype(BF16) for p in pairs]
        ops["kh"] = [(sl(kbar, p) * sl(edec, p)).astype(BF16) for p in pairs]
        ops["vp"] = [sl(pv, p).astype(BF16) for p in pairs]

    def main(ops):
        at, rtp, bt, kt, bh, kh, vp = (ops[k] for k in ("at", "rtp", "bt", "kt", "bh", "kh", "vp"))
        sc = [_dot_nt(cat([at[p], rtp[p]], axis=0), cat([m_a(bt[p]), m_b(kt[p]), m_a(kt[p]), m_b(bt[p])], axis=0))
              for p in pairs]
        yield
        sc = [jnp.where(tri_mask, sc[p], 0.0) for p in pairs]
        npk = [jnp.where(head_a, sc[p][:L, :PAIR], sc[p][:L, PAIR:]) for p in pairs]
        aak = [jnp.where(head_a, sc[p][:L, PAIR:], sc[p][:L, :PAIR]) for p in pairs]
        bot_sc = [sc[p][L:] for p in pairs]
        yield
        aakv = [_dot(aak[p], heads_rows(vp[p])).astype(BF16) for p in pairs]
        yield
        tpk = [eye_pk for p in pairs]
        for _ in range(6):
            res = [_dot(npk[p], heads_rows(npk[p].astype(BF16), tpk[p].astype(BF16))) for p in pairs]
            npk = [res[p][:, :PAIR] for p in pairs]
            tpk = [tpk[p] + res[p][:, PAIR:] for p in pairs]
            yield
        pq = [_dot(tpk[p], heads_rows(at[p], aakv[p])) for p in pairs]
        yield
        pb = [pq[p][:, :PAIR].astype(BF16) for p in pairs]
        qb = [pq[p][:, PAIR:].astype(BF16) for p in pairs]
        rhs = [cat([cat([m_a(pb[p]), m_a(qb[p])], axis=1), cat([zb, m_b(vp[p])], axis=1),
                    cat([zb, m_a(vp[p])], axis=1), cat([m_b(pb[p]), m_b(qb[p])], axis=1)], axis=0) for p in pairs]
        top = [_dot(bot_sc[p], rhs[p]) for p in pairs]
        yield
        bot = [_dot_tn(cat([m_a(bh[p]), m_b(kh[p]), m_a(kh[p]), m_b(bh[p])], axis=0), rhs[p]) for p in pairs]
        yield
        ys = [_dot(cat([sl(ops["rt"], p) + top[p][:, :PAIR],
                        jnp.where(eye, sl(ops["gl"], p), 0.0) + bot[p][:, :PAIR]], axis=0), st_ref[p])
              for p in pairs]
        yield
        for p in pairs:
            st_ref[p] = ys[p][L:] + bot[p][:, PAIR:]
        ops["y"] = [ys[p][:L] + top[p][:, PAIR:] for p in pairs]

    def norm(ops, y_ref):
        y = ops["y"]
        mean = [hsum(y[p]) * (1.0 / HEAD_DIM) for p in pairs]
        yield
        dev = [y[p] - mean[p] for p in pairs]
        var = [hsum(dev[p] * dev[p]) * (1.0 / HEAD_DIM) for p in pairs]
        yield
        bonus = [hsum(sl(ops["rkr"], p)) * sl(ops["pv"], p) for p in pairs]
        yield
        for p in pairs:
            cs = slice(p * PAIR, (p + 1) * PAIR)
            yn = (dev[p] * lax.rsqrt(var[p] + GN_EPS) * vec_ref[VEC_GNW:VEC_GNW + 1, cs]
                  + vec_ref[VEC_GNB:VEC_GNB + 1, cs])
            y_ref[:, cs] = ((yn + bonus[p]) * sl(ops["gate"], p)).astype(y_ref.dtype)
            if p % 2 == 1:
                yield

    return prep, main, norm


def _rwkv_kernel(n_ref, nn_ref, w_ref, mu_ref, wqkv_ref, vec_ref, w2_ref, a2_ref, g2_ref, y_ref, qkv_ref,
                 p_scr, last_ref, st_ref):
    tm = n_ref.shape[0]
    n_chunks = tm // CHUNK
    t = pl.program_id(1)
    slot = lax.rem(t, 2)
    row = lax.broadcasted_iota(jnp.int32, (tm, RW_TN), 0)

    def rw_proj(src_ref, dst_slot):
        n = src_ref[...]
        for j in range(RW_WIDTH // RW_TN):
            cs = slice(j * RW_TN, (j + 1) * RW_TN)
            p = jnp.dot(n, w_ref[:, cs], preferred_element_type=F32)
            prev = jnp.where(row == 0, last_ref[:, cs], pltpu.roll(p, 1, 0))
            last_ref[:, cs] = p[tm - 1:tm, :]
            p_scr[dst_slot, :, cs] = p + mu_ref[:, cs] * (prev - p)
            yield

    def qkv_proj():
        n = n_ref[...]
        per = ATTN_GROUP_SLABS
        for j in range(wqkv_ref.shape[1] // ATTN_GROUP_DIM):
            res = jnp.dot(n, wqkv_ref[:, j * ATTN_GROUP_DIM:(j + 1) * ATTN_GROUP_DIM], preferred_element_type=F32)
            for s in range(per):
                qkv_ref[j * per + s] = res[:, s * LANES:(s + 1) * LANES]
            yield

    @pl.when(t == 0)
    def _init():
        st_ref[...] = jnp.zeros_like(st_ref)
        last_ref[...] = jnp.zeros_like(last_ref)
        _interleave(rw_proj(n_ref, 0))

    def background():
        yield from qkv_proj()
        yield from rw_proj(nn_ref, 1 - slot)

    prep, main, norm = _rwkv_phases(vec_ref, w2_ref, a2_ref, g2_ref, st_ref)
    rows = [pl.ds(u * CHUNK, CHUNK) for u in range(n_chunks)]
    ops = [{} for _ in range(n_chunks)]
    prep_u = lambda u: prep(p_scr.at[slot, rows[u]], ops[u])
    norm_u = lambda u: norm(ops[u], y_ref.at[rows[u]])
    bg = _every(background(), RW_BG_EVERY)
    _interleave(prep_u(0), prep_u(1), background=bg)
    for a in range(0, n_chunks, 2):
        gens = [main(ops[a]), _delayed(main(ops[a + 1]), 2)]
        if a + 2 < n_chunks:
            gens += [prep_u(a + 2), prep_u(a + 3)]
        if a > 0:
            gens += [norm_u(a - 2), norm_u(a - 1)]
        _interleave(*gens, background=bg)
    _interleave(norm_u(n_chunks - 2), norm_u(n_chunks - 1), background=bg)
    _interleave(bg)


def _rwkv(n, w_rw, mu_rw, w_qkv, vecs, w2p, a2p, g2p, batch, seq):
    m, d = n.shape
    nt = seq // RW_TM
    slabs = w_qkv.shape[1] // LANES
    full = lambda a: pl.BlockSpec(a.shape, lambda b, t: (0, 0), pipeline_mode=pl.Buffered(1))
    return pl.pallas_call(
        _rwkv_kernel,
        out_shape=(jax.ShapeDtypeStruct((m, RWKV_DIM), BF16), jax.ShapeDtypeStruct((slabs, m, LANES), F32)),
        grid=(batch, nt),
        in_specs=[pl.BlockSpec((RW_TM, d), lambda b, t: (b * nt + t, 0)),
                  pl.BlockSpec((RW_TM, d), lambda b, t: (b * nt + jnp.minimum(t + 1, nt - 1), 0)),
                  full(w_rw), full(mu_rw), full(w_qkv), full(vecs), full(w2p), full(a2p), full(g2p)],
        out_specs=(pl.BlockSpec((RW_TM, RWKV_DIM), lambda b, t: (b * nt + t, 0)),
                   pl.BlockSpec((slabs, RW_TM, LANES), lambda b, t: (0, b * nt + t, 0))),
        scratch_shapes=[pltpu.VMEM((2, RW_TM, RW_WIDTH), F32), pltpu.VMEM((1, RW_WIDTH), F32),
                        pltpu.VMEM((N_PAIRS, PAIR, PAIR), F32)],
        compiler_params=_params("parallel", "arbitrary"),
        name="rwkv7_proj_chunk_scan",
    )(n, n, w_rw, mu_rw, w_qkv, vecs, w2p, a2p, g2p)


def _dil_attn_kernel(q_ref, kp_ref, k_ref, vp_ref, v_ref, o_ref, l_ref, *, window, dilation, slopes):
    blk = ATTN_BLK
    width = ATTN_GROUP_DIM
    n_win = q_ref.shape[1] // window
    lane_head = lax.broadcasted_iota(jnp.int32, (blk, width), 1) >> 6
    shape = (HEADS_PER_GROUP * blk, 2 * blk)
    rowi = lax.broadcasted_iota(jnp.int32, shape, 0)
    ki = lax.broadcasted_iota(jnp.int32, shape, 1)
    steps = (rowi & (blk - 1)) + blk - ki
    in_band = (steps >= 0) & (steps <= blk)
    slope = jnp.full(shape, slopes[0], F32)
    for h in range(1, HEADS_PER_GROUP):
        slope = jnp.where(rowi >= h * blk, slopes[h], slope)
    alibi = -slope * (steps * dilation).astype(F32)
    bias = jnp.where(in_band, alibi, -jnp.inf)
    first_key = jnp.where(pl.program_id(1) == 0, blk, 0)
    bias_w0 = jnp.where(in_band & (ki >= first_key), alibi, -jnp.inf)

    def load(ref, rows):
        return jnp.concatenate([ref[s, rows, :] for s in range(ATTN_GROUP_SLABS)], axis=1)

    def attend(q, k, v, b):
        qs = jnp.concatenate([jnp.where(lane_head == h, q, jnp.zeros_like(q)) for h in range(HEADS_PER_GROUP)],
                             axis=0)
        s = lax.dot_general(qs, k, (((1,), (1,)), ((), ())), preferred_element_type=F32) + b
        mx = jnp.max(s, axis=-1, keepdims=True)
        e = jnp.exp(s - mx)
        den = jnp.sum(e, axis=-1, keepdims=True)
        lse = mx + jnp.log(den)
        prob = (e / den).astype(BF16)
        o = jnp.zeros((blk, width), F32)
        lb = jnp.zeros((blk, width), F32)
        for h in range(HEADS_PER_GROUP):
            oh = jnp.dot(prob[h * blk:(h + 1) * blk], v, preferred_element_type=F32)
            o = jnp.where(lane_head == h, oh, o)
            lb = jnp.where(lane_head == h, lse[h * blk:(h + 1) * blk], lb)
        return o, lb

    def residue(r, carry):
        for j in range(n_win):
            rows = pl.ds(j * window + r, blk, stride=dilation)
            q = (load(q_ref, rows) * (HEAD_DIM ** -0.5)).astype(BF16)
            if j == 0:
                prev = pl.ds(r, blk, stride=dilation)
                k = jnp.concatenate([load(kp_ref, prev), load(k_ref, rows)], axis=0).astype(BF16)
                v = jnp.concatenate([load(vp_ref, prev), load(v_ref, rows)], axis=0).astype(BF16)
                o, lb = attend(q, k, v, bias_w0)
            else:
                both = pl.ds((j - 1) * window + r, 2 * blk, stride=dilation)
                o, lb = attend(q, load(k_ref, both).astype(BF16), load(v_ref, both).astype(BF16), bias)
            for s in range(ATTN_GROUP_SLABS):
                o_ref[s, rows, :] = o[:, s * LANES:(s + 1) * LANES]
                l_ref[s, rows, :] = lb[:, s * LANES:(s + 1) * LANES]
        return carry

    if dilation == 1:
        residue(0, 0)
    else:
        lax.fori_loop(0, dilation, residue, 0, unroll=2)


def _alibi_slopes(n_heads):
    return [2.0 ** (-8.0 * (h + 1.0) / n_heads) for h in range(n_heads)]


def _dil_attention(qkv, group, batch, seq):
    window, dilation = DIL_GROUPS[group]
    assert window // dilation == ATTN_BLK and seq % window == 0
    m = qkv.shape[1]
    tile = max(window, ATTN_TILE)
    assert seq % tile == 0
    n_win = tile // window
    ntile = seq // tile
    slopes = tuple(_alibi_slopes(ATTN_HEADS)[group * HEADS_PER_GROUP:(group + 1) * HEADS_PER_GROUP])
    groups = ATTN_DIM // ATTN_GROUP_DIM
    cur = lambda col: pl.BlockSpec((ATTN_GROUP_SLABS, tile, LANES), lambda b, i: (col, b * ntile + i, 0))
    prev = lambda col: pl.BlockSpec((ATTN_GROUP_SLABS, window, LANES),
                                    lambda b, i: (col, jnp.maximum((b * ntile + i) * n_win - 1, 0), 0))
    out_sds = jax.ShapeDtypeStruct((ATTN_GROUP_SLABS, m, LANES), F32)
    out_spec = pl.BlockSpec((ATTN_GROUP_SLABS, tile, LANES), lambda b, i: (0, b * ntile + i, 0))
    return pl.pallas_call(
        functools.partial(_dil_attn_kernel, window=window, dilation=dilation, slopes=slopes),
        out_shape=(out_sds, out_sds),
        grid=(batch, ntile),
        in_specs=[cur(group), prev(groups + group), cur(groups + group),
                  prev(2 * groups + group), cur(2 * groups + group)],
        out_specs=(out_spec, out_spec),
        compiler_params=_params("parallel", "arbitrary"),
        name=f"dilated_attn_g{group}",
    )(qkv, qkv, qkv, qkv, qkv)


def _merge_kernel(n_ref, y_ref, o0_ref, o1_ref, o2_ref, l0_ref, l1_ref, l2_ref, x_ref,
                  wg_ref, pr_ref, pa_ref, wo_ref, h_ref):
    d = x_ref.shape[1]
    gates = jnp.dot(n_ref[...], wg_ref[...], preferred_element_type=F32)
    t_rwkv = jnp.dot(y_ref[...], pr_ref[...], preferred_element_type=F32)
    wide = lambda ref: jnp.concatenate([ref[s] for s in range(ATTN_GROUP_SLABS)], axis=1)
    l0, l1, l2 = wide(l0_ref), wide(l1_ref), wide(l2_ref)
    mx = jnp.maximum(jnp.maximum(l0, l1), l2)
    e0, e1, e2 = jnp.exp(l0 - mx), jnp.exp(l1 - mx), jnp.exp(l2 - mx)
    y_attn = (e0 * wide(o0_ref) + e1 * wide(o1_ref) + e2 * wide(o2_ref)) / (e0 + e1 + e2)
    t_attn = _dot(y_attn, pa_ref[...])
    merged = jax.nn.sigmoid(gates[:, :d]) * t_rwkv + jax.nn.sigmoid(gates[:, d:]) * t_attn
    h_ref[...] = x_ref[...] + _dot(merged, wo_ref[...])


def _merge(n, y_rwkv, attn, x, wg, p_rwkv, p_attn, w_out, tm):
    m, d = x.shape
    row = lambda w: pl.BlockSpec((tm, w), lambda i: (i, 0))
    slab = pl.BlockSpec((ATTN_GROUP_SLABS, tm, LANES), lambda i: (0, i, 0))
    full = lambda a: pl.BlockSpec(a.shape, lambda i: (0, 0), pipeline_mode=pl.Buffered(1))
    (o0, l0), (o1, l1), (o2, l2) = attn
    return pl.pallas_call(
        _merge_kernel,
        out_shape=jax.ShapeDtypeStruct((m, d), F32),
        grid=(m // tm,),
        in_specs=[row(d), row(d)] + [slab] * 6 + [row(d), full(wg), full(p_rwkv), full(p_attn), full(w_out)],
        out_specs=row(d),
        compiler_params=_params("parallel"),
        name="gated_merge",
    )(n, y_rwkv, o0, o1, o2, l0, l1, l2, x, wg, p_rwkv, p_attn, w_out)


def _mem_kv_kernel(mem_ref, g_ref, w_ref, o_ref):
    memn = _rmsnorm_f32(mem_ref[...], g_ref[...])
    o_ref[...] = _dot(memn, w_ref[...]).astype(o_ref.dtype)


def _mem_kv(mem2d, g, w_kv, mem_len):
    m, d = mem2d.shape
    n = w_kv.shape[1]
    return pl.pallas_call(
        _mem_kv_kernel,
        out_shape=jax.ShapeDtypeStruct((m, n), BF16),
        grid=(m // mem_len,),
        in_specs=[pl.BlockSpec((mem_len, d), lambda i: (i, 0)), pl.BlockSpec((1, d), lambda i: (0, 0)),
                  pl.BlockSpec((d, n), lambda i: (0, 0))],
        out_specs=pl.BlockSpec((mem_len, n), lambda i: (i, 0)),
        compiler_params=_params("parallel"),
        name="mem_kv_proj",
    )(mem2d, g.reshape(1, d), w_kv)


def _xattn_kernel(h_ref, g_ref, wq_ref, kv_ref, wo_ref, o_ref):
    d = h_ref.shape[1]
    hd = d // XATTN_HEADS
    h = h_ref[...]
    q = _dot(_rmsnorm_f32(h, g_ref[...]), wq_ref[...])
    outs = []
    for hh in range(XATTN_HEADS):
        cs = slice(hh * hd, (hh + 1) * hd)
        s = _dot_nt(q[:, cs], kv_ref[:, cs]) * (hd ** -0.5)
        e = jnp.exp(s - jnp.max(s, axis=-1, keepdims=True))
        prob = e / jnp.sum(e, axis=-1, keepdims=True)
        outs.append(_dot(prob, kv_ref[:, d + hh * hd:d + (hh + 1) * hd]))
    o_ref[...] = h + _dot(jnp.concatenate(outs, axis=-1), wo_ref[...])


def _xattn(h, g, wq, kv, wo, seq, mem_len, tm):
    m, d = h.shape
    tiles_per_seq = seq // tm
    full = lambda a: pl.BlockSpec(a.shape, lambda i: (0, 0), pipeline_mode=pl.Buffered(1))
    return pl.pallas_call(
        _xattn_kernel,
        out_shape=jax.ShapeDtypeStruct((m, d), F32),
        grid=(m // tm,),
        in_specs=[pl.BlockSpec((tm, d), lambda i: (i, 0)), pl.BlockSpec((1, d), lambda i: (0, 0)), full(wq),
                  pl.BlockSpec((mem_len, 2 * d), lambda i: (i // tiles_per_seq, 0)), full(wo)],
        out_specs=pl.BlockSpec((tm, d), lambda i: (i, 0)),
        compiler_params=_params("parallel"),
        name="mem_cross_attn",
    )(h, g.reshape(1, d), wq, kv, wo)


def _ffn_kernel(h_ref, g_ref, w1_ref, w2_ref, gf_ref, o_ref, xn_ref, acc_ref):
    j = pl.program_id(1)

    @pl.when(j == 0)
    def _first():
        xn_ref[...] = _rmsnorm_f32(h_ref[...], g_ref[...]).astype(BF16)
        acc_ref[...] = jnp.zeros_like(acc_ref)

    u = jnp.dot(xn_ref[...], w1_ref[...], preferred_element_type=F32)
    act = jnp.square(jnp.maximum(u, 0.0)).astype(BF16)
    acc_ref[...] += jnp.dot(act, w2_ref[...], preferred_element_type=F32)

    @pl.when(j == pl.num_programs(1) - 1)
    def _last():
        o_ref[...] = _rmsnorm_f32(h_ref[...] + acc_ref[...], gf_ref[...])


def _ffn(h, g, w1, w2, g_final, tm, tf):
    m, d = h.shape
    f = w1.shape[1]
    return pl.pallas_call(
        _ffn_kernel,
        out_shape=jax.ShapeDtypeStruct((m, d), F32),
        grid=(m // tm, f // tf),
        in_specs=[pl.BlockSpec((tm, d), lambda i, j: (i, 0)), pl.BlockSpec((1, d), lambda i, j: (0, 0)),
                  pl.BlockSpec((d, tf), lambda i, j: (0, j)), pl.BlockSpec((tf, d), lambda i, j: (j, 0)),
                  pl.BlockSpec((1, d), lambda i, j: (0, 0))],
        out_specs=pl.BlockSpec((tm, d), lambda i, j: (i, 0)),
        scratch_shapes=[pltpu.VMEM((tm, d), BF16), pltpu.VMEM((tm, d), F32)],
        compiler_params=_params("parallel", "arbitrary"),
        name="relu2_mlp_final_norm",
    )(h, g.reshape(1, d), w1, w2, g_final.reshape(1, d))


def _pad_cols(a, width):
    return jnp.pad(a, ((0, 0), (0, width - a.shape[1])))


def _pad_rows(a, rows):
    return jnp.pad(a, ((0, rows - a.shape[0]), (0, 0)))


def kernel(x, mem, norm_mix_g, w_in, shift_mu, w0, w2, a0, a2, g2, k_k, k_a, r_k, gn_w, gn_b, p_rwkv, p_attn, w_out, norm_x_g, norm_mem_g, xa_wq, xa_wkv, xa_wo, norm_ffn_g, ffn_w1, ffn_w2, norm_final_g):
    batch, seq, d = x.shape
    mem_len = mem.shape[1]
    assert w_in.shape[0] == 1 and d == RWKV_DIM and seq % RW_TM == 0
    m = batch * seq
    x2 = x.reshape(m, d)

    w = w_in[0]
    c_wd, c_ad, c_gd = 3 * RWKV_DIM, 3 * RWKV_DIM + DECAY_LORA, 3 * RWKV_DIM + DECAY_LORA + AAA_LORA
    c_q = c_gd + GATE_LORA
    c_gate = c_q + 3 * ATTN_DIM
    lora_cols = lambda a: jnp.concatenate(
        [a[:, :c_wd], _pad_cols(a[:, c_wd:c_ad], 128), _pad_cols(a[:, c_ad:c_gd], 128),
         _pad_cols(a[:, c_gd:c_q], 256)], axis=1)
    w_rw = lora_cols(w).astype(BF16)
    mu_rw = lora_cols(shift_mu[0].reshape(1, -1)).astype(F32)
    w_qkv = w[:, c_q:c_gate].astype(BF16)
    w_gate = w[:, c_gate:].astype(BF16)
    vecs = jnp.stack([w0[0], a0[0], k_k[0], k_a[0], r_k[0].reshape(-1), gn_w[0], gn_b[0],
                      jnp.zeros((RWKV_DIM,), F32)]).astype(F32)
    w2p = _pad_rows(w2[0], 128).astype(BF16)
    a2p = _pad_rows(a2[0], 128).astype(BF16)
    g2p = _pad_rows(g2[0], 256).astype(BF16)

    n = _rmsnorm(x2, norm_mix_g[0], BF16, 512)
    y_rwkv, qkv = _rwkv(n, w_rw, mu_rw, w_qkv, vecs, w2p, a2p, g2p, batch, seq)
    attn = [_dil_attention(qkv, g, batch, seq) for g in range(len(DIL_GROUPS))]
    h1 = _merge(n, y_rwkv, attn, x2, w_gate, p_rwkv[0].astype(BF16), p_attn[0].astype(BF16),
                w_out[0].astype(BF16), 512)
    kv = _mem_kv(mem.reshape(batch * mem_len, d), norm_mem_g[0], xa_wkv[0].astype(BF16), mem_len)
    h2 = _xattn(h1, norm_x_g[0], xa_wq[0].astype(BF16), kv, xa_wo[0].astype(BF16), seq, mem_len, 512)
    out = _ffn(h2, norm_ffn_g[0], ffn_w1[0].astype(BF16), ffn_w2[0].astype(BF16), norm_final_g, 1024, 1024)
    return out.reshape(batch, seq, d)
```

```python
import functools
import math

import jax
import jax.numpy as jnp
from jax import lax
from jax.experimental import pallas as pl
from jax.experimental.pallas import tpu as pltpu

F32 = jnp.float32
BF16 = jnp.bfloat16

LANES = 128
HEAD_DIM = 64
PAIR = 2 * HEAD_DIM
CHUNK = 64
RWKV_DIM = 1024
N_PAIRS = RWKV_DIM // PAIR
DECAY_LORA = 64
AAA_LORA = 64
GATE_LORA = 160
LORA_WD_OFF = 3 * RWKV_DIM
LORA_AD_OFF = LORA_WD_OFF + 128
LORA_GD_OFF = LORA_AD_OFF + 128
RW_WIDTH = LORA_GD_OFF + 256
RW_TM = 256
RW_TN = 256
RW_BG_EVERY = 2
GN_EPS = HEAD_DIM * 1e-5
LOG2E = 1.0 / math.log(2.0)
DECAY_SCALE = math.exp(-0.5)
NORM_EPS = 1e-6
DIL_GROUPS = ((128, 1), (512, 4), (2048, 16))
HEADS_PER_GROUP = 4
ATTN_HEADS = 12
ATTN_GROUP_DIM = HEADS_PER_GROUP * HEAD_DIM
ATTN_GROUP_SLABS = ATTN_GROUP_DIM // LANES
ATTN_DIM = ATTN_HEADS * HEAD_DIM
ATTN_BLK = 128
ATTN_TILE = 1024
ATTN_UNROLL = 2
XATTN_HEADS = 4
VMEM_LIMIT = 56 * 1024 * 1024


def _dot(a, b):
    return jnp.dot(a.astype(BF16), b.astype(BF16), preferred_element_type=F32)


def _dot_nt(a, b):
    return lax.dot_general(a.astype(BF16), b.astype(BF16), (((1,), (1,)), ((), ())),
                           preferred_element_type=F32)


def _params(*sem):
    return pltpu.CompilerParams(dimension_semantics=sem, vmem_limit_bytes=VMEM_LIMIT)


def _rmsnorm_f32(x, g):
    return x * lax.rsqrt(jnp.mean(x * x, axis=-1, keepdims=True) + NORM_EPS) * g


VEC_W0, VEC_A0, VEC_KK, VEC_KA, VEC_RK, VEC_GNW, VEC_GNB = range(7)


def _interleave(*gens, background=None):
    live = list(gens)
    while live:
        for g in list(live):
            try:
                next(g)
            except StopIteration:
                live.remove(g)
        if background is not None:
            next(background, None)


def _every(gen, n):
    for _ in gen:
        for _ in range(n):
            yield


def _delayed(gen, steps):
    for _ in range(steps):
        yield
    yield from gen


def _rwkv_phases(vec_ref, w2_ref, a2_ref, g2_ref, st_ref):
    L = CHUNK
    pairs = range(N_PAIRS)
    cat = jnp.concatenate

    head_a = lax.broadcasted_iota(jnp.int32, (L, PAIR), 1) < HEAD_DIM
    rowh = lax.broadcasted_iota(jnp.int32, (L, PAIR), 0)
    laneh = lax.broadcasted_iota(jnp.int32, (L, PAIR), 1)
    eye_pk = jnp.where((laneh == rowh) | (laneh == rowh + HEAD_DIM), 1.0, 0.0)
    rows_sc = lax.broadcasted_iota(jnp.int32, (2 * L, 2 * PAIR), 0)
    src_sc = lax.broadcasted_iota(jnp.int32, (2 * L, 2 * PAIR), 1) & (L - 1)
    tri_mask = ((rows_sc < L) & (rows_sc > src_sc)) | ((rows_sc >= L) & (rows_sc - L >= src_sc))
    r3 = lax.broadcasted_iota(jnp.int32, (L, 3 * L), 0)
    c3 = lax.broadcasted_iota(jnp.int32, (L, 3 * L), 1)
    c3 = c3 - jnp.where(c3 >= L, L, 0) - jnp.where(c3 >= 2 * L, L, 0)
    tri3 = jnp.where(r3 >= c3, 1.0, 0.0).astype(BF16)
    zb = jnp.zeros((L, PAIR), BF16)
    vec = lambda i: vec_ref[i:i + 1, :]

    def hsum(x):
        sa = jnp.sum(jnp.where(head_a, x, 0.0), axis=-1, keepdims=True)
        sb = jnp.sum(jnp.where(head_a, 0.0, x), axis=-1, keepdims=True)
        return jnp.where(head_a, sa, sb)

    def sl(x, p):
        return x[:, p * PAIR:(p + 1) * PAIR]

    m_a = lambda x: jnp.where(head_a, x, jnp.zeros_like(x))
    m_b = lambda x: jnp.where(head_a, jnp.zeros_like(x), x)

    def heads_rows(*cols):
        return cat([cat([m_a(x) for x in cols], axis=1), cat([m_b(x) for x in cols], axis=1)], axis=0)

    def prep(p_ref, ops):
        pr = p_ref[:, 0:RWKV_DIM]
        pk = p_ref[:, RWKV_DIM:2 * RWKV_DIM]
        pv = p_ref[:, 2 * RWKV_DIM:3 * RWKV_DIM]
        wd = jnp.tanh(p_ref[:, LORA_WD_OFF:LORA_WD_OFF + 128])
        lw = -DECAY_SCALE * jax.nn.sigmoid(vec(VEC_W0) + _dot(wd, w2_ref[...]))
        yield
        a_lr = jax.nn.sigmoid(vec(VEC_A0) + _dot(p_ref[:, LORA_AD_OFF:LORA_AD_OFF + 128], a2_ref[...]))
        yield
        ops["gate"] = _dot(jax.nn.sigmoid(p_ref[:, LORA_GD_OFF:LORA_GD_OFF + 256]), g2_ref[...])
        yield
        kk = pk * vec(VEC_KK)
        kbar = pk * (1.0 + (a_lr - 1.0) * vec(VEC_KA))
        yield
        h1 = lw.astype(BF16)
        r1 = lw - h1.astype(F32)
        h2 = r1.astype(BF16)
        h3 = (r1 - h2.astype(F32)).astype(BF16)
        c = jnp.dot(tri3, cat([h1, h2, h3], axis=0), preferred_element_type=F32)
        yield
        cl = c[L - 1:L, :]
        einv = jnp.exp(-c)
        yield
        edec = jnp.exp(cl - c)
        yield
        rt = pr * jnp.exp(c)
        yield
        eprev = jnp.exp(c - lw)
        ops["gl"] = jnp.exp(cl)
        ops["rkr"] = pr * kbar * vec(VEC_RK)
        ops["rt"] = rt
        ops["pv"] = pv
        yield
        ss = [hsum(sl(kk, p) * sl(kk, p)) for p in pairs]
        yield
        kkn = [sl(kk, p) * lax.rsqrt(jnp.maximum(ss[p], 1e-24)) for p in pairs]
        bvec = [kkn[p] * sl(a_lr, p) for p in pairs]
        yield
        ops["at"] = [(-kkn[p] * sl(eprev, p)).astype(BF16) for p in pairs]
        ops["rtp"] = [sl(rt, p).astype(BF16) for p in pairs]
        yield
        ops["bt"] = [(bvec[p] * sl(einv, p)).astype(BF16) for p in pairs]
        ops["kt"] = [(sl(kbar, p) * sl(einv, p)).astype(BF16) for p in pairs]
        yield
        ops["bh"] = [bvec[p] * sl(edec, p) for p in pairs]
        ops["kh"] = [sl(kbar, p) * sl(edec, p) for p in pairs]
        ops["vp"] = [sl(pv, p).astype(BF16) for p in pairs]

    def main(ops):
        at, rtp, bt, kt, bh, kh, vp = (ops[k] for k in ("at", "rtp", "bt", "kt", "bh", "kh", "vp"))
        sc = [_dot_nt(cat([at[p], rtp[p]], axis=0), cat([m_a(bt[p]), m_b(kt[p]), m_a(kt[p]), m_b(bt[p])], axis=0))
              for p in pairs]
        yield
        sc = [jnp.where(tri_mask, sc[p], 0.0) for p in pairs]
        npk = [jnp.where(head_a, sc[p][:L, :PAIR], sc[p][:L, PAIR:]) for p in pairs]
        aak = [jnp.where(head_a, sc[p][:L, PAIR:], sc[p][:L, :PAIR]) for p in pairs]
        bot_sc = [sc[p][L:] for p in pairs]
        yield
        aakv = [_dot(aak[p], heads_rows(vp[p])).astype(BF16) for p in pairs]
        yield
        tpk = [eye_pk for p in pairs]
        for _ in range(6):
            res = [_dot(npk[p], heads_rows(npk[p].astype(BF16), tpk[p].astype(BF16))) for p in pairs]
            npk = [res[p][:, :PAIR] for p in pairs]
            tpk = [tpk[p] + res[p][:, PAIR:] for p in pairs]
            yield
        pq = [_dot(tpk[p], heads_rows(at[p], aakv[p])) for p in pairs]
        yield
        pb = [pq[p][:, :PAIR].astype(BF16) for p in pairs]
        qb = [pq[p][:, PAIR:].astype(BF16) for p in pairs]
        rhs = [cat([cat([m_a(pb[p]), m_a(qb[p])], axis=1), cat([zb, m_b(vp[p])], axis=1),
                    cat([zb, m_a(vp[p])], axis=1), cat([m_b(pb[p]), m_b(qb[p])], axis=1)], axis=0) for p in pairs]
        zt1 = [cat([bh[p], kh[p]], axis=0).T for p in pairs]
        zt2 = [cat([kh[p], bh[p]], axis=0).T for p in pairs]
        yield
        lhs_st = [cat([jnp.where(head_a, zt1[p][:L], zt1[p][L:]), jnp.where(head_a, zt2[p][:L], zt2[p][L:])],
                      axis=1) for p in pairs]
        tb = [_dot(cat([bot_sc[p], lhs_st[p]], axis=0), rhs[p]) for p in pairs]
        yield
        st = [st_ref[p].astype(BF16) for p in pairs]
        ys = [_dot(cat([sl(ops["rt"], p) + tb[p][:L, :PAIR],
                        jnp.where(eye_pk > 0.0, sl(ops["gl"], p), 0.0) + tb[p][L:, :PAIR]], axis=0),
                   heads_rows(st[p])) for p in pairs]
        yield
        for p in pairs:
            st_ref[p] = ys[p][L:] + tb[p][L:, PAIR:]
        ops["y"] = [ys[p][:L] + tb[p][:L, PAIR:] for p in pairs]

    def norm(ops, y_ref):
        y = ops["y"]
        mean = [hsum(y[p]) * (1.0 / HEAD_DIM) for p in pairs]
        yield
        dev = [y[p] - mean[p] for p in pairs]
        var = [hsum(dev[p] * dev[p]) * (1.0 / HEAD_DIM) for p in pairs]
        yield
        bonus = [hsum(sl(ops["rkr"], p)) * sl(ops["pv"], p) for p in pairs]
        yield
        for p in pairs:
            cs = slice(p * PAIR, (p + 1) * PAIR)
            yn = (dev[p] * lax.rsqrt(var[p] + GN_EPS) * vec_ref[VEC_GNW:VEC_GNW + 1, cs]
                  + vec_ref[VEC_GNB:VEC_GNB + 1, cs])
            y_ref[:, cs] = ((yn + bonus[p]) * sl(ops["gate"], p)).astype(y_ref.dtype)
            if p % 2 == 1:
                yield

    return prep, main, norm


def _rwkv_kernel(x_ref, xn_ref, g_ref, wm_ref, wl_ref, mum_ref, mul_ref, wqkv_ref, vec_ref, w2_ref, a2_ref, g2_ref,
                 y_ref, qkv_ref, p_scr, n_scr, last_ref, st_ref):
    tm = x_ref.shape[0]
    n_chunks = tm // CHUNK
    t = pl.program_id(1)
    slot = lax.rem(t, 2)
    row = lax.broadcasted_iota(jnp.int32, (tm, RW_TN), 0)

    def norm_x(src_ref, dst_slot):
        n_scr[dst_slot] = _rmsnorm_f32(src_ref[...], g_ref[...]).astype(BF16)
        yield

    def rw_proj(src_slot, dst_slot):
        n = n_scr[src_slot]
        blocks = [(wm_ref, mum_ref, j, j) for j in range(wm_ref.shape[1] // RW_TN)]
        blocks += [(wl_ref, mul_ref, j, wm_ref.shape[1] // RW_TN + j) for j in range(wl_ref.shape[1] // RW_TN)]
        for w_ref, mu_ref, j, jd in blocks:
            cs = slice(j * RW_TN, (j + 1) * RW_TN)
            cd = slice(jd * RW_TN, (jd + 1) * RW_TN)
            p = jnp.dot(n, w_ref[:, cs], preferred_element_type=F32)
            prev = jnp.where(row == 0, last_ref[:, cd], pltpu.roll(p, 1, 0))
            last_ref[:, cd] = p[tm - 1:tm, :]
            p_scr[dst_slot, :, cd] = p + mu_ref[:, cs] * (prev - p)
            yield

    def qkv_proj():
        n = n_scr[slot]
        per = ATTN_GROUP_SLABS
        for j in range(wqkv_ref.shape[1] // ATTN_GROUP_DIM):
            res = jnp.dot(n, wqkv_ref[:, j * ATTN_GROUP_DIM:(j + 1) * ATTN_GROUP_DIM], preferred_element_type=F32)
            for s in range(per):
                qkv_ref[j * per + s] = res[:, s * LANES:(s + 1) * LANES]
            yield

    @pl.when(t == 0)
    def _init():
        st_ref[...] = jnp.zeros_like(st_ref)
        last_ref[...] = jnp.zeros_like(last_ref)
        _interleave(norm_x(x_ref, 0))
        _interleave(rw_proj(0, 0))

    def background():
        yield from qkv_proj()
        yield from norm_x(xn_ref, 1 - slot)
        yield from rw_proj(1 - slot, 1 - slot)

    prep, main, norm = _rwkv_phases(vec_ref, w2_ref, a2_ref, g2_ref, st_ref)
    rows = [pl.ds(u * CHUNK, CHUNK) for u in range(n_chunks)]
    ops = [{} for _ in range(n_chunks)]
    prep_u = lambda u: prep(p_scr.at[slot, rows[u]], ops[u])
    norm_u = lambda u: norm(ops[u], y_ref.at[rows[u]])
    bg = _every(background(), RW_BG_EVERY)
    _interleave(prep_u(0), prep_u(1), background=bg)
    for a in range(0, n_chunks, 2):
        gens = [main(ops[a]), _delayed(main(ops[a + 1]), 2)]
        if a + 2 < n_chunks:
            gens += [prep_u(a + 2), prep_u(a + 3)]
        if a > 0:
            gens += [norm_u(a - 2), norm_u(a - 1)]
        _interleave(*gens, background=bg)
    _interleave(norm_u(n_chunks - 2), norm_u(n_chunks - 1), background=bg)
    _interleave(bg)


def _rwkv(x, g, w_main, w_lora, mu_main, mu_lora, w_qkv, vecs, w2p, a2p, g2p, batch, seq):
    m, d = x.shape
    nt = seq // RW_TM
    slabs = w_qkv.shape[1] // LANES
    assert w_main.shape[1] + w_lora.shape[1] == RW_WIDTH
    full = lambda a: pl.BlockSpec(a.shape, lambda b, t: (0, 0), pipeline_mode=pl.Buffered(1))
    return pl.pallas_call(
        _rwkv_kernel,
        out_shape=(jax.ShapeDtypeStruct((m, RWKV_DIM), BF16), jax.ShapeDtypeStruct((slabs, m, LANES), F32)),
        grid=(batch, nt),
        in_specs=[pl.BlockSpec((RW_TM, d), lambda b, t: (b * nt + t, 0)),
                  pl.BlockSpec((RW_TM, d), lambda b, t: (b * nt + jnp.minimum(t + 1, nt - 1), 0)),
                  full(g), full(w_main), full(w_lora), full(mu_main), full(mu_lora), full(w_qkv), full(vecs),
                  full(w2p), full(a2p), full(g2p)],
        out_specs=(pl.BlockSpec((RW_TM, RWKV_DIM), lambda b, t: (b * nt + t, 0)),
                   pl.BlockSpec((slabs, RW_TM, LANES), lambda b, t: (0, b * nt + t, 0))),
        scratch_shapes=[pltpu.VMEM((2, RW_TM, RW_WIDTH), F32), pltpu.VMEM((2, RW_TM, d), BF16),
                        pltpu.VMEM((1, RW_WIDTH), F32), pltpu.VMEM((N_PAIRS, HEAD_DIM, PAIR), F32)],
        compiler_params=_params("parallel", "arbitrary"),
        name="rwkv7_proj_chunk_scan",
    )(x, x, g, w_main, w_lora, mu_main, mu_lora, w_qkv, vecs, w2p, a2p, g2p)


def _dil_attn_kernel(q_ref, kp_ref, k_ref, vp_ref, v_ref, o_ref, l_ref, *, window, dilation, slopes):
    blk = ATTN_BLK
    width = ATTN_GROUP_DIM
    n_win = q_ref.shape[1] // window
    lane_head = lax.broadcasted_iota(jnp.int32, (blk, width), 1) >> 6
    shape = (HEADS_PER_GROUP * blk, 2 * blk)
    rowi = lax.broadcasted_iota(jnp.int32, shape, 0)
    ki = lax.broadcasted_iota(jnp.int32, shape, 1)
    steps = (rowi & (blk - 1)) + blk - ki
    in_band = (steps >= 0) & (steps <= blk)
    slope = jnp.full(shape, slopes[0], F32)
    for h in range(1, HEADS_PER_GROUP):
        slope = jnp.where(rowi >= h * blk, slopes[h], slope)
    alibi = (-LOG2E) * slope * (steps * dilation).astype(F32)
    bias = jnp.where(in_band, alibi, -jnp.inf)
    first_key = jnp.where(pl.program_id(1) == 0, blk, 0)
    bias_w0 = jnp.where(in_band & (ki >= first_key), alibi, -jnp.inf)

    def load(ref, rows):
        return jnp.concatenate([ref[s, rows, :] for s in range(ATTN_GROUP_SLABS)], axis=1)

    def unit(j, r):
        rows = pl.ds(j * window + r, blk, stride=dilation)
        q = (load(q_ref, rows) * (LOG2E * HEAD_DIM ** -0.5)).astype(BF16)
        if j == 0:
            prev = pl.ds(r, blk, stride=dilation)
            k = jnp.concatenate([load(kp_ref, prev), load(k_ref, rows)], axis=0).astype(BF16)
            v = jnp.concatenate([load(vp_ref, prev), load(v_ref, rows)], axis=0).astype(BF16)
            b = bias_w0
        else:
            both = pl.ds((j - 1) * window + r, 2 * blk, stride=dilation)
            k = load(k_ref, both).astype(BF16)
            v = load(v_ref, both).astype(BF16)
            b = bias
        yield
        qs = jnp.concatenate([jnp.where(lane_head == h, q, jnp.zeros_like(q)) for h in range(HEADS_PER_GROUP)],
                             axis=0)
        s = lax.dot_general(qs, k, (((1,), (1,)), ((), ())), preferred_element_type=F32) + b
        yield
        mx = jnp.max(s, axis=-1, keepdims=True)
        yield
        e = lax.exp2(s - mx)
        yield
        den = jnp.sum(e, axis=-1, keepdims=True)
        lse = mx * (1.0 / LOG2E) + jnp.log(den)
        yield
        prob = (e / den).astype(BF16)
        yield
        o = jnp.zeros((blk, width), F32)
        lb = jnp.zeros((blk, width), F32)
        for h in range(HEADS_PER_GROUP):
            oh = jnp.dot(prob[h * blk:(h + 1) * blk], v, preferred_element_type=F32)
            o = jnp.where(lane_head == h, oh, o)
            lb = jnp.where(lane_head == h, lse[h * blk:(h + 1) * blk], lb)
        yield
        for s_ in range(ATTN_GROUP_SLABS):
            o_ref[s_, rows, :] = o[:, s_ * LANES:(s_ + 1) * LANES]
            l_ref[s_, rows, :] = lb[:, s_ * LANES:(s_ + 1) * LANES]

    def residues(i, carry):
        units = [(j, i * per_iter + dr) for dr in range(per_iter) for j in range(n_win)]
        for a in range(0, len(units), 2):
            _interleave(*(unit(j, r) for j, r in units[a:a + 2]))
        return carry

    per_iter = min(dilation, ATTN_UNROLL)
    if dilation == per_iter:
        residues(0, 0)
    else:
        lax.fori_loop(0, dilation // per_iter, residues, 0)


def _alibi_slopes(n_heads):
    return [2.0 ** (-8.0 * (h + 1.0) / n_heads) for h in range(n_heads)]


def _dil_attention(qkv, group, batch, seq):
    window, dilation = DIL_GROUPS[group]
    assert window // dilation == ATTN_BLK and seq % window == 0
    m = qkv.shape[1]
    tile = max(window, ATTN_TILE)
    assert seq % tile == 0
    n_win = tile // window
    ntile = seq // tile
    slopes = tuple(_alibi_slopes(ATTN_HEADS)[group * HEADS_PER_GROUP:(group + 1) * HEADS_PER_GROUP])
    groups = ATTN_DIM // ATTN_GROUP_DIM
    cur = lambda col: pl.BlockSpec((ATTN_GROUP_SLABS, tile, LANES), lambda b, i: (col, b * ntile + i, 0))
    prev = lambda col: pl.BlockSpec((ATTN_GROUP_SLABS, window, LANES),
                                    lambda b, i: (col, jnp.maximum((b * ntile + i) * n_win - 1, 0), 0))
    out_sds = jax.ShapeDtypeStruct((ATTN_GROUP_SLABS, m, LANES), F32)
    out_spec = pl.BlockSpec((ATTN_GROUP_SLABS, tile, LANES), lambda b, i: (0, b * ntile + i, 0))
    return pl.pallas_call(
        functools.partial(_dil_attn_kernel, window=window, dilation=dilation, slopes=slopes),
        out_shape=(out_sds, out_sds),
        grid=(batch, ntile),
        in_specs=[cur(group), prev(groups + group), cur(groups + group),
                  prev(2 * groups + group), cur(2 * groups + group)],
        out_specs=(out_spec, out_spec),
        compiler_params=_params("parallel", "arbitrary"),
        name=f"dilated_attn_g{group}",
    )(qkv, qkv, qkv, qkv, qkv)


def _merge_kernel(y_ref, o0_ref, o1_ref, o2_ref, l0_ref, l1_ref, l2_ref, x_ref,
                  g_ref, wg_ref, pr_ref, pa_ref, wo_ref, h_ref):
    d = x_ref.shape[1]
    gates = _dot(_rmsnorm_f32(x_ref[...], g_ref[...]), wg_ref[...])
    t_rwkv = jnp.dot(y_ref[...], pr_ref[...], preferred_element_type=F32)
    wide = lambda ref: jnp.concatenate([ref[s] for s in range(ATTN_GROUP_SLABS)], axis=1)
    l0, l1, l2 = wide(l0_ref), wide(l1_ref), wide(l2_ref)
    mx = jnp.maximum(jnp.maximum(l0, l1), l2)
    e0, e1, e2 = jnp.exp(l0 - mx), jnp.exp(l1 - mx), jnp.exp(l2 - mx)
    y_attn = (e0 * wide(o0_ref) + e1 * wide(o1_ref) + e2 * wide(o2_ref)) / (e0 + e1 + e2)
    t_attn = _dot(y_attn, pa_ref[...])
    merged = jax.nn.sigmoid(gates[:, :d]) * t_rwkv + jax.nn.sigmoid(gates[:, d:]) * t_attn
    h_ref[...] = x_ref[...] + _dot(merged, wo_ref[...])


def _merge(y_rwkv, attn, x, g, wg, p_rwkv, p_attn, w_out, tm):
    m, d = x.shape
    row = lambda w: pl.BlockSpec((tm, w), lambda i: (i, 0))
    slab = pl.BlockSpec((ATTN_GROUP_SLABS, tm, LANES), lambda i: (0, i, 0))
    full = lambda a: pl.BlockSpec(a.shape, lambda i: (0, 0), pipeline_mode=pl.Buffered(1))
    (o0, l0), (o1, l1), (o2, l2) = attn
    return pl.pallas_call(
        _merge_kernel,
        out_shape=jax.ShapeDtypeStruct((m, d), F32),
        grid=(m // tm,),
        in_specs=[row(d)] + [slab] * 6 + [row(d), full(g), full(wg), full(p_rwkv), full(p_attn), full(w_out)],
        out_specs=row(d),
        compiler_params=_params("parallel"),
        name="gated_merge",
    )(y_rwkv, o0, o1, o2, l0, l1, l2, x, g, wg, p_rwkv, p_attn, w_out)


def _mem_kv_kernel(mem_ref, g_ref, w_ref, o_ref):
    memn = _rmsnorm_f32(mem_ref[...], g_ref[...])
    o_ref[...] = _dot(memn, w_ref[...]).astype(o_ref.dtype)


def _mem_kv(mem2d, g, w_kv, mem_len):
    m, d = mem2d.shape
    n = w_kv.shape[1]
    return pl.pallas_call(
        _mem_kv_kernel,
        out_shape=jax.ShapeDtypeStruct((m, n), BF16),
        grid=(m // mem_len,),
        in_specs=[pl.BlockSpec((mem_len, d), lambda i: (i, 0)), pl.BlockSpec((1, d), lambda i: (0, 0)),
                  pl.BlockSpec((d, n), lambda i: (0, 0))],
        out_specs=pl.BlockSpec((mem_len, n), lambda i: (i, 0)),
        compiler_params=_params("parallel"),
        name="mem_kv_proj",
    )(mem2d, g.reshape(1, d), w_kv)


def _xattn_kernel(h_ref, g_ref, wq_ref, kv_ref, wo_ref, o_ref):
    d = h_ref.shape[1]
    hd = d // XATTN_HEADS
    h = h_ref[...]
    heads = range(XATTN_HEADS)
    cs = lambda hh: slice(hh * hd, (hh + 1) * hd)
    q = (_dot(_rmsnorm_f32(h, g_ref[...]), wq_ref[...]) * (LOG2E * hd ** -0.5)).astype(BF16)
    s = [_dot_nt(q[:, cs(hh)], kv_ref[:, cs(hh)]) for hh in heads]
    e = [lax.exp2(s[hh] - jnp.max(s[hh], axis=-1, keepdims=True)) for hh in heads]
    prob = [e[hh] / jnp.sum(e[hh], axis=-1, keepdims=True) for hh in heads]
    outs = [_dot(prob[hh], kv_ref[:, d + hh * hd:d + (hh + 1) * hd]) for hh in heads]
    o_ref[...] = h + _dot(jnp.concatenate(outs, axis=-1), wo_ref[...])


def _xattn(h, g, wq, kv, wo, seq, mem_len, tm):
    m, d = h.shape
    tiles_per_seq = seq // tm
    full = lambda a: pl.BlockSpec(a.shape, lambda i: (0, 0), pipeline_mode=pl.Buffered(1))
    return pl.pallas_call(
        _xattn_kernel,
        out_shape=jax.ShapeDtypeStruct((m, d), F32),
        grid=(m // tm,),
        in_specs=[pl.BlockSpec((tm, d), lambda i: (i, 0)), pl.BlockSpec((1, d), lambda i: (0, 0)), full(wq),
                  pl.BlockSpec((mem_len, 2 * d), lambda i: (i // tiles_per_seq, 0)), full(wo)],
        out_specs=pl.BlockSpec((tm, d), lambda i: (i, 0)),
        compiler_params=_params("parallel"),
        name="mem_cross_attn",
    )(h, g.reshape(1, d), wq, kv, wo)


def _ffn_kernel(h_ref, g_ref, w1_ref, w2_ref, gf_ref, o_ref, xn_ref, acc_ref):
    j = pl.program_id(1)

    @pl.when(j == 0)
    def _first():
        xn_ref[...] = _rmsnorm_f32(h_ref[...], g_ref[...]).astype(BF16)
        acc_ref[...] = jnp.zeros_like(acc_ref)

    u = jnp.dot(xn_ref[...], w1_ref[...], preferred_element_type=F32)
    act = jnp.square(jnp.maximum(u, 0.0)).astype(BF16)
    acc_ref[...] += jnp.dot(act, w2_ref[...], preferred_element_type=F32)

    @pl.when(j == pl.num_programs(1) - 1)
    def _last():
        o_ref[...] = _rmsnorm_f32(h_ref[...] + acc_ref[...], gf_ref[...])


def _ffn(h, g, w1, w2, g_final, tm, tf):
    m, d = h.shape
    f = w1.shape[1]
    return pl.pallas_call(
        _ffn_kernel,
        out_shape=jax.ShapeDtypeStruct((m, d), F32),
        grid=(m // tm, f // tf),
        in_specs=[pl.BlockSpec((tm, d), lambda i, j: (i, 0)), pl.BlockSpec((1, d), lambda i, j: (0, 0)),
                  pl.BlockSpec((d, tf), lambda i, j: (0, j)), pl.BlockSpec((tf, d), lambda i, j: (j, 0)),
                  pl.BlockSpec((1, d), lambda i, j: (0, 0))],
        out_specs=pl.BlockSpec((tm, d), lambda i, j: (i, 0)),
        scratch_shapes=[pltpu.VMEM((tm, d), BF16), pltpu.VMEM((tm, d), F32)],
        compiler_params=_params("parallel", "arbitrary"),
        name="relu2_mlp_final_norm",
    )(h, g.reshape(1, d), w1, w2, g_final.reshape(1, d))


def _pad_cols(a, width):
    return jnp.pad(a, ((0, 0), (0, width - a.shape[1])))


def _pad_rows(a, rows):
    return jnp.pad(a, ((0, rows - a.shape[0]), (0, 0)))


def kernel(x, mem, norm_mix_g, w_in, shift_mu, w0, w2, a0, a2, g2, k_k, k_a, r_k, gn_w, gn_b, p_rwkv, p_attn, w_out, norm_x_g, norm_mem_g, xa_wq, xa_wkv, xa_wo, norm_ffn_g, ffn_w1, ffn_w2, norm_final_g):
    batch, seq, d = x.shape
    mem_len = mem.shape[1]
    assert w_in.shape[0] == 1 and d == RWKV_DIM and seq % RW_TM == 0
    m = batch * seq
    x2 = x.reshape(m, d)

    w = w_in[0]
    c_wd, c_ad, c_gd = 3 * RWKV_DIM, 3 * RWKV_DIM + DECAY_LORA, 3 * RWKV_DIM + DECAY_LORA + AAA_LORA
    c_q = c_gd + GATE_LORA
    c_gate = c_q + 3 * ATTN_DIM
    lora_cols = lambda a: jnp.concatenate(
        [_pad_cols(a[:, c_wd:c_ad], 128), _pad_cols(a[:, c_ad:c_gd], 128), _pad_cols(a[:, c_gd:c_q], 256)], axis=1)
    mu = shift_mu[0].reshape(1, -1).astype(F32)
    w_main, w_lora = w[:, :c_wd].astype(BF16), lora_cols(w).astype(BF16)
    mu_main, mu_lora = mu[:, :c_wd], lora_cols(mu)
    w_qkv = w[:, c_q:c_gate].astype(BF16)
    w_gate = w[:, c_gate:].astype(BF16)
    vecs = jnp.stack([w0[0], a0[0], k_k[0], k_a[0], r_k[0].reshape(-1), gn_w[0], gn_b[0],
                      jnp.zeros((RWKV_DIM,), F32)]).astype(F32)
    w2p = _pad_rows(w2[0], 128).astype(BF16)
    a2p = _pad_rows(a2[0], 128).astype(BF16)
    g2p = _pad_rows(g2[0], 256).astype(BF16)

    g_mix = norm_mix_g[0].reshape(1, d)
    y_rwkv, qkv = _rwkv(x2, g_mix, w_main, w_lora, mu_main, mu_lora, w_qkv, vecs, w2p, a2p, g2p, batch, seq)
    attn = [_dil_attention(qkv, g, batch, seq) for g in range(len(DIL_GROUPS))]
    h1 = _merge(y_rwkv, attn, x2, g_mix, w_gate, p_rwkv[0].astype(BF16), p_attn[0].astype(BF16),
                w_out[0].astype(BF16), 512)
    kv = _mem_kv(mem.reshape(batch * mem_len, d), norm_mem_g[0], xa_wkv[0].astype(BF16), mem_len)
    h2 = _xattn(h1, norm_x_g[0], xa_wq[0].astype(BF16), kv, xa_wo[0].astype(BF16), seq, mem_len, 512)
    out = _ffn(h2, norm_ffn_g[0], ffn_w1[0].astype(BF16), ffn_w2[0].astype(BF16), norm_final_g, 1024, 1024)
    return out.reshape(batch, seq, d)
```

```python
import functools
import math

import jax
import jax.numpy as jnp
from jax import lax
from jax.experimental import pallas as pl
from jax.experimental.pallas import tpu as pltpu

F32 = jnp.float32
BF16 = jnp.bfloat16

LANES = 128
HEAD_DIM = 64
PAIR = 2 * HEAD_DIM
CHUNK = 64
RWKV_DIM = 1024
N_PAIRS = RWKV_DIM // PAIR
DECAY_LORA = 64
AAA_LORA = 64
GATE_LORA = 160
LORA_WD_OFF = 3 * RWKV_DIM
LORA_AD_OFF = LORA_WD_OFF + 128
LORA_GD_OFF = LORA_AD_OFF + 128
RW_WIDTH = LORA_GD_OFF + 256
RW_TM = 256
RW_TN = 256
RW_BG_EVERY = 5
GN_EPS = HEAD_DIM * 1e-5
LOG2E = 1.0 / math.log(2.0)
DECAY_SCALE = math.exp(-0.5)
NORM_EPS = 1e-6
DIL_GROUPS = ((128, 1), (512, 4), (2048, 16))
HEADS_PER_GROUP = 4
ATTN_HEADS = 12
ATTN_GROUP_DIM = HEADS_PER_GROUP * HEAD_DIM
ATTN_GROUP_SLABS = ATTN_GROUP_DIM // LANES
ATTN_DIM = ATTN_HEADS * HEAD_DIM
ATTN_BLK = 128
ATTN_TILE = 1024
ATTN_UNROLL = 4
XATTN_HEADS = 4
FFN_TM = 1024
FFN_SPLIT = 2
FFN_TF = 1024
VMEM_LIMIT = 56 * 1024 * 1024


def _dot(a, b):
    return jnp.dot(a.astype(BF16), b.astype(BF16), preferred_element_type=F32)


def _dot_nt(a, b):
    return lax.dot_general(a.astype(BF16), b.astype(BF16), (((1,), (1,)), ((), ())),
                           preferred_element_type=F32)


def _params(*sem):
    return pltpu.CompilerParams(dimension_semantics=sem, vmem_limit_bytes=VMEM_LIMIT)


def _rmsnorm_f32(x, g):
    return x * lax.rsqrt(jnp.mean(x * x, axis=-1, keepdims=True) + NORM_EPS) * g


VEC_W0, VEC_A0, VEC_KK, VEC_KA, VEC_RK, VEC_GNW, VEC_GNB = range(7)


def _interleave(*gens, background=None):
    live = list(gens)
    while live:
        for g in list(live):
            try:
                next(g)
            except StopIteration:
                live.remove(g)
        if background is not None:
            next(background, None)


def _every(gen, n):
    for _ in gen:
        for _ in range(n):
            yield


def _delayed(gen, steps):
    for _ in range(steps):
        yield
    yield from gen


def _rwkv_phases(vec_ref, w2_ref, a2_ref, g2_ref, st_ref):
    L = CHUNK
    pairs = range(N_PAIRS)
    cat = jnp.concatenate

    head_a = lax.broadcasted_iota(jnp.int32, (L, PAIR), 1) < HEAD_DIM
    rowh = lax.broadcasted_iota(jnp.int32, (L, PAIR), 0)
    laneh = lax.broadcasted_iota(jnp.int32, (L, PAIR), 1)
    eye_pk = jnp.where((laneh == rowh) | (laneh == rowh + HEAD_DIM), 1.0, 0.0)
    rows_sc = lax.broadcasted_iota(jnp.int32, (2 * L, 2 * PAIR), 0)
    src_sc = lax.broadcasted_iota(jnp.int32, (2 * L, 2 * PAIR), 1) & (L - 1)
    tri_mask = ((rows_sc < L) & (rows_sc > src_sc)) | ((rows_sc >= L) & (rows_sc - L >= src_sc))
    r3 = lax.broadcasted_iota(jnp.int32, (L, 3 * L), 0)
    c3 = lax.broadcasted_iota(jnp.int32, (L, 3 * L), 1)
    c3 = c3 - jnp.where(c3 >= L, L, 0) - jnp.where(c3 >= 2 * L, L, 0)
    tri3 = jnp.where(r3 >= c3, 1.0, 0.0).astype(BF16)
    zb = jnp.zeros((L, PAIR), BF16)
    vec = lambda i: vec_ref[i:i + 1, :]

    def hsum(x):
        sa = jnp.sum(jnp.where(head_a, x, 0.0), axis=-1, keepdims=True)
        sb = jnp.sum(jnp.where(head_a, 0.0, x), axis=-1, keepdims=True)
        return jnp.where(head_a, sa, sb)

    def sl(x, p):
        return x[:, p * PAIR:(p + 1) * PAIR]

    m_a = lambda x: jnp.where(head_a, x, jnp.zeros_like(x))
    m_b = lambda x: jnp.where(head_a, jnp.zeros_like(x), x)

    def heads_rows(*cols):
        return cat([cat([m_a(x) for x in cols], axis=1), cat([m_b(x) for x in cols], axis=1)], axis=0)

    def prep(p_ref, ops):
        pr = p_ref[:, 0:RWKV_DIM]
        pk = p_ref[:, RWKV_DIM:2 * RWKV_DIM]
        pv = p_ref[:, 2 * RWKV_DIM:3 * RWKV_DIM]
        wd = jnp.tanh(p_ref[:, LORA_WD_OFF:LORA_WD_OFF + 128])
        lw = -DECAY_SCALE * jax.nn.sigmoid(vec(VEC_W0) + _dot(wd, w2_ref[...]))
        yield
        a_lr = jax.nn.sigmoid(vec(VEC_A0) + _dot(p_ref[:, LORA_AD_OFF:LORA_AD_OFF + 128], a2_ref[...]))
        yield
        ops["gate"] = _dot(jax.nn.sigmoid(p_ref[:, LORA_GD_OFF:LORA_GD_OFF + 256]), g2_ref[...])
        yield
        kk = pk * vec(VEC_KK)
        kbar = pk * (1.0 + (a_lr - 1.0) * vec(VEC_KA))
        yield
        h1 = lw.astype(BF16)
        r1 = lw - h1.astype(F32)
        h2 = r1.astype(BF16)
        h3 = (r1 - h2.astype(F32)).astype(BF16)
        c = jnp.dot(tri3, cat([h1, h2, h3], axis=0), preferred_element_type=F32)
        yield
        cl = c[L - 1:L, :]
        einv = jnp.exp(-c)
        yield
        edec = jnp.exp(cl - c)
        yield
        rt = pr * jnp.exp(c)
        yield
        eprev = jnp.exp(c - lw)
        ops["gl"] = jnp.exp(cl)
        ops["rkr"] = pr * kbar * vec(VEC_RK)
        ops["rt"] = rt
        ops["pv"] = pv
        yield
        ss = [hsum(sl(kk, p) * sl(kk, p)) for p in pairs]
        yield
        kkn = [sl(kk, p) * lax.rsqrt(jnp.maximum(ss[p], 1e-24)) for p in pairs]
        bvec = [kkn[p] * sl(a_lr, p) for p in pairs]
        yield
        ops["at"] = [(-kkn[p] * sl(eprev, p)).astype(BF16) for p in pairs]
        ops["rtp"] = [sl(rt, p).astype(BF16) for p in pairs]
        yield
        ops["bt"] = [(bvec[p] * sl(einv, p)).astype(BF16) for p in pairs]
        ops["kt"] = [(sl(kbar, p) * sl(einv, p)).astype(BF16) for p in pairs]
        yield
        ops["bh"] = [bvec[p] * sl(edec, p) for p in pairs]
        ops["kh"] = [sl(kbar, p) * sl(edec, p) for p in pairs]
        ops["vp"] = [sl(pv, p).astype(BF16) for p in pairs]

    def main(ops):
        at, rtp, bt, kt, bh, kh, vp = (ops[k] for k in ("at", "rtp", "bt", "kt", "bh", "kh", "vp"))
        sc = [_dot_nt(cat([at[p], rtp[p]], axis=0), cat([m_a(bt[p]), m_b(kt[p]), m_a(kt[p]), m_b(bt[p])], axis=0))
              for p in pairs]
        yield
        sc = [jnp.where(tri_mask, sc[p], 0.0) for p in pairs]
        npk = [jnp.where(head_a, sc[p][:L, :PAIR], sc[p][:L, PAIR:]) for p in pairs]
        aak = [jnp.where(head_a, sc[p][:L, PAIR:], sc[p][:L, :PAIR]) for p in pairs]
        bot_sc = [sc[p][L:] for p in pairs]
        yield
        aakv = [_dot(aak[p], heads_rows(vp[p])).astype(BF16) for p in pairs]
        yield
        tpk = [eye_pk for p in pairs]
        for _ in range(6):
            res = [_dot(npk[p], heads_rows(npk[p].astype(BF16), tpk[p].astype(BF16))) for p in pairs]
            npk = [res[p][:, :PAIR] for p in pairs]
            tpk = [tpk[p] + res[p][:, PAIR:] for p in pairs]
            yield
        pq = [_dot(tpk[p], heads_rows(at[p], aakv[p])) for p in pairs]
        yield
        pb = [pq[p][:, :PAIR].astype(BF16) for p in pairs]
        qb = [pq[p][:, PAIR:].astype(BF16) for p in pairs]
        rhs = [cat([cat([m_a(pb[p]), m_a(qb[p])], axis=1), cat([zb, m_b(vp[p])], axis=1),
                    cat([zb, m_a(vp[p])], axis=1), cat([m_b(pb[p]), m_b(qb[p])], axis=1)], axis=0) for p in pairs]
        zt1 = [cat([bh[p], kh[p]], axis=0).T for p in pairs]
        zt2 = [cat([kh[p], bh[p]], axis=0).T for p in pairs]
        yield
        lhs_st = [cat([jnp.where(head_a, zt1[p][:L], zt1[p][L:]), jnp.where(head_a, zt2[p][:L], zt2[p][L:])],
                      axis=1) for p in pairs]
        tb = [_dot(cat([bot_sc[p], lhs_st[p]], axis=0), rhs[p]) for p in pairs]
        yield
        st = [st_ref[p].astype(BF16) for p in pairs]
        ys = [_dot(cat([sl(ops["rt"], p) + tb[p][:L, :PAIR],
                        jnp.where(eye_pk > 0.0, sl(ops["gl"], p), 0.0) + tb[p][L:, :PAIR]], axis=0),
                   heads_rows(st[p])) for p in pairs]
        yield
        for p in pairs:
            st_ref[p] = ys[p][L:] + tb[p][L:, PAIR:]
        ops["y"] = [ys[p][:L] + tb[p][:L, PAIR:] for p in pairs]

    def norm(ops, y_ref):
        y = ops["y"]
        mean = [hsum(y[p]) * (1.0 / HEAD_DIM) for p in pairs]
        yield
        dev = [y[p] - mean[p] for p in pairs]
        var = [hsum(dev[p] * dev[p]) * (1.0 / HEAD_DIM) for p in pairs]
        yield
        bonus = [hsum(sl(ops["rkr"], p)) * sl(ops["pv"], p) for p in pairs]
        yield
        for p in pairs:
            cs = slice(p * PAIR, (p + 1) * PAIR)
            yn = (dev[p] * lax.rsqrt(var[p] + GN_EPS) * vec_ref[VEC_GNW:VEC_GNW + 1, cs]
                  + vec_ref[VEC_GNB:VEC_GNB + 1, cs])
            y_ref[:, cs] = ((yn + bonus[p]) * sl(ops["gate"], p)).astype(y_ref.dtype)
            if p % 2 == 1:
                yield

    return prep, main, norm


def _rwkv_kernel(x_ref, xn_ref, g_ref, wm_ref, wl_ref, mum_ref, mul_ref, wqkv_ref, vec_ref, w2_ref, a2_ref, g2_ref,
                 y_ref, qkv_ref, p_scr, n_scr, last_ref, st_ref):
    tm = x_ref.shape[0]
    n_chunks = tm // CHUNK
    t = pl.program_id(1)
    slot = lax.rem(t, 2)
    row = lax.broadcasted_iota(jnp.int32, (tm, RW_TN), 0)

    def norm_x(src_ref, dst_slot):
        n_scr[dst_slot] = _rmsnorm_f32(src_ref[...], g_ref[...]).astype(BF16)
        yield

    def rw_proj(src_slot, dst_slot):
        n = n_scr[src_slot]
        blocks = [(wm_ref, mum_ref, j, j) for j in range(wm_ref.shape[1] // RW_TN)]
        blocks += [(wl_ref, mul_ref, j, wm_ref.shape[1] // RW_TN + j) for j in range(wl_ref.shape[1] // RW_TN)]
        for w_ref, mu_ref, j, jd in blocks:
            cs = slice(j * RW_TN, (j + 1) * RW_TN)
            cd = slice(jd * RW_TN, (jd + 1) * RW_TN)
            p = jnp.dot(n, w_ref[:, cs], preferred_element_type=F32)
            prev = jnp.where(row == 0, last_ref[:, cd], pltpu.roll(p, 1, 0))
            last_ref[:, cd] = p[tm - 1:tm, :]
            p_scr[dst_slot, :, cd] = p + mu_ref[:, cs] * (prev - p)
            yield

    def qkv_proj():
        n = n_scr[slot]
        per = ATTN_GROUP_SLABS
        for j in range(wqkv_ref.shape[1] // ATTN_GROUP_DIM):
            res = jnp.dot(n, wqkv_ref[:, j * ATTN_GROUP_DIM:(j + 1) * ATTN_GROUP_DIM], preferred_element_type=F32)
            for s in range(per):
                qkv_ref[j * per + s] = res[:, s * LANES:(s + 1) * LANES]
            yield

    @pl.when(t == 0)
    def _init():
        st_ref[...] = jnp.zeros_like(st_ref)
        last_ref[...] = jnp.zeros_like(last_ref)
        _interleave(norm_x(x_ref, 0))
        _interleave(rw_proj(0, 0))

    def background():
        yield from qkv_proj()
        yield from norm_x(xn_ref, 1 - slot)
        yield from rw_proj(1 - slot, 1 - slot)

    prep, main, norm = _rwkv_phases(vec_ref, w2_ref, a2_ref, g2_ref, st_ref)
    rows = [pl.ds(u * CHUNK, CHUNK) for u in range(n_chunks)]
    ops = [{} for _ in range(n_chunks)]
    prep_u = lambda u: prep(p_scr.at[slot, rows[u]], ops[u])
    norm_u = lambda u: norm(ops[u], y_ref.at[rows[u]])
    bg = background()
    _interleave(*(prep_u(u) for u in range(n_chunks)), background=bg)
    _interleave(*(_delayed(main(ops[u]), 2 * u) for u in range(n_chunks)),
                background=_every(bg, RW_BG_EVERY))
    _interleave(*(norm_u(u) for u in range(n_chunks)), background=bg)
    _interleave(bg)


def _rwkv(x, g, w_main, w_lora, mu_main, mu_lora, w_qkv, vecs, w2p, a2p, g2p, batch, seq):
    m, d = x.shape
    nt = seq // RW_TM
    slabs = w_qkv.shape[1] // LANES
    assert w_main.shape[1] + w_lora.shape[1] == RW_WIDTH
    full = lambda a: pl.BlockSpec(a.shape, lambda b, t: (0, 0), pipeline_mode=pl.Buffered(1))
    return pl.pallas_call(
        _rwkv_kernel,
        out_shape=(jax.ShapeDtypeStruct((m, RWKV_DIM), BF16), jax.ShapeDtypeStruct((slabs, m, LANES), F32)),
        grid=(batch, nt),
        in_specs=[pl.BlockSpec((RW_TM, d), lambda b, t: (b * nt + t, 0)),
                  pl.BlockSpec((RW_TM, d), lambda b, t: (b * nt + jnp.minimum(t + 1, nt - 1), 0)),
                  full(g), full(w_main), full(w_lora), full(mu_main), full(mu_lora), full(w_qkv), full(vecs),
                  full(w2p), full(a2p), full(g2p)],
        out_specs=(pl.BlockSpec((RW_TM, RWKV_DIM), lambda b, t: (b * nt + t, 0)),
                   pl.BlockSpec((slabs, RW_TM, LANES), lambda b, t: (0, b * nt + t, 0))),
        scratch_shapes=[pltpu.VMEM((2, RW_TM, RW_WIDTH), F32), pltpu.VMEM((2, RW_TM, d), BF16),
                        pltpu.VMEM((1, RW_WIDTH), F32), pltpu.VMEM((N_PAIRS, HEAD_DIM, PAIR), F32)],
        compiler_params=_params("parallel", "arbitrary"),
        name="rwkv7_proj_chunk_scan",
    )(x, x, g, w_main, w_lora, mu_main, mu_lora, w_qkv, vecs, w2p, a2p, g2p)


def _dil_attn_kernel(q_ref, kp_ref, k_ref, vp_ref, v_ref, o_ref, l_ref, *, window, dilation, slopes):
    blk = ATTN_BLK
    width = ATTN_GROUP_DIM
    n_win = q_ref.shape[1] // window
    lane_head = lax.broadcasted_iota(jnp.int32, (blk, width), 1) >> 6
    shape = (HEADS_PER_GROUP * blk, 2 * blk)
    rowi = lax.broadcasted_iota(jnp.int32, shape, 0)
    ki = lax.broadcasted_iota(jnp.int32, shape, 1)
    steps = (rowi & (blk - 1)) + blk - ki
    in_band = (steps >= 0) & (steps <= blk)
    slope = jnp.full(shape, slopes[0], F32)
    for h in range(1, HEADS_PER_GROUP):
        slope = jnp.where(rowi >= h * blk, slopes[h], slope)
    alibi = (-LOG2E) * slope * (steps * dilation).astype(F32)
    bias = jnp.where(in_band, alibi, -jnp.inf)
    first_key = jnp.where(pl.program_id(1) == 0, blk, 0)
    bias_w0 = jnp.where(in_band & (ki >= first_key), alibi, -jnp.inf)

    def load(ref, rows):
        return jnp.concatenate([ref[s, rows, :] for s in range(ATTN_GROUP_SLABS)], axis=1)

    def unit(j, r):
        rows = pl.ds(j * window + r, blk, stride=dilation)
        q = (load(q_ref, rows) * (LOG2E * HEAD_DIM ** -0.5)).astype(BF16)
        if j == 0:
            prev = pl.ds(r, blk, stride=dilation)
            k = jnp.concatenate([load(kp_ref, prev), load(k_ref, rows)], axis=0).astype(BF16)
            v = jnp.concatenate([load(vp_ref, prev), load(v_ref, rows)], axis=0).astype(BF16)
            b = bias_w0
        else:
            both = pl.ds((j - 1) * window + r, 2 * blk, stride=dilation)
            k = load(k_ref, both).astype(BF16)
            v = load(v_ref, both).astype(BF16)
            b = bias
        yield
        qs = jnp.concatenate([jnp.where(lane_head == h, q, jnp.zeros_like(q)) for h in range(HEADS_PER_GROUP)],
                             axis=0)
        s = lax.dot_general(qs, k, (((1,), (1,)), ((), ())), preferred_element_type=F32) + b
        yield
        mx = jnp.max(s, axis=-1, keepdims=True)
        yield
        e = lax.exp2(s - mx)
        yield
        den = jnp.sum(e, axis=-1, keepdims=True)
        lse = mx * (1.0 / LOG2E) + jnp.log(den)
        yield
        prob = (e / den).astype(BF16)
        yield
        o = jnp.zeros((blk, width), F32)
        lb = jnp.zeros((blk, width), F32)
        for h in range(HEADS_PER_GROUP):
            oh = jnp.dot(prob[h * blk:(h + 1) * blk], v, preferred_element_type=F32)
            o = jnp.where(lane_head == h, oh, o)
            lb = jnp.where(lane_head == h, lse[h * blk:(h + 1) * blk], lb)
        yield
        for s_ in range(ATTN_GROUP_SLABS):
            o_ref[s_, rows, :] = o[:, s_ * LANES:(s_ + 1) * LANES]
            l_ref[s_, rows, :] = lb[:, s_ * LANES:(s_ + 1) * LANES]

    def residues(i, carry):
        units = [(j, i * per_iter + dr) for dr in range(per_iter) for j in range(n_win)]
        for a in range(0, len(units), ATTN_UNROLL):
            _interleave(*(unit(j, r) for j, r in units[a:a + ATTN_UNROLL]))
        return carry

    per_iter = max(min(dilation, ATTN_UNROLL // n_win), 1)
    if dilation == per_iter:
        residues(0, 0)
    else:
        lax.fori_loop(0, dilation // per_iter, residues, 0)


def _alibi_slopes(n_heads):
    return [2.0 ** (-8.0 * (h + 1.0) / n_heads) for h in range(n_heads)]


def _dil_attention(qkv, group, batch, seq):
    window, dilation = DIL_GROUPS[group]
    assert window // dilation == ATTN_BLK and seq % window == 0
    m = qkv.shape[1]
    tile = max(window, ATTN_TILE)
    assert seq % tile == 0
    n_win = tile // window
    ntile = seq // tile
    slopes = tuple(_alibi_slopes(ATTN_HEADS)[group * HEADS_PER_GROUP:(group + 1) * HEADS_PER_GROUP])
    groups = ATTN_DIM // ATTN_GROUP_DIM
    cur = lambda col: pl.BlockSpec((ATTN_GROUP_SLABS, tile, LANES), lambda b, i: (col, b * ntile + i, 0))
    prev = lambda col: pl.BlockSpec((ATTN_GROUP_SLABS, window, LANES),
                                    lambda b, i: (col, jnp.maximum((b * ntile + i) * n_win - 1, 0), 0))
    out_sds = jax.ShapeDtypeStruct((ATTN_GROUP_SLABS, m, LANES), F32)
    out_spec = pl.BlockSpec((ATTN_GROUP_SLABS, tile, LANES), lambda b, i: (0, b * ntile + i, 0))
    return pl.pallas_call(
        functools.partial(_dil_attn_kernel, window=window, dilation=dilation, slopes=slopes),
        out_shape=(out_sds, out_sds),
        grid=(batch, ntile),
        in_specs=[cur(group), prev(groups + group), cur(groups + group),
                  prev(2 * groups + group), cur(2 * groups + group)],
        out_specs=(out_spec, out_spec),
        compiler_params=_params("parallel", "arbitrary"),
        name=f"dilated_attn_g{group}",
    )(qkv, qkv, qkv, qkv, qkv)


def _merge_kernel(y_ref, o0_ref, o1_ref, o2_ref, l0_ref, l1_ref, l2_ref, x_ref,
                  g_ref, wg_ref, pr_ref, pa_ref, wo_ref, h_ref):
    d = x_ref.shape[1]
    gates = _dot(_rmsnorm_f32(x_ref[...], g_ref[...]), wg_ref[...])
    t_rwkv = jnp.dot(y_ref[...], pr_ref[...], preferred_element_type=F32)
    wide = lambda ref: jnp.concatenate([ref[s] for s in range(ATTN_GROUP_SLABS)], axis=1)
    l0, l1, l2 = wide(l0_ref), wide(l1_ref), wide(l2_ref)
    mx = jnp.maximum(jnp.maximum(l0, l1), l2)
    e0, e1, e2 = jnp.exp(l0 - mx), jnp.exp(l1 - mx), jnp.exp(l2 - mx)
    y_attn = (e0 * wide(o0_ref) + e1 * wide(o1_ref) + e2 * wide(o2_ref)) / (e0 + e1 + e2)
    t_attn = _dot(y_attn, pa_ref[...])
    merged = jax.nn.sigmoid(gates[:, :d]) * t_rwkv + jax.nn.sigmoid(gates[:, d:]) * t_attn
    h_ref[...] = x_ref[...] + _dot(merged, wo_ref[...])


def _merge(y_rwkv, attn, x, g, wg, p_rwkv, p_attn, w_out, tm):
    m, d = x.shape
    row = lambda w: pl.BlockSpec((tm, w), lambda i: (i, 0))
    slab = pl.BlockSpec((ATTN_GROUP_SLABS, tm, LANES), lambda i: (0, i, 0))
    full = lambda a: pl.BlockSpec(a.shape, lambda i: (0, 0), pipeline_mode=pl.Buffered(1))
    (o0, l0), (o1, l1), (o2, l2) = attn
    return pl.pallas_call(
        _merge_kernel,
        out_shape=jax.ShapeDtypeStruct((m, d), F32),
        grid=(m // tm,),
        in_specs=[row(d)] + [slab] * 6 + [row(d), full(g), full(wg), full(p_rwkv), full(p_attn), full(w_out)],
        out_specs=row(d),
        compiler_params=_params("parallel"),
        name="gated_merge",
    )(y_rwkv, o0, o1, o2, l0, l1, l2, x, g, wg, p_rwkv, p_attn, w_out)


def _mem_kv_kernel(mem_ref, g_ref, w_ref, o_ref):
    memn = _rmsnorm_f32(mem_ref[...], g_ref[...])
    o_ref[...] = _dot(memn, w_ref[...]).astype(o_ref.dtype)


def _mem_kv(mem2d, g, w_kv, mem_len):
    m, d = mem2d.shape
    n = w_kv.shape[1]
    return pl.pallas_call(
        _mem_kv_kernel,
        out_shape=jax.ShapeDtypeStruct((m, n), BF16),
        grid=(m // mem_len,),
        in_specs=[pl.BlockSpec((mem_len, d), lambda i: (i, 0)), pl.BlockSpec((1, d), lambda i: (0, 0)),
                  pl.BlockSpec((d, n), lambda i: (0, 0))],
        out_specs=pl.BlockSpec((mem_len, n), lambda i: (i, 0)),
        compiler_params=_params("parallel"),
        name="mem_kv_proj",
    )(mem2d, g.reshape(1, d), w_kv)


def _xattn_kernel(h_ref, g_ref, wq_ref, kv_ref, wo_ref, gf_ref, o_ref, xn_ref):
    d = h_ref.shape[1]
    hd = d // XATTN_HEADS
    h = h_ref[...]
    heads = range(XATTN_HEADS)
    cs = lambda hh: slice(hh * hd, (hh + 1) * hd)
    q = (_dot(_rmsnorm_f32(h, g_ref[...]), wq_ref[...]) * (LOG2E * hd ** -0.5)).astype(BF16)
    s = [_dot_nt(q[:, cs(hh)], kv_ref[:, cs(hh)]) for hh in heads]
    e = [lax.exp2(s[hh] - jnp.max(s[hh], axis=-1, keepdims=True)) for hh in heads]
    prob = [e[hh] / jnp.sum(e[hh], axis=-1, keepdims=True) for hh in heads]
    outs = [_dot(prob[hh], kv_ref[:, d + hh * hd:d + (hh + 1) * hd]) for hh in heads]
    h2 = h + _dot(jnp.concatenate(outs, axis=-1), wo_ref[...])
    o_ref[...] = h2
    xn_ref[...] = _rmsnorm_f32(h2, gf_ref[...]).astype(xn_ref.dtype)


def _xattn(h, g, wq, kv, wo, g_ffn, seq, mem_len, tm):
    m, d = h.shape
    tiles_per_seq = seq // tm
    full = lambda a: pl.BlockSpec(a.shape, lambda i: (0, 0), pipeline_mode=pl.Buffered(1))
    row = pl.BlockSpec((tm, d), lambda i: (i, 0))
    return pl.pallas_call(
        _xattn_kernel,
        out_shape=(jax.ShapeDtypeStruct((m, d), F32), jax.ShapeDtypeStruct((m, d), BF16)),
        grid=(m // tm,),
        in_specs=[row, full(g), full(wq), pl.BlockSpec((mem_len, 2 * d), lambda i: (i // tiles_per_seq, 0)),
                  full(wo), full(g_ffn)],
        out_specs=(row, row),
        compiler_params=_params("parallel"),
        name="mem_cross_attn",
    )(h, g, wq, kv, wo, g_ffn)


def _ffn_kernel(h_ref, xn_ref, w1_ref, w2_ref, gf_ref, o_ref):
    tm = h_ref.shape[0]
    f = w1_ref.shape[1]
    rows_per = tm // FFN_SPLIT

    def group(r):
        rows = pl.ds(r * rows_per, rows_per)
        xn = xn_ref[rows, :]
        acc = None
        for c in range(f // FFN_TF):
            cs = slice(c * FFN_TF, (c + 1) * FFN_TF)
            u = jnp.dot(xn, w1_ref[:, cs], preferred_element_type=F32)
            act = jnp.square(jnp.maximum(u, 0.0)).astype(BF16)
            part = jnp.dot(act, w2_ref[cs, :], preferred_element_type=F32)
            acc = part if acc is None else acc + part
            yield
        o_ref[rows, :] = _rmsnorm_f32(h_ref[rows, :] + acc, gf_ref[...])

    _interleave(*(_delayed(group(r), r) for r in range(FFN_SPLIT)))


def _ffn(h, xn, w1, w2, g_final, tm):
    m, d = h.shape
    full = lambda a: pl.BlockSpec(a.shape, lambda i: (0, 0), pipeline_mode=pl.Buffered(1))
    row = pl.BlockSpec((tm, d), lambda i: (i, 0))
    return pl.pallas_call(
        _ffn_kernel,
        out_shape=jax.ShapeDtypeStruct((m, d), F32),
        grid=(m // tm,),
        in_specs=[row, row, full(w1), full(w2), full(g_final)],
        out_specs=row,
        compiler_params=_params("parallel"),
        name="relu2_mlp_final_norm",
    )(h, xn, w1, w2, g_final)


def _pad_cols(a, width):
    return jnp.pad(a, ((0, 0), (0, width - a.shape[1])))


def _pad_rows(a, rows):
    return jnp.pad(a, ((0, rows - a.shape[0]), (0, 0)))


def kernel(x, mem, norm_mix_g, w_in, shift_mu, w0, w2, a0, a2, g2, k_k, k_a, r_k, gn_w, gn_b, p_rwkv, p_attn, w_out, norm_x_g, norm_mem_g, xa_wq, xa_wkv, xa_wo, norm_ffn_g, ffn_w1, ffn_w2, norm_final_g):
    batch, seq, d = x.shape
    mem_len = mem.shape[1]
    assert w_in.shape[0] == 1 and d == RWKV_DIM and seq % RW_TM == 0
    m = batch * seq
    x2 = x.reshape(m, d)

    w = w_in[0]
    c_wd, c_ad, c_gd = 3 * RWKV_DIM, 3 * RWKV_DIM + DECAY_LORA, 3 * RWKV_DIM + DECAY_LORA + AAA_LORA
    c_q = c_gd + GATE_LORA
    c_gate = c_q + 3 * ATTN_DIM
    lora_cols = lambda a: jnp.concatenate(
        [_pad_cols(a[:, c_wd:c_ad], 128), _pad_cols(a[:, c_ad:c_gd], 128), _pad_cols(a[:, c_gd:c_q], 256)], axis=1)
    mu = shift_mu[0].reshape(1, -1).astype(F32)
    w_main, w_lora = w[:, :c_wd].astype(BF16), lora_cols(w).astype(BF16)
    mu_main, mu_lora = mu[:, :c_wd], lora_cols(mu)
    w_qkv = w[:, c_q:c_gate].astype(BF16)
    w_gate = w[:, c_gate:].astype(BF16)
    vecs = jnp.stack([w0[0], a0[0], k_k[0], k_a[0], r_k[0].reshape(-1), gn_w[0], gn_b[0],
                      jnp.zeros((RWKV_DIM,), F32)]).astype(F32)
    w2p = _pad_rows(w2[0], 128).astype(BF16)
    a2p = _pad_rows(a2[0], 128).astype(BF16)
    g2p = _pad_rows(g2[0], 256).astype(BF16)

    g_mix = norm_mix_g[0].reshape(1, d)
    y_rwkv, qkv = _rwkv(x2, g_mix, w_main, w_lora, mu_main, mu_lora, w_qkv, vecs, w2p, a2p, g2p, batch, seq)
    attn = [_dil_attention(qkv, g, batch, seq) for g in range(len(DIL_GROUPS))]
    h1 = _merge(y_rwkv, attn, x2, g_mix, w_gate, p_rwkv[0].astype(BF16), p_attn[0].astype(BF16),
                w_out[0].astype(BF16), 512)
    kv = _mem_kv(mem.reshape(batch * mem_len, d), norm_mem_g[0], xa_wkv[0].astype(BF16), mem_len)
    h2, xn2 = _xattn(h1, norm_x_g[0].reshape(1, d), xa_wq[0].astype(BF16), kv, xa_wo[0].astype(BF16),
                     norm_ffn_g[0].reshape(1, d), seq, mem_len, 512)
    out = _ffn(h2, xn2, ffn_w1[0].astype(BF16), ffn_w2[0].astype(BF16), norm_final_g.reshape(1, d), FFN_TM)
    return out.reshape(batch, seq, d)
```

```python
import functools
import math

import jax
import jax.numpy as jnp
from jax import lax
from jax.experimental import pallas as pl
from jax.experimental.pallas import tpu as pltpu

F32 = jnp.float32
BF16 = jnp.bfloat16

LANES = 128
HEAD_DIM = 64
PAIR = 2 * HEAD_DIM
CHUNK = 64
RWKV_DIM = 1024
N_PAIRS = RWKV_DIM // PAIR
DECAY_LORA = 64
AAA_LORA = 64
GATE_LORA = 160
LORA_WD_OFF = 3 * RWKV_DIM
LORA_AD_OFF = LORA_WD_OFF + 128
LORA_GD_OFF = LORA_AD_OFF + 128
RW_WIDTH = LORA_GD_OFF + 256
RW_TM = 256
RW_GROUP = 4
RW_TN = 256
RW_BG_EVERY = (1, 5, 1)
GN_EPS = HEAD_DIM * 1e-5
LOG2E = 1.0 / math.log(2.0)
DECAY_SCALE = math.exp(-0.5)
NORM_EPS = 1e-6
DIL_GROUPS = ((128, 1), (512, 4), (2048, 16))
HEADS_PER_GROUP = 4
ATTN_HEADS = 12
ATTN_GROUP_DIM = HEADS_PER_GROUP * HEAD_DIM
ATTN_GROUP_SLABS = ATTN_GROUP_DIM // LANES
ATTN_DIM = ATTN_HEADS * HEAD_DIM
ATTN_BLK = 128
ATTN_TILE = 1024
ATTN_UNROLL = 4
XATTN_HEADS = 4
MERGE_TM = 1024
MERGE_SPLIT = 2
XATTN_TM = 1024
XATTN_SPLIT = 2
FFN_TM = 1024
FFN_SPLIT = 2
FFN_TF = 1024
VMEM_LIMIT = 56 * 1024 * 1024


def _dot(a, b):
    return jnp.dot(a.astype(BF16), b.astype(BF16), preferred_element_type=F32)


def _dot_nt(a, b):
    return lax.dot_general(a.astype(BF16), b.astype(BF16), (((1,), (1,)), ((), ())),
                           preferred_element_type=F32)


def _params(*sem):
    return pltpu.CompilerParams(dimension_semantics=sem, vmem_limit_bytes=VMEM_LIMIT)


def _rmsnorm_f32(x, g):
    return x * lax.rsqrt(jnp.mean(x * x, axis=-1, keepdims=True) + NORM_EPS) * g


VEC_W0, VEC_A0, VEC_KK, VEC_KA, VEC_RK, VEC_GNW, VEC_GNB = range(7)


def _interleave(*gens, background=None):
    live = list(gens)
    while live:
        for g in list(live):
            try:
                next(g)
            except StopIteration:
                live.remove(g)
        if background is not None:
            next(background, None)


def _every(gen, n):
    for _ in gen:
        for _ in range(n):
            yield


def _delayed(gen, steps):
    for _ in range(steps):
        yield
    yield from gen


def _rwkv_phases(vec_ref, w2_ref, a2_ref, g2_ref, st_ref):
    L = CHUNK
    pairs = range(N_PAIRS)
    cat = jnp.concatenate

    head_a = lax.broadcasted_iota(jnp.int32, (L, PAIR), 1) < HEAD_DIM
    rowh = lax.broadcasted_iota(jnp.int32, (L, PAIR), 0)
    laneh = lax.broadcasted_iota(jnp.int32, (L, PAIR), 1)
    eye_pk = jnp.where((laneh == rowh) | (laneh == rowh + HEAD_DIM), 1.0, 0.0)
    rows_sc = lax.broadcasted_iota(jnp.int32, (2 * L, 2 * PAIR), 0)
    src_sc = lax.broadcasted_iota(jnp.int32, (2 * L, 2 * PAIR), 1) & (L - 1)
    tri_mask = ((rows_sc < L) & (rows_sc > src_sc)) | ((rows_sc >= L) & (rows_sc - L >= src_sc))
    r3 = lax.broadcasted_iota(jnp.int32, (L, 3 * L), 0)
    c3 = lax.broadcasted_iota(jnp.int32, (L, 3 * L), 1)
    c3 = c3 - jnp.where(c3 >= L, L, 0) - jnp.where(c3 >= 2 * L, L, 0)
    tri3 = jnp.where(r3 >= c3, 1.0, 0.0).astype(BF16)
    zb = jnp.zeros((L, PAIR), BF16)
    vec = lambda i: vec_ref[i:i + 1, :]

    def hsum(x):
        sa = jnp.sum(jnp.where(head_a, x, 0.0), axis=-1, keepdims=True)
        sb = jnp.sum(jnp.where(head_a, 0.0, x), axis=-1, keepdims=True)
        return jnp.where(head_a, sa, sb)

    def sl(x, p):
        return x[:, p * PAIR:(p + 1) * PAIR]

    m_a = lambda x: jnp.where(head_a, x, jnp.zeros_like(x))
    m_b = lambda x: jnp.where(head_a, jnp.zeros_like(x), x)

    def heads_rows(*cols):
        return cat([cat([m_a(x) for x in cols], axis=1), cat([m_b(x) for x in cols], axis=1)], axis=0)

    def prep(p_ref, ops):
        pr = p_ref[:, 0:RWKV_DIM]
        pk = p_ref[:, RWKV_DIM:2 * RWKV_DIM]
        pv = p_ref[:, 2 * RWKV_DIM:3 * RWKV_DIM]
        wd = jnp.tanh(p_ref[:, LORA_WD_OFF:LORA_WD_OFF + 128])
        lw = -DECAY_SCALE * jax.nn.sigmoid(vec(VEC_W0) + _dot(wd, w2_ref[...]))
        yield
        a_lr = jax.nn.sigmoid(vec(VEC_A0) + _dot(p_ref[:, LORA_AD_OFF:LORA_AD_OFF + 128], a2_ref[...]))
        yield
        ops["gate"] = _dot(jax.nn.sigmoid(p_ref[:, LORA_GD_OFF:LORA_GD_OFF + 256]), g2_ref[...])
        yield
        kk = pk * vec(VEC_KK)
        kbar = pk * (1.0 + (a_lr - 1.0) * vec(VEC_KA))
        yield
        h1 = lw.astype(BF16)
        r1 = lw - h1.astype(F32)
        h2 = r1.astype(BF16)
        h3 = (r1 - h2.astype(F32)).astype(BF16)
        c = jnp.dot(tri3, cat([h1, h2, h3], axis=0), preferred_element_type=F32)
        yield
        cl = c[L - 1:L, :]
        einv = jnp.exp(-c)
        yield
        edec = jnp.exp(cl - c)
        yield
        rt = pr * jnp.exp(c)
        yield
        eprev = jnp.exp(c - lw)
        ops["gl"] = jnp.exp(cl)
        ops["rkr"] = pr * kbar * vec(VEC_RK)
        ops["rt"] = rt
        ops["pv"] = pv
        yield
        ss = [hsum(sl(kk, p) * sl(kk, p)) for p in pairs]
        yield
        kkn = [sl(kk, p) * lax.rsqrt(jnp.maximum(ss[p], 1e-24)) for p in pairs]
        bvec = [kkn[p] * sl(a_lr, p) for p in pairs]
        yield
        ops["at"] = [(-kkn[p] * sl(eprev, p)).astype(BF16) for p in pairs]
        ops["rtp"] = [sl(rt, p).astype(BF16) for p in pairs]
        yield
        ops["bt"] = [(bvec[p] * sl(einv, p)).astype(BF16) for p in pairs]
        ops["kt"] = [(sl(kbar, p) * sl(einv, p)).astype(BF16) for p in pairs]
        yield
        ops["bh"] = [bvec[p] * sl(edec, p) for p in pairs]
        ops["kh"] = [sl(kbar, p) * sl(edec, p) for p in pairs]
        ops["vp"] = [sl(pv, p).astype(BF16) for p in pairs]

    def main(ops):
        at, rtp, bt, kt, bh, kh, vp = (ops[k] for k in ("at", "rtp", "bt", "kt", "bh", "kh", "vp"))
        sc = [_dot_nt(cat([at[p], rtp[p]], axis=0), cat([m_a(bt[p]), m_b(kt[p]), m_a(kt[p]), m_b(bt[p])], axis=0))
              for p in pairs]
        yield
        sc = [jnp.where(tri_mask, sc[p], 0.0) for p in pairs]
        npk = [jnp.where(head_a, sc[p][:L, :PAIR], sc[p][:L, PAIR:]) for p in pairs]
        aak = [jnp.where(head_a, sc[p][:L, PAIR:], sc[p][:L, :PAIR]) for p in pairs]
        bot_sc = [sc[p][L:] for p in pairs]
        yield
        aakv = [_dot(aak[p], heads_rows(vp[p])).astype(BF16) for p in pairs]
        yield
        tpk = [eye_pk for p in pairs]
        for _ in range(6):
            res = [_dot(npk[p], heads_rows(npk[p].astype(BF16), tpk[p].astype(BF16))) for p in pairs]
            npk = [res[p][:, :PAIR] for p in pairs]
            tpk = [tpk[p] + res[p][:, PAIR:] for p in pairs]
            yield
        pq = [_dot(tpk[p], heads_rows(at[p], aakv[p])) for p in pairs]
        yield
        pb = [pq[p][:, :PAIR].astype(BF16) for p in pairs]
        qb = [pq[p][:, PAIR:].astype(BF16) for p in pairs]
        rhs = [cat([cat([m_a(pb[p]), m_a(qb[p])], axis=1), cat([zb, m_b(vp[p])], axis=1),
                    cat([zb, m_a(vp[p])], axis=1), cat([m_b(pb[p]), m_b(qb[p])], axis=1)], axis=0) for p in pairs]
        zt1 = [cat([bh[p], kh[p]], axis=0).T for p in pairs]
        zt2 = [cat([kh[p], bh[p]], axis=0).T for p in pairs]
        yield
        lhs_st = [cat([jnp.where(head_a, zt1[p][:L], zt1[p][L:]), jnp.where(head_a, zt2[p][:L], zt2[p][L:])],
                      axis=1) for p in pairs]
        tb = [_dot(cat([bot_sc[p], lhs_st[p]], axis=0), rhs[p]) for p in pairs]
        yield
        st = [st_ref[p].astype(BF16) for p in pairs]
        ys = [_dot(cat([sl(ops["rt"], p) + tb[p][:L, :PAIR],
                        jnp.where(eye_pk > 0.0, sl(ops["gl"], p), 0.0) + tb[p][L:, :PAIR]], axis=0),
                   heads_rows(st[p])) for p in pairs]
        yield
        for p in pairs:
            st_ref[p] = ys[p][L:] + tb[p][L:, PAIR:]
        ops["y"] = [ys[p][:L] + tb[p][:L, PAIR:] for p in pairs]

    def norm(ops, y_ref):
        y = ops["y"]
        mean = [hsum(y[p]) * (1.0 / HEAD_DIM) for p in pairs]
        yield
        dev = [y[p] - mean[p] for p in pairs]
        var = [hsum(dev[p] * dev[p]) * (1.0 / HEAD_DIM) for p in pairs]
        yield
        bonus = [hsum(sl(ops["rkr"], p)) * sl(ops["pv"], p) for p in pairs]
        yield
        for p in pairs:
            cs = slice(p * PAIR, (p + 1) * PAIR)
            yn = (dev[p] * lax.rsqrt(var[p] + GN_EPS) * vec_ref[VEC_GNW:VEC_GNW + 1, cs]
                  + vec_ref[VEC_GNB:VEC_GNB + 1, cs])
            y_ref[:, cs] = ((yn + bonus[p]) * sl(ops["gate"], p)).astype(y_ref.dtype)
            if p % 2 == 1:
                yield

    return prep, main, norm


def _rwkv_kernel(x_ref, xn_ref, g_ref, wm_ref, wl_ref, mum_ref, mul_ref, wqkv_ref, vec_ref, w2_ref, a2_ref, g2_ref,
                 y_ref, qkv_ref, p_scr, n_scr, last_ref, st_ref):
    tm = x_ref.shape[0]
    n_chunks = tm // CHUNK
    t = pl.program_id(1)
    slot = lax.rem(t, 2)
    row = lax.broadcasted_iota(jnp.int32, (tm, RW_TN), 0)

    def norm_x(src_ref, dst_slot):
        n_scr[dst_slot] = _rmsnorm_f32(src_ref[...], g_ref[...]).astype(BF16)
        yield

    def rw_proj(src_slot, dst_slot):
        n = n_scr[src_slot]
        blocks = [(wm_ref, mum_ref, j, j) for j in range(wm_ref.shape[1] // RW_TN)]
        blocks += [(wl_ref, mul_ref, j, wm_ref.shape[1] // RW_TN + j) for j in range(wl_ref.shape[1] // RW_TN)]
        for w_ref, mu_ref, j, jd in blocks:
            cs = slice(j * RW_TN, (j + 1) * RW_TN)
            cd = slice(jd * RW_TN, (jd + 1) * RW_TN)
            p = jnp.dot(n, w_ref[:, cs], preferred_element_type=F32)
            prev = jnp.where(row == 0, last_ref[:, cd], pltpu.roll(p, 1, 0))
            last_ref[:, cd] = p[tm - 1:tm, :]
            p_scr[dst_slot, :, cd] = p + mu_ref[:, cs] * (prev - p)
            yield

    def qkv_proj():
        n = n_scr[slot]
        per = ATTN_GROUP_SLABS
        for j in range(wqkv_ref.shape[1] // ATTN_GROUP_DIM):
            res = jnp.dot(n, wqkv_ref[:, j * ATTN_GROUP_DIM:(j + 1) * ATTN_GROUP_DIM], preferred_element_type=F32)
            for s in range(per):
                qkv_ref[j * per + s] = res[:, s * LANES:(s + 1) * LANES]
            yield

    @pl.when(t == 0)
    def _init():
        st_ref[...] = jnp.zeros_like(st_ref)
        last_ref[...] = jnp.zeros_like(last_ref)
        _interleave(norm_x(x_ref, 0))
        _interleave(rw_proj(0, 0))

    def background():
        yield from qkv_proj()
        yield from norm_x(xn_ref, 1 - slot)
        yield from rw_proj(1 - slot, 1 - slot)

    prep, main, norm = _rwkv_phases(vec_ref, w2_ref, a2_ref, g2_ref, st_ref)
    rows = [pl.ds(u * CHUNK, CHUNK) for u in range(n_chunks)]
    ops = [{} for _ in range(n_chunks)]
    prep_u = lambda u: prep(p_scr.at[slot, rows[u]], ops[u])
    norm_u = lambda u: norm(ops[u], y_ref.at[rows[u]])
    bg = background()
    for first in range(0, n_chunks, RW_GROUP):
        group = range(first, first + RW_GROUP)
        _interleave(*(prep_u(u) for u in group), background=_every(bg, RW_BG_EVERY[0]))
        _interleave(*(_delayed(main(ops[u]), 2 * (u - first)) for u in group),
                    background=_every(bg, RW_BG_EVERY[1]))
        _interleave(*(norm_u(u) for u in group), background=_every(bg, RW_BG_EVERY[2]))
    _interleave(bg)


def _rwkv(x, g, w_main, w_lora, mu_main, mu_lora, w_qkv, vecs, w2p, a2p, g2p, batch, seq):
    m, d = x.shape
    nt = seq // RW_TM
    slabs = w_qkv.shape[1] // LANES
    assert w_main.shape[1] + w_lora.shape[1] == RW_WIDTH
    full = lambda a: pl.BlockSpec(a.shape, lambda b, t: (0, 0), pipeline_mode=pl.Buffered(1))
    return pl.pallas_call(
        _rwkv_kernel,
        out_shape=(jax.ShapeDtypeStruct((m, RWKV_DIM), BF16), jax.ShapeDtypeStruct((slabs, m, LANES), F32)),
        grid=(batch, nt),
        in_specs=[pl.BlockSpec((RW_TM, d), lambda b, t: (b * nt + t, 0)),
                  pl.BlockSpec((RW_TM, d), lambda b, t: (b * nt + jnp.minimum(t + 1, nt - 1), 0)),
                  full(g), full(w_main), full(w_lora), full(mu_main), full(mu_lora), full(w_qkv), full(vecs),
                  full(w2p), full(a2p), full(g2p)],
        out_specs=(pl.BlockSpec((RW_TM, RWKV_DIM), lambda b, t: (b * nt + t, 0)),
                   pl.BlockSpec((slabs, RW_TM, LANES), lambda b, t: (0, b * nt + t, 0))),
        scratch_shapes=[pltpu.VMEM((2, RW_TM, RW_WIDTH), F32), pltpu.VMEM((2, RW_TM, d), BF16),
                        pltpu.VMEM((1, RW_WIDTH), F32), pltpu.VMEM((N_PAIRS, HEAD_DIM, PAIR), F32)],
        compiler_params=_params("parallel", "arbitrary"),
        name="rwkv7_proj_chunk_scan",
    )(x, x, g, w_main, w_lora, mu_main, mu_lora, w_qkv, vecs, w2p, a2p, g2p)


def _dil_attn_kernel(q_ref, kp_ref, k_ref, vp_ref, v_ref, o_ref, l_ref, *, window, dilation, slopes):
    blk = ATTN_BLK
    width = ATTN_GROUP_DIM
    n_win = q_ref.shape[1] // window
    lane_head = lax.broadcasted_iota(jnp.int32, (blk, width), 1) >> 6
    shape = (HEADS_PER_GROUP * blk, 2 * blk)
    rowi = lax.broadcasted_iota(jnp.int32, shape, 0)
    ki = lax.broadcasted_iota(jnp.int32, shape, 1)
    steps = (rowi & (blk - 1)) + blk - ki
    in_band = (steps >= 0) & (steps <= blk)
    slope = jnp.full(shape, slopes[0], F32)
    for h in range(1, HEADS_PER_GROUP):
        slope = jnp.where(rowi >= h * blk, slopes[h], slope)
    alibi = (-LOG2E) * slope * (steps * dilation).astype(F32)
    bias = jnp.where(in_band, alibi, -jnp.inf)
    first_key = jnp.where(pl.program_id(1) == 0, blk, 0)
    bias_w0 = jnp.where(in_band & (ki >= first_key), alibi, -jnp.inf)

    def load(ref, rows):
        return jnp.concatenate([ref[s, rows, :] for s in range(ATTN_GROUP_SLABS)], axis=1)

    def unit(j, r):
        rows = pl.ds(j * window + r, blk, stride=dilation)
        q = (load(q_ref, rows) * (LOG2E * HEAD_DIM ** -0.5)).astype(BF16)
        if j == 0:
            prev = pl.ds(r, blk, stride=dilation)
            k = jnp.concatenate([load(kp_ref, prev), load(k_ref, rows)], axis=0).astype(BF16)
            v = jnp.concatenate([load(vp_ref, prev), load(v_ref, rows)], axis=0).astype(BF16)
            b = bias_w0
        else:
            both = pl.ds((j - 1) * window + r, 2 * blk, stride=dilation)
            k = load(k_ref, both).astype(BF16)
            v = load(v_ref, both).astype(BF16)
            b = bias
        yield
        qs = jnp.concatenate([jnp.where(lane_head == h, q, jnp.zeros_like(q)) for h in range(HEADS_PER_GROUP)],
                             axis=0)
        s = lax.dot_general(qs, k, (((1,), (1,)), ((), ())), preferred_element_type=F32) + b
        yield
        mx = jnp.max(s, axis=-1, keepdims=True)
        yield
        e = lax.exp2(s - mx)
        yield
        den = jnp.sum(e, axis=-1, keepdims=True)
        lse = mx * (1.0 / LOG2E) + jnp.log(den)
        yield
        prob = (e / den).astype(BF16)
        yield
        o = jnp.dot(prob[:blk], v, preferred_element_type=F32)
        lb = jnp.broadcast_to(lse[:blk], (blk, width))
        for h in range(1, HEADS_PER_GROUP):
            oh = jnp.dot(prob[h * blk:(h + 1) * blk], v, preferred_element_type=F32)
            o = jnp.where(lane_head == h, oh, o)
            lb = jnp.where(lane_head == h, lse[h * blk:(h + 1) * blk], lb)
        yield
        for s_ in range(ATTN_GROUP_SLABS):
            o_ref[s_, rows, :] = o[:, s_ * LANES:(s_ + 1) * LANES]
            l_ref[s_, rows, :] = lb[:, s_ * LANES:(s_ + 1) * LANES]

    def residues(i, carry):
        units = [(j, i * per_iter + dr) for dr in range(per_iter) for j in range(n_win)]
        for a in range(0, len(units), ATTN_UNROLL):
            _interleave(*(unit(j, r) for j, r in units[a:a + ATTN_UNROLL]))
        return carry

    per_iter = max(min(dilation, ATTN_UNROLL // n_win), 1)
    if dilation == per_iter:
        residues(0, 0)
    else:
        lax.fori_loop(0, dilation // per_iter, residues, 0)


def _alibi_slopes(n_heads):
    return [2.0 ** (-8.0 * (h + 1.0) / n_heads) for h in range(n_heads)]


def _dil_attention(qkv, group, batch, seq):
    window, dilation = DIL_GROUPS[group]
    assert window // dilation == ATTN_BLK and seq % window == 0
    m = qkv.shape[1]
    tile = max(window, ATTN_TILE)
    assert seq % tile == 0
    n_win = tile // window
    ntile = seq // tile
    slopes = tuple(_alibi_slopes(ATTN_HEADS)[group * HEADS_PER_GROUP:(group + 1) * HEADS_PER_GROUP])
    groups = ATTN_DIM // ATTN_GROUP_DIM
    cur = lambda col: pl.BlockSpec((ATTN_GROUP_SLABS, tile, LANES), lambda b, i: (col, b * ntile + i, 0))
    prev = lambda col: pl.BlockSpec((ATTN_GROUP_SLABS, window, LANES),
                                    lambda b, i: (col, jnp.maximum((b * ntile + i) * n_win - 1, 0), 0))
    out_sds = jax.ShapeDtypeStruct((ATTN_GROUP_SLABS, m, LANES), F32)
    out_spec = pl.BlockSpec((ATTN_GROUP_SLABS, tile, LANES), lambda b, i: (0, b * ntile + i, 0))
    return pl.pallas_call(
        functools.partial(_dil_attn_kernel, window=window, dilation=dilation, slopes=slopes),
        out_shape=(out_sds, out_sds),
        grid=(batch, ntile),
        in_specs=[cur(group), prev(groups + group), cur(groups + group),
                  prev(2 * groups + group), cur(2 * groups + group)],
        out_specs=(out_spec, out_spec),
        compiler_params=_params("parallel", "arbitrary"),
        name=f"dilated_attn_g{group}",
    )(qkv, qkv, qkv, qkv, qkv)


def _merge_kernel(y_ref, o0_ref, o1_ref, o2_ref, l0_ref, l1_ref, l2_ref, x_ref,
                  g_ref, wg_ref, pr_ref, pa_ref, wo_ref, h_ref):
    d = x_ref.shape[1]
    rows_per = x_ref.shape[0] // MERGE_SPLIT

    def group(r):
        rows = pl.ds(r * rows_per, rows_per)
        wide = lambda ref: jnp.concatenate([ref[s, rows, :] for s in range(ATTN_GROUP_SLABS)], axis=1)
        x = x_ref[rows, :]
        gates = _dot(_rmsnorm_f32(x, g_ref[...]), wg_ref[...])
        yield
        t_rwkv = jnp.dot(y_ref[rows, :], pr_ref[...], preferred_element_type=F32)
        yield
        l0, l1, l2 = wide(l0_ref), wide(l1_ref), wide(l2_ref)
        mx = jnp.maximum(jnp.maximum(l0, l1), l2)
        e0, e1, e2 = jnp.exp(l0 - mx), jnp.exp(l1 - mx), jnp.exp(l2 - mx)
        y_attn = (e0 * wide(o0_ref) + e1 * wide(o1_ref) + e2 * wide(o2_ref)) / (e0 + e1 + e2)
        t_attn = _dot(y_attn, pa_ref[...])
        yield
        merged = jax.nn.sigmoid(gates[:, :d]) * t_rwkv
        yield
        merged = merged + jax.nn.sigmoid(gates[:, d:]) * t_attn
        yield
        h_ref[rows, :] = x + _dot(merged, wo_ref[...])

    _interleave(*(_delayed(group(r), 2 * r) for r in range(MERGE_SPLIT)))


def _merge(y_rwkv, attn, x, g, wg, p_rwkv, p_attn, w_out, tm):
    m, d = x.shape
    row = lambda w: pl.BlockSpec((tm, w), lambda i: (i, 0))
    slab = pl.BlockSpec((ATTN_GROUP_SLABS, tm, LANES), lambda i: (0, i, 0))
    full = lambda a: pl.BlockSpec(a.shape, lambda i: (0, 0), pipeline_mode=pl.Buffered(1))
    (o0, l0), (o1, l1), (o2, l2) = attn
    return pl.pallas_call(
        _merge_kernel,
        out_shape=jax.ShapeDtypeStruct((m, d), F32),
        grid=(m // tm,),
        in_specs=[row(d)] + [slab] * 6 + [row(d), full(g), full(wg), full(p_rwkv), full(p_attn), full(w_out)],
        out_specs=row(d),
        compiler_params=_params("parallel"),
        name="gated_merge",
    )(y_rwkv, o0, o1, o2, l0, l1, l2, x, g, wg, p_rwkv, p_attn, w_out)


def _mem_kv_kernel(mem_ref, g_ref, w_ref, o_ref):
    memn = _rmsnorm_f32(mem_ref[...], g_ref[...])
    o_ref[...] = _dot(memn, w_ref[...]).astype(o_ref.dtype)


def _mem_kv(mem2d, g, w_kv, mem_len):
    m, d = mem2d.shape
    n = w_kv.shape[1]
    return pl.pallas_call(
        _mem_kv_kernel,
        out_shape=jax.ShapeDtypeStruct((m, n), BF16),
        grid=(m // mem_len,),
        in_specs=[pl.BlockSpec((mem_len, d), lambda i: (i, 0)), pl.BlockSpec((1, d), lambda i: (0, 0)),
                  pl.BlockSpec((d, n), lambda i: (0, 0))],
        out_specs=pl.BlockSpec((mem_len, n), lambda i: (i, 0)),
        compiler_params=_params("parallel"),
        name="mem_kv_proj",
    )(mem2d, g.reshape(1, d), w_kv)


def _xattn_kernel(h_ref, g_ref, wq_ref, kv_ref, wo_ref, gf_ref, o_ref, xn_ref):
    d = h_ref.shape[1]
    hd = d // XATTN_HEADS
    heads = range(XATTN_HEADS)
    cs = lambda hh: slice(hh * hd, (hh + 1) * hd)
    rows_per = h_ref.shape[0] // XATTN_SPLIT

    def group(r):
        rows = pl.ds(r * rows_per, rows_per)
        h = h_ref[rows, :]
        q = (_dot(_rmsnorm_f32(h, g_ref[...]), wq_ref[...]) * (LOG2E * hd ** -0.5)).astype(BF16)
        yield
        s = [_dot_nt(q[:, cs(hh)], kv_ref[:, cs(hh)]) for hh in heads]
        yield
        e = [lax.exp2(s[hh] - jnp.max(s[hh], axis=-1, keepdims=True)) for hh in heads]
        yield
        prob = [e[hh] / jnp.sum(e[hh], axis=-1, keepdims=True) for hh in heads]
        yield
        outs = [_dot(prob[hh], kv_ref[:, d + hh * hd:d + (hh + 1) * hd]) for hh in heads]
        yield
        h2 = h + _dot(jnp.concatenate(outs, axis=-1), wo_ref[...])
        o_ref[rows, :] = h2
        yield
        xn_ref[rows, :] = _rmsnorm_f32(h2, gf_ref[...]).astype(xn_ref.dtype)

    _interleave(*(_delayed(group(r), 2 * r) for r in range(XATTN_SPLIT)))


def _xattn(h, g, wq, kv, wo, g_ffn, seq, mem_len, tm):
    m, d = h.shape
    tiles_per_seq = seq // tm
    full = lambda a: pl.BlockSpec(a.shape, lambda i: (0, 0), pipeline_mode=pl.Buffered(1))
    row = pl.BlockSpec((tm, d), lambda i: (i, 0))
    return pl.pallas_call(
        _xattn_kernel,
        out_shape=(jax.ShapeDtypeStruct((m, d), F32), jax.ShapeDtypeStruct((m, d), BF16)),
        grid=(m // tm,),
        in_specs=[row, full(g), full(wq), pl.BlockSpec((mem_len, 2 * d), lambda i: (i // tiles_per_seq, 0)),
                  full(wo), full(g_ffn)],
        out_specs=(row, row),
        compiler_params=_params("parallel"),
        name="mem_cross_attn",
    )(h, g, wq, kv, wo, g_ffn)


def _ffn_kernel(h_ref, xn_ref, w1_ref, w2_ref, gf_ref, o_ref):
    tm = h_ref.shape[0]
    f = w1_ref.shape[1]
    rows_per = tm // FFN_SPLIT

    def group(r):
        rows = pl.ds(r * rows_per, rows_per)
        xn = xn_ref[rows, :]
        acc = None
        for c in range(f // FFN_TF):
            cs = slice(c * FFN_TF, (c + 1) * FFN_TF)
            u = jnp.dot(xn, w1_ref[:, cs], preferred_element_type=F32)
            act = jnp.square(jnp.maximum(u, 0.0)).astype(BF16)
            part = jnp.dot(act, w2_ref[cs, :], preferred_element_type=F32)
            acc = part if acc is None else acc + part
            yield
        o_ref[rows, :] = _rmsnorm_f32(h_ref[rows, :] + acc, gf_ref[...])

    _interleave(*(_delayed(group(r), r) for r in range(FFN_SPLIT)))


def _ffn(h, xn, w1, w2, g_final, tm):
    m, d = h.shape
    full = lambda a: pl.BlockSpec(a.shape, lambda i: (0, 0), pipeline_mode=pl.Buffered(1))
    row = pl.BlockSpec((tm, d), lambda i: (i, 0))
    return pl.pallas_call(
        _ffn_kernel,
        out_shape=jax.ShapeDtypeStruct((m, d), F32),
        grid=(m // tm,),
        in_specs=[row, row, full(w1), full(w2), full(g_final)],
        out_specs=row,
        compiler_params=_params("parallel"),
        name="relu2_mlp_final_norm",
    )(h, xn, w1, w2, g_final)


def _pad_cols(a, width):
    return jnp.pad(a, ((0, 0), (0, width - a.shape[1])))


def _pad_rows(a, rows):
    return jnp.pad(a, ((0, rows - a.shape[0]), (0, 0)))


def kernel(x, mem, norm_mix_g, w_in, shift_mu, w0, w2, a0, a2, g2, k_k, k_a, r_k, gn_w, gn_b, p_rwkv, p_attn, w_out, norm_x_g, norm_mem_g, xa_wq, xa_wkv, xa_wo, norm_ffn_g, ffn_w1, ffn_w2, norm_final_g):
    batch, seq, d = x.shape
    mem_len = mem.shape[1]
    assert w_in.shape[0] == 1 and d == RWKV_DIM and seq % RW_TM == 0
    m = batch * seq
    x2 = x.reshape(m, d)

    w = w_in[0]
    c_wd, c_ad, c_gd = 3 * RWKV_DIM, 3 * RWKV_DIM + DECAY_LORA, 3 * RWKV_DIM + DECAY_LORA + AAA_LORA
    c_q = c_gd + GATE_LORA
    c_gate = c_q + 3 * ATTN_DIM
    lora_cols = lambda a: jnp.concatenate(
        [_pad_cols(a[:, c_wd:c_ad], 128), _pad_cols(a[:, c_ad:c_gd], 128), _pad_cols(a[:, c_gd:c_q], 256)], axis=1)
    mu = shift_mu[0].reshape(1, -1).astype(F32)
    w_main, w_lora = w[:, :c_wd].astype(BF16), lora_cols(w).astype(BF16)
    mu_main, mu_lora = mu[:, :c_wd], lora_cols(mu)
    w_qkv = w[:, c_q:c_gate].astype(BF16)
    w_gate = w[:, c_gate:].astype(BF16)
    vecs = jnp.stack([w0[0], a0[0], k_k[0], k_a[0], r_k[0].reshape(-1), gn_w[0], gn_b[0],
                      jnp.zeros((RWKV_DIM,), F32)]).astype(F32)
    w2p = _pad_rows(w2[0], 128).astype(BF16)
    a2p = _pad_rows(a2[0], 128).astype(BF16)
    g2p = _pad_rows(g2[0], 256).astype(BF16)

    g_mix = norm_mix_g[0].reshape(1, d)
    y_rwkv, qkv = _rwkv(x2, g_mix, w_main, w_lora, mu_main, mu_lora, w_qkv, vecs, w2p, a2p, g2p, batch, seq)
    attn = [_dil_attention(qkv, g, batch, seq) for g in range(len(DIL_GROUPS))]
    h1 = _merge(y_rwkv, attn, x2, g_mix, w_gate, p_rwkv[0].astype(BF16), p_attn[0].astype(BF16),
                w_out[0].astype(BF16), MERGE_TM)
    kv = _mem_kv(mem.reshape(batch * mem_len, d), norm_mem_g[0], xa_wkv[0].astype(BF16), mem_len)
    h2, xn2 = _xattn(h1, norm_x_g[0].reshape(1, d), xa_wq[0].astype(BF16), kv, xa_wo[0].astype(BF16),
                     norm_ffn_g[0].reshape(1, d), seq, mem_len, XATTN_TM)
    out = _ffn(h2, xn2, ffn_w1[0].astype(BF16), ffn_w2[0].astype(BF16), norm_final_g.reshape(1, d), FFN_TM)
    return out.reshape(batch, seq, d)
```

```python
import functools
import math

import jax
import jax.numpy as jnp
from jax import lax
from jax.experimental import pallas as pl
from jax.experimental.pallas import tpu as pltpu

F32 = jnp.float32
BF16 = jnp.bfloat16

LANES = 128
HEAD_DIM = 64
PAIR = 2 * HEAD_DIM
CHUNK = 64
RWKV_DIM = 1024
N_PAIRS = RWKV_DIM // PAIR
DECAY_LORA = 64
AAA_LORA = 64
GATE_LORA = 160
LORA_WD_OFF = 3 * RWKV_DIM
LORA_AD_OFF = LORA_WD_OFF + 128
LORA_GD_OFF = LORA_AD_OFF + 128
RW_WIDTH = LORA_GD_OFF + 256
RW_TM = 256
RW_GROUP = 4
RW_TN = 256
RW_BG_EVERY = (1, 5, 1)
GN_EPS = HEAD_DIM * 1e-5
LOG2E = 1.0 / math.log(2.0)
DECAY_SCALE = math.exp(-0.5)
NORM_EPS = 1e-6
DIL_GROUPS = ((128, 1), (512, 4), (2048, 16))
HEADS_PER_GROUP = 4
ATTN_HEADS = 12
ATTN_GROUP_DIM = HEADS_PER_GROUP * HEAD_DIM
ATTN_GROUP_SLABS = ATTN_GROUP_DIM // LANES
ATTN_DIM = ATTN_HEADS * HEAD_DIM
ATTN_BLK = 128
ATTN_TILE = 2048
ATTN_UNROLL = 4
XATTN_HEADS = 4
MERGE_TM = 1024
MERGE_SPLIT = 2
XATTN_TM = 1024
XATTN_SPLIT = 2
FFN_TM = 1024
FFN_SPLIT = 2
FFN_TF = 1024
VMEM_LIMIT = 56 * 1024 * 1024


def _dot(a, b):
    return jnp.dot(a.astype(BF16), b.astype(BF16), preferred_element_type=F32)


def _dot_nt(a, b):
    return lax.dot_general(a.astype(BF16), b.astype(BF16), (((1,), (1,)), ((), ())),
                           preferred_element_type=F32)


def _params(*sem):
    return pltpu.CompilerParams(dimension_semantics=sem, vmem_limit_bytes=VMEM_LIMIT)


def _rmsnorm_f32(x, g):
    return x * lax.rsqrt(jnp.mean(x * x, axis=-1, keepdims=True) + NORM_EPS) * g


VEC_W0, VEC_A0, VEC_KK, VEC_KA, VEC_RK, VEC_GNW, VEC_GNB = range(7)


def _interleave(*gens, background=None):
    live = list(gens)
    while live:
        for g in list(live):
            try:
                next(g)
            except StopIteration:
                live.remove(g)
        if background is not None:
            next(background, None)


def _every(gen, n):
    for _ in gen:
        for _ in range(n):
            yield


def _delayed(gen, steps):
    for _ in range(steps):
        yield
    yield from gen


def _rwkv_phases(vec_ref, w2_ref, a2_ref, g2_ref, st_ref):
    L = CHUNK
    pairs = range(N_PAIRS)
    cat = jnp.concatenate

    head_a = lax.broadcasted_iota(jnp.int32, (L, PAIR), 1) < HEAD_DIM
    rowh = lax.broadcasted_iota(jnp.int32, (L, PAIR), 0)
    laneh = lax.broadcasted_iota(jnp.int32, (L, PAIR), 1)
    eye_pk = jnp.where((laneh == rowh) | (laneh == rowh + HEAD_DIM), 1.0, 0.0)
    rows_sc = lax.broadcasted_iota(jnp.int32, (2 * L, 2 * PAIR), 0)
    src_sc = lax.broadcasted_iota(jnp.int32, (2 * L, 2 * PAIR), 1) & (L - 1)
    tri_mask = ((rows_sc < L) & (rows_sc > src_sc)) | ((rows_sc >= L) & (rows_sc - L >= src_sc))
    r3 = lax.broadcasted_iota(jnp.int32, (L, 3 * L), 0)
    c3 = lax.broadcasted_iota(jnp.int32, (L, 3 * L), 1)
    c3 = c3 - jnp.where(c3 >= L, L, 0) - jnp.where(c3 >= 2 * L, L, 0)
    tri3 = jnp.where(r3 >= c3, 1.0, 0.0).astype(BF16)
    zb = jnp.zeros((L, PAIR), BF16)
    vec = lambda i: vec_ref[i:i + 1, :]

    def hsum(x):
        sa = jnp.sum(jnp.where(head_a, x, 0.0), axis=-1, keepdims=True)
        sb = jnp.sum(jnp.where(head_a, 0.0, x), axis=-1, keepdims=True)
        return jnp.where(head_a, sa, sb)

    def sl(x, p):
        return x[:, p * PAIR:(p + 1) * PAIR]

    m_a = lambda x: jnp.where(head_a, x, jnp.zeros_like(x))
    m_b = lambda x: jnp.where(head_a, jnp.zeros_like(x), x)

    def heads_rows(*cols):
        return cat([cat([m_a(x) for x in cols], axis=1), cat([m_b(x) for x in cols], axis=1)], axis=0)

    def prep(p_ref, ops):
        pr = p_ref[:, 0:RWKV_DIM]
        pk = p_ref[:, RWKV_DIM:2 * RWKV_DIM]
        pv = p_ref[:, 2 * RWKV_DIM:3 * RWKV_DIM]
        wd = jnp.tanh(p_ref[:, LORA_WD_OFF:LORA_WD_OFF + 128])
        lw = -DECAY_SCALE * jax.nn.sigmoid(vec(VEC_W0) + _dot(wd, w2_ref[...]))
        yield
        a_lr = jax.nn.sigmoid(vec(VEC_A0) + _dot(p_ref[:, LORA_AD_OFF:LORA_AD_OFF + 128], a2_ref[...]))
        yield
        ops["gate"] = _dot(jax.nn.sigmoid(p_ref[:, LORA_GD_OFF:LORA_GD_OFF + 256]), g2_ref[...])
        yield
        kk = pk * vec(VEC_KK)
        kbar = pk * (1.0 + (a_lr - 1.0) * vec(VEC_KA))
        yield
        h1 = lw.astype(BF16)
        r1 = lw - h1.astype(F32)
        h2 = r1.astype(BF16)
        h3 = (r1 - h2.astype(F32)).astype(BF16)
        c = jnp.dot(tri3, cat([h1, h2, h3], axis=0), preferred_element_type=F32)
        yield
        cl = c[L - 1:L, :]
        einv = jnp.exp(-c)
        yield
        edec = jnp.exp(cl - c)
        yield
        rt = pr * jnp.exp(c)
        yield
        eprev = jnp.exp(c - lw)
        ops["gl"] = jnp.exp(cl)
        ops["rkr"] = pr * kbar * vec(VEC_RK)
        ops["rt"] = rt
        ops["pv"] = pv
        yield
        ss = [hsum(sl(kk, p) * sl(kk, p)) for p in pairs]
        yield
        kkn = [sl(kk, p) * lax.rsqrt(jnp.maximum(ss[p], 1e-24)) for p in pairs]
        bvec = [kkn[p] * sl(a_lr, p) for p in pairs]
        yield
        ops["at"] = [(-kkn[p] * sl(eprev, p)).astype(BF16) for p in pairs]
        ops["rtp"] = [sl(rt, p).astype(BF16) for p in pairs]
        yield
        ops["bt"] = [(bvec[p] * sl(einv, p)).astype(BF16) for p in pairs]
        ops["kt"] = [(sl(kbar, p) * sl(einv, p)).astype(BF16) for p in pairs]
        yield
        ops["bh"] = [bvec[p] * sl(edec, p) for p in pairs]
        ops["kh"] = [sl(kbar, p) * sl(edec, p) for p in pairs]
        ops["vp"] = [sl(pv, p).astype(BF16) for p in pairs]

    def main(ops):
        at, rtp, bt, kt, bh, kh, vp = (ops[k] for k in ("at", "rtp", "bt", "kt", "bh", "kh", "vp"))
        sc = [_dot_nt(cat([at[p], rtp[p]], axis=0), cat([m_a(bt[p]), m_b(kt[p]), m_a(kt[p]), m_b(bt[p])], axis=0))
              for p in pairs]
        yield
        sc = [jnp.where(tri_mask, sc[p], 0.0) for p in pairs]
        npk = [jnp.where(head_a, sc[p][:L, :PAIR], sc[p][:L, PAIR:]) for p in pairs]
        aak = [jnp.where(head_a, sc[p][:L, PAIR:], sc[p][:L, :PAIR]) for p in pairs]
        bot_sc = [sc[p][L:] for p in pairs]
        yield
        aakv = [_dot(aak[p], heads_rows(vp[p])).astype(BF16) for p in pairs]
        yield
        tpk = [eye_pk for p in pairs]
        for _ in range(6):
            res = [_dot(npk[p], heads_rows(npk[p].astype(BF16), tpk[p].astype(BF16))) for p in pairs]
            npk = [res[p][:, :PAIR] for p in pairs]
            tpk = [tpk[p] + res[p][:, PAIR:] for p in pairs]
            yield
        pq = [_dot(tpk[p], heads_rows(at[p], aakv[p])) for p in pairs]
        yield
        pb = [pq[p][:, :PAIR].astype(BF16) for p in pairs]
        qb = [pq[p][:, PAIR:].astype(BF16) for p in pairs]
        rhs = [cat([cat([m_a(pb[p]), m_a(qb[p])], axis=1), cat([zb, m_b(vp[p])], axis=1),
                    cat([zb, m_a(vp[p])], axis=1), cat([m_b(pb[p]), m_b(qb[p])], axis=1)], axis=0) for p in pairs]
        zt1 = [cat([bh[p], kh[p]], axis=0).T for p in pairs]
        zt2 = [cat([kh[p], bh[p]], axis=0).T for p in pairs]
        yield
        lhs_st = [cat([jnp.where(head_a, zt1[p][:L], zt1[p][L:]), jnp.where(head_a, zt2[p][:L], zt2[p][L:])],
                      axis=1) for p in pairs]
        tb = [_dot(cat([bot_sc[p], lhs_st[p]], axis=0), rhs[p]) for p in pairs]
        yield
        st = [st_ref[p].astype(BF16) for p in pairs]
        ys = [_dot(cat([sl(ops["rt"], p) + tb[p][:L, :PAIR],
                        jnp.where(eye_pk > 0.0, sl(ops["gl"], p), 0.0) + tb[p][L:, :PAIR]], axis=0),
                   heads_rows(st[p])) for p in pairs]
        yield
        for p in pairs:
            st_ref[p] = ys[p][L:] + tb[p][L:, PAIR:]
        ops["y"] = [ys[p][:L] + tb[p][:L, PAIR:] for p in pairs]

    def norm(ops, y_ref):
        y = ops["y"]
        mean = [hsum(y[p]) * (1.0 / HEAD_DIM) for p in pairs]
        yield
        dev = [y[p] - mean[p] for p in pairs]
        var = [hsum(dev[p] * dev[p]) * (1.0 / HEAD_DIM) for p in pairs]
        yield
        bonus = [hsum(sl(ops["rkr"], p)) * sl(ops["pv"], p) for p in pairs]
        yield
        for p in pairs:
            cs = slice(p * PAIR, (p + 1) * PAIR)
            yn = (dev[p] * lax.rsqrt(var[p] + GN_EPS) * vec_ref[VEC_GNW:VEC_GNW + 1, cs]
                  + vec_ref[VEC_GNB:VEC_GNB + 1, cs])
            y_ref[:, cs] = ((yn + bonus[p]) * sl(ops["gate"], p)).astype(y_ref.dtype)
            if p % 2 == 1:
                yield

    return prep, main, norm


def _rwkv_kernel(x_ref, xn_ref, g_ref, wm_ref, wl_ref, mum_ref, mul_ref, wqkv_ref, vec_ref, w2_ref, a2_ref, g2_ref,
                 y_ref, qkv_ref, p_scr, n_scr, last_ref, st_ref):
    tm = x_ref.shape[0]
    n_chunks = tm // CHUNK
    t = pl.program_id(1)
    slot = lax.rem(t, 2)
    row = lax.broadcasted_iota(jnp.int32, (tm, RW_TN), 0)

    def norm_x(src_ref, dst_slot):
        n_scr[dst_slot] = _rmsnorm_f32(src_ref[...], g_ref[...]).astype(BF16)
        yield

    def rw_proj(src_slot, dst_slot):
        n = n_scr[src_slot]
        blocks = [(wm_ref, mum_ref, j, j) for j in range(wm_ref.shape[1] // RW_TN)]
        blocks += [(wl_ref, mul_ref, j, wm_ref.shape[1] // RW_TN + j) for j in range(wl_ref.shape[1] // RW_TN)]
        for w_ref, mu_ref, j, jd in blocks:
            cs = slice(j * RW_TN, (j + 1) * RW_TN)
            cd = slice(jd * RW_TN, (jd + 1) * RW_TN)
            p = jnp.dot(n, w_ref[:, cs], preferred_element_type=F32)
            prev = jnp.where(row == 0, last_ref[:, cd], pltpu.roll(p, 1, 0))
            last_ref[:, cd] = p[tm - 1:tm, :]
            p_scr[dst_slot, :, cd] = p + mu_ref[:, cs] * (prev - p)
            yield

    def qkv_proj():
        n = n_scr[slot]
        per = ATTN_GROUP_SLABS
        for j in range(wqkv_ref.shape[1] // ATTN_GROUP_DIM):
            res = jnp.dot(n, wqkv_ref[:, j * ATTN_GROUP_DIM:(j + 1) * ATTN_GROUP_DIM], preferred_element_type=F32)
            for s in range(per):
                qkv_ref[j * per + s] = res[:, s * LANES:(s + 1) * LANES]
            yield

    @pl.when(t == 0)
    def _init():
        st_ref[...] = jnp.zeros_like(st_ref)
        last_ref[...] = jnp.zeros_like(last_ref)
        _interleave(norm_x(x_ref, 0))
        _interleave(rw_proj(0, 0))

    def background():
        yield from qkv_proj()
        yield from norm_x(xn_ref, 1 - slot)
        yield from rw_proj(1 - slot, 1 - slot)

    prep, main, norm = _rwkv_phases(vec_ref, w2_ref, a2_ref, g2_ref, st_ref)
    rows = [pl.ds(u * CHUNK, CHUNK) for u in range(n_chunks)]
    ops = [{} for _ in range(n_chunks)]
    prep_u = lambda u: prep(p_scr.at[slot, rows[u]], ops[u])
    norm_u = lambda u: norm(ops[u], y_ref.at[rows[u]])
    bg = background()
    for first in range(0, n_chunks, RW_GROUP):
        group = range(first, first + RW_GROUP)
        _interleave(*(prep_u(u) for u in group), background=_every(bg, RW_BG_EVERY[0]))
        _interleave(*(_delayed(main(ops[u]), 2 * (u - first)) for u in group),
                    background=_every(bg, RW_BG_EVERY[1]))
        _interleave(*(norm_u(u) for u in group), background=_every(bg, RW_BG_EVERY[2]))
    _interleave(bg)


def _rwkv(x, g, w_main, w_lora, mu_main, mu_lora, w_qkv, vecs, w2p, a2p, g2p, batch, seq):
    m, d = x.shape
    nt = seq // RW_TM
    slabs = w_qkv.shape[1] // LANES
    assert w_main.shape[1] + w_lora.shape[1] == RW_WIDTH
    full = lambda a: pl.BlockSpec(a.shape, lambda b, t: (0, 0), pipeline_mode=pl.Buffered(1))
    return pl.pallas_call(
        _rwkv_kernel,
        out_shape=(jax.ShapeDtypeStruct((m, RWKV_DIM), BF16), jax.ShapeDtypeStruct((slabs, m, LANES), F32)),
        grid=(batch, nt),
        in_specs=[pl.BlockSpec((RW_TM, d), lambda b, t: (b * nt + t, 0)),
                  pl.BlockSpec((RW_TM, d), lambda b, t: (b * nt + jnp.minimum(t + 1, nt - 1), 0)),
                  full(g), full(w_main), full(w_lora), full(mu_main), full(mu_lora), full(w_qkv), full(vecs),
                  full(w2p), full(a2p), full(g2p)],
        out_specs=(pl.BlockSpec((RW_TM, RWKV_DIM), lambda b, t: (b * nt + t, 0)),
                   pl.BlockSpec((slabs, RW_TM, LANES), lambda b, t: (0, b * nt + t, 0))),
        scratch_shapes=[pltpu.VMEM((2, RW_TM, RW_WIDTH), F32), pltpu.VMEM((2, RW_TM, d), BF16),
                        pltpu.VMEM((1, RW_WIDTH), F32), pltpu.VMEM((N_PAIRS, HEAD_DIM, PAIR), F32)],
        compiler_params=_params("parallel", "arbitrary"),
        name="rwkv7_proj_chunk_scan",
    )(x, x, g, w_main, w_lora, mu_main, mu_lora, w_qkv, vecs, w2p, a2p, g2p)


def _dil_attn_kernel(q_ref, kp_ref, k_ref, vp_ref, v_ref, o_ref, l_ref, *, window, dilation, slopes):
    blk = ATTN_BLK
    width = ATTN_GROUP_DIM
    n_win = q_ref.shape[1] // window
    lane_head = lax.broadcasted_iota(jnp.int32, (blk, width), 1) >> 6
    shape = (HEADS_PER_GROUP * blk, 2 * blk)
    rowi = lax.broadcasted_iota(jnp.int32, shape, 0)
    ki = lax.broadcasted_iota(jnp.int32, shape, 1)
    steps = (rowi & (blk - 1)) + blk - ki
    in_band = (steps >= 0) & (steps <= blk)
    slope = jnp.full(shape, slopes[0], F32)
    for h in range(1, HEADS_PER_GROUP):
        slope = jnp.where(rowi >= h * blk, slopes[h], slope)
    alibi = (-LOG2E) * slope * (steps * dilation).astype(F32)
    bias = jnp.where(in_band, alibi, -jnp.inf)
    first_key = jnp.where(pl.program_id(1) == 0, blk, 0)
    bias_w0 = jnp.where(in_band & (ki >= first_key), alibi, -jnp.inf)

    def load(ref, rows):
        return jnp.concatenate([ref[s, rows, :] for s in range(ATTN_GROUP_SLABS)], axis=1)

    def unit(j, r):
        rows = pl.ds(j * window + r, blk, stride=dilation)
        q = (load(q_ref, rows) * (LOG2E * HEAD_DIM ** -0.5)).astype(BF16)
        if j == 0:
            prev = pl.ds(r, blk, stride=dilation)
            k = jnp.concatenate([load(kp_ref, prev), load(k_ref, rows)], axis=0).astype(BF16)
            v = jnp.concatenate([load(vp_ref, prev), load(v_ref, rows)], axis=0).astype(BF16)
            b = bias_w0
        else:
            both = pl.ds((j - 1) * window + r, 2 * blk, stride=dilation)
            k = load(k_ref, both).astype(BF16)
            v = load(v_ref, both).astype(BF16)
            b = bias
        yield
        qs = jnp.concatenate([jnp.where(lane_head == h, q, jnp.zeros_like(q)) for h in range(HEADS_PER_GROUP)],
                             axis=0)
        s = lax.dot_general(qs, k, (((1,), (1,)), ((), ())), preferred_element_type=F32) + b
        yield
        mx = jnp.max(s, axis=-1, keepdims=True)
        yield
        e = lax.exp2(s - mx)
        yield
        den = jnp.sum(e, axis=-1, keepdims=True)
        lse = mx * (1.0 / LOG2E) + jnp.log(den)
        yield
        prob = (e / den).astype(BF16)
        yield
        o = jnp.dot(prob[:blk], v, preferred_element_type=F32)
        lb = jnp.broadcast_to(lse[:blk], (blk, width))
        for h in range(1, HEADS_PER_GROUP):
            oh = jnp.dot(prob[h * blk:(h + 1) * blk], v, preferred_element_type=F32)
            o = jnp.where(lane_head == h, oh, o)
            lb = jnp.where(lane_head == h, lse[h * blk:(h + 1) * blk], lb)
        yield
        for s_ in range(ATTN_GROUP_SLABS):
            o_ref[s_, rows, :] = o[:, s_ * LANES:(s_ + 1) * LANES]
            l_ref[s_, rows, :] = lb[:, s_ * LANES:(s_ + 1) * LANES]

    def residues(i, carry):
        units = [(j, i * per_iter + dr) for dr in range(per_iter) for j in range(n_win)]
        for a in range(0, len(units), ATTN_UNROLL):
            _interleave(*(unit(j, r) for j, r in units[a:a + ATTN_UNROLL]))
        return carry

    per_iter = max(min(dilation, ATTN_UNROLL // n_win), 1)
    if dilation == per_iter:
        residues(0, 0)
    else:
        lax.fori_loop(0, dilation // per_iter, residues, 0)


def _alibi_slopes(n_heads):
    return [2.0 ** (-8.0 * (h + 1.0) / n_heads) for h in range(n_heads)]


def _dil_attention(qkv, group, batch, seq):
    window, dilation = DIL_GROUPS[group]
    assert window // dilation == ATTN_BLK and seq % window == 0
    m = qkv.shape[1]
    tile = max(window, ATTN_TILE)
    assert seq % tile == 0
    n_win = tile // window
    ntile = seq // tile
    slopes = tuple(_alibi_slopes(ATTN_HEADS)[group * HEADS_PER_GROUP:(group + 1) * HEADS_PER_GROUP])
    groups = ATTN_DIM // ATTN_GROUP_DIM
    cur = lambda col: pl.BlockSpec((ATTN_GROUP_SLABS, tile, LANES), lambda b, i: (col, b * ntile + i, 0))
    prev = lambda col: pl.BlockSpec((ATTN_GROUP_SLABS, window, LANES),
                                    lambda b, i: (col, jnp.maximum((b * ntile + i) * n_win - 1, 0), 0))
    out_sds = jax.ShapeDtypeStruct((ATTN_GROUP_SLABS, m, LANES), F32)
    out_spec = pl.BlockSpec((ATTN_GROUP_SLABS, tile, LANES), lambda b, i: (0, b * ntile + i, 0))
    return pl.pallas_call(
        functools.partial(_dil_attn_kernel, window=window, dilation=dilation, slopes=slopes),
        out_shape=(out_sds, out_sds),
        grid=(batch, ntile),
        in_specs=[cur(group), prev(groups + group), cur(groups + group),
                  prev(2 * groups + group), cur(2 * groups + group)],
        out_specs=(out_spec, out_spec),
        compiler_params=_params("parallel", "arbitrary"),
        name=f"dilated_attn_g{group}",
    )(qkv, qkv, qkv, qkv, qkv)


def _merge_kernel(y_ref, o0_ref, o1_ref, o2_ref, l0_ref, l1_ref, l2_ref, x_ref,
                  g_ref, wg_ref, pr_ref, pa_ref, wo_ref, h_ref):
    d = x_ref.shape[1]
    rows_per = x_ref.shape[0] // MERGE_SPLIT

    def group(r):
        rows = pl.ds(r * rows_per, rows_per)
        wide = lambda ref: jnp.concatenate([ref[s, rows, :] for s in range(ATTN_GROUP_SLABS)], axis=1)
        x = x_ref[rows, :]
        gates = _dot(_rmsnorm_f32(x, g_ref[...]), wg_ref[...])
        yield
        t_rwkv = jnp.dot(y_ref[rows, :], pr_ref[...], preferred_element_type=F32)
        yield
        l0, l1, l2 = wide(l0_ref), wide(l1_ref), wide(l2_ref)
        mx = jnp.maximum(jnp.maximum(l0, l1), l2)
        e0, e1, e2 = jnp.exp(l0 - mx), jnp.exp(l1 - mx), jnp.exp(l2 - mx)
        y_attn = (e0 * wide(o0_ref) + e1 * wide(o1_ref) + e2 * wide(o2_ref)) / (e0 + e1 + e2)
        t_attn = _dot(y_attn, pa_ref[...])
        yield
        merged = jax.nn.sigmoid(gates[:, :d]) * t_rwkv
        yield
        merged = merged + jax.nn.sigmoid(gates[:, d:]) * t_attn
        yield
        h_ref[rows, :] = x + _dot(merged, wo_ref[...])

    _interleave(*(_delayed(group(r), 2 * r) for r in range(MERGE_SPLIT)))


def _merge(y_rwkv, attn, x, g, wg, p_rwkv, p_attn, w_out, tm):
    m, d = x.shape
    row = lambda w: pl.BlockSpec((tm, w), lambda i: (i, 0))
    slab = pl.BlockSpec((ATTN_GROUP_SLABS, tm, LANES), lambda i: (0, i, 0))
    full = lambda a: pl.BlockSpec(a.shape, lambda i: (0, 0), pipeline_mode=pl.Buffered(1))
    (o0, l0), (o1, l1), (o2, l2) = attn
    return pl.pallas_call(
        _merge_kernel,
        out_shape=jax.ShapeDtypeStruct((m, d), F32),
        grid=(m // tm,),
        in_specs=[row(d)] + [slab] * 6 + [row(d), full(g), full(wg), full(p_rwkv), full(p_attn), full(w_out)],
        out_specs=row(d),
        compiler_params=_params("parallel"),
        name="gated_merge",
    )(y_rwkv, o0, o1, o2, l0, l1, l2, x, g, wg, p_rwkv, p_attn, w_out)


def _mem_kv_kernel(mem_ref, g_ref, w_ref, o_ref):
    memn = _rmsnorm_f32(mem_ref[...], g_ref[...])
    o_ref[...] = _dot(memn, w_ref[...]).astype(o_ref.dtype)


def _mem_kv(mem2d, g, w_kv, mem_len):
    m, d = mem2d.shape
    n = w_kv.shape[1]
    return pl.pallas_call(
        _mem_kv_kernel,
        out_shape=jax.ShapeDtypeStruct((m, n), BF16),
        grid=(m // mem_len,),
        in_specs=[pl.BlockSpec((mem_len, d), lambda i: (i, 0)), pl.BlockSpec((1, d), lambda i: (0, 0)),
                  pl.BlockSpec((d, n), lambda i: (0, 0))],
        out_specs=pl.BlockSpec((mem_len, n), lambda i: (i, 0)),
        compiler_params=_params("parallel"),
        name="mem_kv_proj",
    )(mem2d, g.reshape(1, d), w_kv)


def _xattn_kernel(h_ref, g_ref, wq_ref, kv_ref, wo_ref, gf_ref, o_ref, xn_ref):
    d = h_ref.shape[1]
    hd = d // XATTN_HEADS
    heads = range(XATTN_HEADS)
    cs = lambda hh: slice(hh * hd, (hh + 1) * hd)
    rows_per = h_ref.shape[0] // XATTN_SPLIT

    def group(r):
        rows = pl.ds(r * rows_per, rows_per)
        h = h_ref[rows, :]
        q = (_dot(_rmsnorm_f32(h, g_ref[...]), wq_ref[...]) * (LOG2E * hd ** -0.5)).astype(BF16)
        yield
        s = [_dot_nt(q[:, cs(hh)], kv_ref[:, cs(hh)]) for hh in heads]
        yield
        e = [lax.exp2(s[hh] - jnp.max(s[hh], axis=-1, keepdims=True)) for hh in heads]
        yield
        prob = [e[hh] / jnp.sum(e[hh], axis=-1, keepdims=True) for hh in heads]
        yield
        outs = [_dot(prob[hh], kv_ref[:, d + hh * hd:d + (hh + 1) * hd]) for hh in heads]
        yield
        h2 = h + _dot(jnp.concatenate(outs, axis=-1), wo_ref[...])
        o_ref[rows, :] = h2
        yield
        xn_ref[rows, :] = _rmsnorm_f32(h2, gf_ref[...]).astype(xn_ref.dtype)

    _interleave(*(_delayed(group(r), 2 * r) for r in range(XATTN_SPLIT)))


def _xattn(h, g, wq, kv, wo, g_ffn, seq, mem_len, tm):
    m, d = h.shape
    tiles_per_seq = seq // tm
    full = lambda a: pl.BlockSpec(a.shape, lambda i: (0, 0), pipeline_mode=pl.Buffered(1))
    row = pl.BlockSpec((tm, d), lambda i: (i, 0))
    return pl.pallas_call(
        _xattn_kernel,
        out_shape=(jax.ShapeDtypeStruct((m, d), F32), jax.ShapeDtypeStruct((m, d), BF16)),
        grid=(m // tm,),
        in_specs=[row, full(g), full(wq), pl.BlockSpec((mem_len, 2 * d), lambda i: (i // tiles_per_seq, 0)),
                  full(wo), full(g_ffn)],
        out_specs=(row, row),
        compiler_params=_params("parallel"),
        name="mem_cross_attn",
    )(h, g, wq, kv, wo, g_ffn)


def _ffn_kernel(h_ref, xn_ref, w1_ref, w2_ref, gf_ref, o_ref):
    tm = h_ref.shape[0]
    f = w1_ref.shape[1]
    rows_per = tm // FFN_SPLIT

    def group(r):
        rows = pl.ds(r * rows_per, rows_per)
        xn = xn_ref[rows, :]
        acc = None
        for c in range(f // FFN_TF):
            cs = slice(c * FFN_TF, (c + 1) * FFN_TF)
            u = jnp.dot(xn, w1_ref[:, cs], preferred_element_type=F32)
            act = jnp.square(jnp.maximum(u, 0.0)).astype(BF16)
            part = jnp.dot(act, w2_ref[cs, :], preferred_element_type=F32)
            acc = part if acc is None else acc + part
            yield
        o_ref[rows, :] = _rmsnorm_f32(h_ref[rows, :] + acc, gf_ref[...])

    _interleave(*(_delayed(group(r), r) for r in range(FFN_SPLIT)))


def _ffn(h, xn, w1, w2, g_final, tm):
    m, d = h.shape
    full = lambda a: pl.BlockSpec(a.shape, lambda i: (0, 0), pipeline_mode=pl.Buffered(1))
    row = pl.BlockSpec((tm, d), lambda i: (i, 0))
    return pl.pallas_call(
        _ffn_kernel,
        out_shape=jax.ShapeDtypeStruct((m, d), F32),
        grid=(m // tm,),
        in_specs=[row, row, full(w1), full(w2), full(g_final)],
        out_specs=row,
        compiler_params=_params("parallel"),
        name="relu2_mlp_final_norm",
    )(h, xn, w1, w2, g_final)


def _pad_cols(a, width):
    return jnp.pad(a, ((0, 0), (0, width - a.shape[1])))


def _pad_rows(a, rows):
    return jnp.pad(a, ((0, rows - a.shape[0]), (0, 0)))


def kernel(x, mem, norm_mix_g, w_in, shift_mu, w0, w2, a0, a2, g2, k_k, k_a, r_k, gn_w, gn_b, p_rwkv, p_attn, w_out, norm_x_g, norm_mem_g, xa_wq, xa_wkv, xa_wo, norm_ffn_g, ffn_w1, ffn_w2, norm_final_g):
    batch, seq, d = x.shape
    mem_len = mem.shape[1]
    assert w_in.shape[0] == 1 and d == RWKV_DIM and seq % RW_TM == 0
    m = batch * seq
    x2 = x.reshape(m, d)

    w = w_in[0]
    c_wd, c_ad, c_gd = 3 * RWKV_DIM, 3 * RWKV_DIM + DECAY_LORA, 3 * RWKV_DIM + DECAY_LORA + AAA_LORA
    c_q = c_gd + GATE_LORA
    c_gate = c_q + 3 * ATTN_DIM
    lora_cols = lambda a: jnp.concatenate(
        [_pad_cols(a[:, c_wd:c_ad], 128), _pad_cols(a[:, c_ad:c_gd], 128), _pad_cols(a[:, c_gd:c_q], 256)], axis=1)
    mu = shift_mu[0].reshape(1, -1).astype(F32)
    w_main, w_lora = w[:, :c_wd].astype(BF16), lora_cols(w).astype(BF16)
    mu_main, mu_lora = mu[:, :c_wd], lora_cols(mu)
    w_qkv = w[:, c_q:c_gate].astype(BF16)
    w_gate = w[:, c_gate:].astype(BF16)
    vecs = jnp.stack([w0[0], a0[0], k_k[0], k_a[0], r_k[0].reshape(-1), gn_w[0], gn_b[0],
                      jnp.zeros((RWKV_DIM,), F32)]).astype(F32)
    w2p = _pad_rows(w2[0], 128).astype(BF16)
    a2p = _pad_rows(a2[0], 128).astype(BF16)
    g2p = _pad_rows(g2[0], 256).astype(BF16)

    g_mix = norm_mix_g[0].reshape(1, d)
    y_rwkv, qkv = _rwkv(x2, g_mix, w_main, w_lora, mu_main, mu_lora, w_qkv, vecs, w2p, a2p, g2p, batch, seq)
    attn = [_dil_attention(qkv, g, batch, seq) for g in range(len(DIL_GROUPS))]
    h1 = _merge(y_rwkv, attn, x2, g_mix, w_gate, p_rwkv[0].astype(BF16), p_attn[0].astype(BF16),
                w_out[0].astype(BF16), MERGE_TM)
    kv = _mem_kv(mem.reshape(batch * mem_len, d), norm_mem_g[0], xa_wkv[0].astype(BF16), mem_len)
    h2, xn2 = _xattn(h1, norm_x_g[0].reshape(1, d), xa_wq[0].astype(BF16), kv, xa_wo[0].astype(BF16),
                     norm_ffn_g[0].reshape(1, d), seq, mem_len, XATTN_TM)
    out = _ffn(h2, xn2, ffn_w1[0].astype(BF16), ffn_w2[0].astype(BF16), norm_final_g.reshape(1, d), FFN_TM)
    return out.reshape(batch, seq, d)
```

```python
import functools
import math

import jax
import jax.numpy as jnp
from jax import lax
from jax.experimental import pallas as pl
from jax.experimental.pallas import tpu as pltpu

F32 = jnp.float32
BF16 = jnp.bfloat16

LANES = 128
HEAD_DIM = 64
PAIR = 2 * HEAD_DIM
CHUNK = 64
RWKV_DIM = 1024
N_PAIRS = RWKV_DIM // PAIR
DECAY_LORA = 64
AAA_LORA = 64
GATE_LORA = 160
LORA_WD_OFF = 3 * RWKV_DIM
LORA_AD_OFF = LORA_WD_OFF + 128
LORA_GD_OFF = LORA_AD_OFF + 128
RW_WIDTH = LORA_GD_OFF + 256
RW_TM = 256
RW_GROUP = 4
RW_TN = 256
RW_BG_EVERY = (1, 5, 1)
GN_EPS = HEAD_DIM * 1e-5
LOG2E = 1.0 / math.log(2.0)
DECAY_SCALE = math.exp(-0.5)
NORM_EPS = 1e-6
DIL_GROUPS = ((128, 1), (512, 4), (2048, 16))
HEADS_PER_GROUP = 4
ATTN_HEADS = 12
ATTN_GROUP_DIM = HEADS_PER_GROUP * HEAD_DIM
ATTN_GROUP_SLABS = ATTN_GROUP_DIM // LANES
ATTN_DIM = ATTN_HEADS * HEAD_DIM
ATTN_BLK = 128
ATTN_TILE = 2048
ATTN_UNROLL = 4
XATTN_HEADS = 4
MERGE_TM = 1024
MERGE_SPLIT = 2
XATTN_TM = 1024
XATTN_SPLIT = 2
FFN_TM = 1024
FFN_SPLIT = 2
FFN_TF = 1024
VMEM_LIMIT = 56 * 1024 * 1024


def _dot(a, b):
    return jnp.dot(a.astype(BF16), b.astype(BF16), preferred_element_type=F32)


def _dot_nt(a, b):
    return lax.dot_general(a.astype(BF16), b.astype(BF16), (((1,), (1,)), ((), ())),
                           preferred_element_type=F32)


def _params(*sem):
    return pltpu.CompilerParams(dimension_semantics=sem, vmem_limit_bytes=VMEM_LIMIT)


def _rmsnorm_f32(x, g):
    return x * lax.rsqrt(jnp.mean(x * x, axis=-1, keepdims=True) + NORM_EPS) * g


VEC_W0, VEC_A0, VEC_KK, VEC_KA, VEC_RK, VEC_GNW, VEC_GNB = range(7)


def _interleave(*gens, background=None):
    live = list(gens)
    while live:
        for g in list(live):
            try:
                next(g)
            except StopIteration:
                live.remove(g)
        if background is not None:
            next(background, None)


def _every(gen, n):
    for _ in gen:
        for _ in range(n):
            yield


def _delayed(gen, steps):
    for _ in range(steps):
        yield
    yield from gen


def _rwkv_phases(vec_ref, w2_ref, a2_ref, g2_ref, st_ref):
    L = CHUNK
    pairs = range(N_PAIRS)
    cat = jnp.concatenate

    head_a = lax.broadcasted_iota(jnp.int32, (L, PAIR), 1) < HEAD_DIM
    rowh = lax.broadcasted_iota(jnp.int32, (L, PAIR), 0)
    laneh = lax.broadcasted_iota(jnp.int32, (L, PAIR), 1)
    eye_pk = jnp.where((laneh == rowh) | (laneh == rowh + HEAD_DIM), 1.0, 0.0)
    rows_sc = lax.broadcasted_iota(jnp.int32, (2 * L, 2 * PAIR), 0)
    src_sc = lax.broadcasted_iota(jnp.int32, (2 * L, 2 * PAIR), 1) & (L - 1)
    tri_mask = ((rows_sc < L) & (rows_sc > src_sc)) | ((rows_sc >= L) & (rows_sc - L >= src_sc))
    r3 = lax.broadcasted_iota(jnp.int32, (L, 3 * L), 0)
    c3 = lax.broadcasted_iota(jnp.int32, (L, 3 * L), 1)
    c3 = c3 - jnp.where(c3 >= L, L, 0) - jnp.where(c3 >= 2 * L, L, 0)
    tri3 = jnp.where(r3 >= c3, 1.0, 0.0).astype(BF16)
    zb = jnp.zeros((L, PAIR), BF16)
    vec = lambda i: vec_ref[i:i + 1, :]

    def hsum(x):
        sa = jnp.sum(jnp.where(head_a, x, 0.0), axis=-1, keepdims=True)
        sb = jnp.sum(jnp.where(head_a, 0.0, x), axis=-1, keepdims=True)
        return jnp.where(head_a, sa, sb)

    def sl(x, p):
        return x[:, p * PAIR:(p + 1) * PAIR]

    m_a = lambda x: jnp.where(head_a, x, jnp.zeros_like(x))
    m_b = lambda x: jnp.where(head_a, jnp.zeros_like(x), x)

    def heads_rows(*cols):
        return cat([cat([m_a(x) for x in cols], axis=1), cat([m_b(x) for x in cols], axis=1)], axis=0)

    def prep(p_ref, ops):
        pr = p_ref[:, 0:RWKV_DIM]
        pk = p_ref[:, RWKV_DIM:2 * RWKV_DIM]
        pv = p_ref[:, 2 * RWKV_DIM:3 * RWKV_DIM]
        wd = jnp.tanh(p_ref[:, LORA_WD_OFF:LORA_WD_OFF + 128])
        lw = -DECAY_SCALE * jax.nn.sigmoid(vec(VEC_W0) + _dot(wd, w2_ref[...]))
        yield
        a_lr = jax.nn.sigmoid(vec(VEC_A0) + _dot(p_ref[:, LORA_AD_OFF:LORA_AD_OFF + 128], a2_ref[...]))
        yield
        ops["gate"] = _dot(jax.nn.sigmoid(p_ref[:, LORA_GD_OFF:LORA_GD_OFF + 256]), g2_ref[...])
        yield
        kk = pk * vec(VEC_KK)
        kbar = pk * (1.0 + (a_lr - 1.0) * vec(VEC_KA))
        yield
        h1 = lw.astype(BF16)
        r1 = lw - h1.astype(F32)
        h2 = r1.astype(BF16)
        h3 = (r1 - h2.astype(F32)).astype(BF16)
        c = jnp.dot(tri3, cat([h1, h2, h3], axis=0), preferred_element_type=F32)
        yield
        cl = c[L - 1:L, :]
        einv = jnp.exp(-c)
        yield
        edec = jnp.exp(cl - c)
        yield
        rt = pr * jnp.exp(c)
        yield
        eprev = jnp.exp(c - lw)
        ops["gl"] = jnp.exp(cl)
        ops["rkr"] = pr * kbar * vec(VEC_RK)
        ops["rt"] = rt
        ops["pv"] = pv
        yield
        ss = [hsum(sl(kk, p) * sl(kk, p)) for p in pairs]
        yield
        kkn = [sl(kk, p) * lax.rsqrt(jnp.maximum(ss[p], 1e-24)) for p in pairs]
        bvec = [kkn[p] * sl(a_lr, p) for p in pairs]
        yield
        ops["at"] = [(-kkn[p] * sl(eprev, p)).astype(BF16) for p in pairs]
        ops["rtp"] = [sl(rt, p).astype(BF16) for p in pairs]
        yield
        ops["bt"] = [(bvec[p] * sl(einv, p)).astype(BF16) for p in pairs]
        ops["kt"] = [(sl(kbar, p) * sl(einv, p)).astype(BF16) for p in pairs]
        yield
        ops["bh"] = [bvec[p] * sl(edec, p) for p in pairs]
        ops["kh"] = [sl(kbar, p) * sl(edec, p) for p in pairs]
        ops["vp"] = [sl(pv, p).astype(BF16) for p in pairs]

    def main(ops):
        at, rtp, bt, kt, bh, kh, vp = (ops[k] for k in ("at", "rtp", "bt", "kt", "bh", "kh", "vp"))
        sc = [_dot_nt(cat([at[p], rtp[p]], axis=0), cat([m_a(bt[p]), m_b(kt[p]), m_a(kt[p]), m_b(bt[p])], axis=0))
              for p in pairs]
        yield
        sc = [jnp.where(tri_mask, sc[p], 0.0) for p in pairs]
        npk = [jnp.where(head_a, sc[p][:L, :PAIR], sc[p][:L, PAIR:]) for p in pairs]
        aak = [jnp.where(head_a, sc[p][:L, PAIR:], sc[p][:L, :PAIR]) for p in pairs]
        bot_sc = [sc[p][L:] for p in pairs]
        yield
        aakv = [_dot(aak[p], heads_rows(vp[p])).astype(BF16) for p in pairs]
        yield
        tpk = [eye_pk for p in pairs]
        for _ in range(6):
            res = [_dot(npk[p], heads_rows(npk[p].astype(BF16), tpk[p].astype(BF16))) for p in pairs]
            npk = [res[p][:, :PAIR] for p in pairs]
            tpk = [tpk[p] + res[p][:, PAIR:] for p in pairs]
            yield
        pq = [_dot(tpk[p], heads_rows(at[p], aakv[p])) for p in pairs]
        yield
        pb = [pq[p][:, :PAIR].astype(BF16) for p in pairs]
        qb = [pq[p][:, PAIR:].astype(BF16) for p in pairs]
        rhs = [cat([cat([m_a(pb[p]), m_a(qb[p])], axis=1), cat([zb, m_b(vp[p])], axis=1),
                    cat([zb, m_a(vp[p])], axis=1), cat([m_b(pb[p]), m_b(qb[p])], axis=1)], axis=0) for p in pairs]
        zt1 = [cat([bh[p], kh[p]], axis=0).T for p in pairs]
        zt2 = [cat([kh[p], bh[p]], axis=0).T for p in pairs]
        yield
        lhs_st = [cat([jnp.where(head_a, zt1[p][:L], zt1[p][L:]), jnp.where(head_a, zt2[p][:L], zt2[p][L:])],
                      axis=1) for p in pairs]
        tb = [_dot(cat([bot_sc[p], lhs_st[p]], axis=0), rhs[p]) for p in pairs]
        yield
        st = [st_ref[p].astype(BF16) for p in pairs]
        ys = [_dot(cat([sl(ops["rt"], p) + tb[p][:L, :PAIR],
                        jnp.where(eye_pk > 0.0, sl(ops["gl"], p), 0.0) + tb[p][L:, :PAIR]], axis=0),
                   heads_rows(st[p])) for p in pairs]
        yield
        for p in pairs:
            st_ref[p] = ys[p][L:] + tb[p][L:, PAIR:]
        ops["y"] = [ys[p][:L] + tb[p][:L, PAIR:] for p in pairs]

    def norm(ops, y_ref):
        y = ops["y"]
        mean = [hsum(y[p]) * (1.0 / HEAD_DIM) for p in pairs]
        yield
        dev = [y[p] - mean[p] for p in pairs]
        var = [hsum(dev[p] * dev[p]) * (1.0 / HEAD_DIM) for p in pairs]
        yield
        bonus = [hsum(sl(ops["rkr"], p)) * sl(ops["pv"], p) for p in pairs]
        yield
        for p in pairs:
            cs = slice(p * PAIR, (p + 1) * PAIR)
            yn = (dev[p] * lax.rsqrt(var[p] + GN_EPS) * vec_ref[VEC_GNW:VEC_GNW + 1, cs]
                  + vec_ref[VEC_GNB:VEC_GNB + 1, cs])
            y_ref[:, cs] = ((yn + bonus[p]) * sl(ops["gate"], p)).astype(y_ref.dtype)
            if p % 2 == 1:
                yield

    return prep, main, norm


def _rwkv_kernel(x_ref, xn_ref, g_ref, wm_ref, wl_ref, mum_ref, mul_ref, wqkv_ref, vec_ref, w2_ref, a2_ref, g2_ref,
                 y_ref, qkv_ref, p_scr, n_scr, last_ref, st_ref):
    tm = x_ref.shape[0]
    n_chunks = tm // CHUNK
    t = pl.program_id(1)
    slot = lax.rem(t, 2)
    row = lax.broadcasted_iota(jnp.int32, (tm, RW_TN), 0)

    def norm_x(src_ref, dst_slot):
        n_scr[dst_slot] = _rmsnorm_f32(src_ref[...], g_ref[...]).astype(BF16)
        yield

    def rw_proj(src_slot, dst_slot):
        n = n_scr[src_slot]
        blocks = [(wm_ref, mum_ref, j, j) for j in range(wm_ref.shape[1] // RW_TN)]
        blocks += [(wl_ref, mul_ref, j, wm_ref.shape[1] // RW_TN + j) for j in range(wl_ref.shape[1] // RW_TN)]
        for w_ref, mu_ref, j, jd in blocks:
            cs = slice(j * RW_TN, (j + 1) * RW_TN)
            cd = slice(jd * RW_TN, (jd + 1) * RW_TN)
            p = jnp.dot(n, w_ref[:, cs], preferred_element_type=F32)
            prev = jnp.where(row == 0, last_ref[:, cd], pltpu.roll(p, 1, 0))
            last_ref[:, cd] = p[tm - 1:tm, :]
            p_scr[dst_slot, :, cd] = p + mu_ref[:, cs] * (prev - p)
            yield

    def qkv_proj():
        n = n_scr[slot]
        per = ATTN_GROUP_SLABS
        for j in range(wqkv_ref.shape[1] // ATTN_GROUP_DIM):
            res = jnp.dot(n, wqkv_ref[:, j * ATTN_GROUP_DIM:(j + 1) * ATTN_GROUP_DIM], preferred_element_type=F32)
            for s in range(per):
                qkv_ref[j * per + s] = res[:, s * LANES:(s + 1) * LANES]
            yield

    @pl.when(t == 0)
    def _init():
        st_ref[...] = jnp.zeros_like(st_ref)
        last_ref[...] = jnp.zeros_like(last_ref)
        _interleave(norm_x(x_ref, 0))
        _interleave(rw_proj(0, 0))

    def background():
        yield from qkv_proj()
        yield from norm_x(xn_ref, 1 - slot)
        yield from rw_proj(1 - slot, 1 - slot)

    prep, main, norm = _rwkv_phases(vec_ref, w2_ref, a2_ref, g2_ref, st_ref)
    rows = [pl.ds(u * CHUNK, CHUNK) for u in range(n_chunks)]
    ops = [{} for _ in range(n_chunks)]
    prep_u = lambda u: prep(p_scr.at[slot, rows[u]], ops[u])
    norm_u = lambda u: norm(ops[u], y_ref.at[rows[u]])
    bg = background()
    for first in range(0, n_chunks, RW_GROUP):
        group = range(first, first + RW_GROUP)
        _interleave(*(prep_u(u) for u in group), background=_every(bg, RW_BG_EVERY[0]))
        _interleave(*(_delayed(main(ops[u]), 2 * (u - first)) for u in group),
                    background=_every(bg, RW_BG_EVERY[1]))
        _interleave(*(norm_u(u) for u in group), background=_every(bg, RW_BG_EVERY[2]))
    _interleave(bg)


def _rwkv(x, g, w_main, w_lora, mu_main, mu_lora, w_qkv, vecs, w2p, a2p, g2p, batch, seq):
    m, d = x.shape
    nt = seq // RW_TM
    slabs = w_qkv.shape[1] // LANES
    assert w_main.shape[1] + w_lora.shape[1] == RW_WIDTH
    full = lambda a: pl.BlockSpec(a.shape, lambda b, t: (0, 0), pipeline_mode=pl.Buffered(1))
    return pl.pallas_call(
        _rwkv_kernel,
        out_shape=(jax.ShapeDtypeStruct((m, RWKV_DIM), BF16), jax.ShapeDtypeStruct((slabs, m, LANES), F32)),
        grid=(batch, nt),
        in_specs=[pl.BlockSpec((RW_TM, d), lambda b, t: (b * nt + t, 0)),
                  pl.BlockSpec((RW_TM, d), lambda b, t: (b * nt + jnp.minimum(t + 1, nt - 1), 0)),
                  full(g), full(w_main), full(w_lora), full(mu_main), full(mu_lora), full(w_qkv), full(vecs),
                  full(w2p), full(a2p), full(g2p)],
        out_specs=(pl.BlockSpec((RW_TM, RWKV_DIM), lambda b, t: (b * nt + t, 0)),
                   pl.BlockSpec((slabs, RW_TM, LANES), lambda b, t: (0, b * nt + t, 0))),
        scratch_shapes=[pltpu.VMEM((2, RW_TM, RW_WIDTH), F32), pltpu.VMEM((2, RW_TM, d), BF16),
                        pltpu.VMEM((1, RW_WIDTH), F32), pltpu.VMEM((N_PAIRS, HEAD_DIM, PAIR), F32)],
        compiler_params=_params("parallel", "arbitrary"),
        name="rwkv7_proj_chunk_scan",
    )(x, x, g, w_main, w_lora, mu_main, mu_lora, w_qkv, vecs, w2p, a2p, g2p)


def _dil_attn_kernel(q_ref, kp_ref, k_ref, vp_ref, v_ref, *rest, window, dilation, slopes):
    n_casts = (len(rest) - 2) // 2
    o_ref, l_ref = rest[n_casts], rest[n_casts + 1]
    for src, dst in zip(rest[:n_casts], rest[n_casts + 2:]):
        dst[...] = src[...].astype(dst.dtype)
    blk = ATTN_BLK
    width = ATTN_GROUP_DIM
    n_win = q_ref.shape[1] // window
    lane_head = lax.broadcasted_iota(jnp.int32, (blk, width), 1) >> 6
    shape = (HEADS_PER_GROUP * blk, 2 * blk)
    rowi = lax.broadcasted_iota(jnp.int32, shape, 0)
    ki = lax.broadcasted_iota(jnp.int32, shape, 1)
    steps = (rowi & (blk - 1)) + blk - ki
    in_band = (steps >= 0) & (steps <= blk)
    slope = jnp.full(shape, slopes[0], F32)
    for h in range(1, HEADS_PER_GROUP):
        slope = jnp.where(rowi >= h * blk, slopes[h], slope)
    alibi = (-LOG2E) * slope * (steps * dilation).astype(F32)
    bias = jnp.where(in_band, alibi, -jnp.inf)
    first_key = jnp.where(pl.program_id(1) == 0, blk, 0)
    bias_w0 = jnp.where(in_band & (ki >= first_key), alibi, -jnp.inf)

    def load(ref, rows):
        return jnp.concatenate([ref[s, rows, :] for s in range(ATTN_GROUP_SLABS)], axis=1)

    def unit(j, r):
        rows = pl.ds(j * window + r, blk, stride=dilation)
        q = (load(q_ref, rows) * (LOG2E * HEAD_DIM ** -0.5)).astype(BF16)
        if j == 0:
            prev = pl.ds(r, blk, stride=dilation)
            k = jnp.concatenate([load(kp_ref, prev), load(k_ref, rows)], axis=0).astype(BF16)
            v = jnp.concatenate([load(vp_ref, prev), load(v_ref, rows)], axis=0).astype(BF16)
            b = bias_w0
        else:
            both = pl.ds((j - 1) * window + r, 2 * blk, stride=dilation)
            k = load(k_ref, both).astype(BF16)
            v = load(v_ref, both).astype(BF16)
            b = bias
        yield
        qs = jnp.concatenate([jnp.where(lane_head == h, q, jnp.zeros_like(q)) for h in range(HEADS_PER_GROUP)],
                             axis=0)
        s = lax.dot_general(qs, k, (((1,), (1,)), ((), ())), preferred_element_type=F32) + b
        yield
        mx = jnp.max(s, axis=-1, keepdims=True)
        yield
        e = lax.exp2(s - mx)
        yield
        den = jnp.sum(e, axis=-1, keepdims=True)
        lse = mx * (1.0 / LOG2E) + jnp.log(den)
        yield
        prob = (e / den).astype(BF16)
        yield
        o = jnp.dot(prob[:blk], v, preferred_element_type=F32)
        lb = jnp.broadcast_to(lse[:blk], (blk, width))
        for h in range(1, HEADS_PER_GROUP):
            oh = jnp.dot(prob[h * blk:(h + 1) * blk], v, preferred_element_type=F32)
            o = jnp.where(lane_head == h, oh, o)
            lb = jnp.where(lane_head == h, lse[h * blk:(h + 1) * blk], lb)
        yield
        for s_ in range(ATTN_GROUP_SLABS):
            o_ref[s_, rows, :] = o[:, s_ * LANES:(s_ + 1) * LANES]
            l_ref[s_, rows, :] = lb[:, s_ * LANES:(s_ + 1) * LANES]

    def residues(i, carry):
        units = [(j, i * per_iter + dr) for dr in range(per_iter) for j in range(n_win)]
        for a in range(0, len(units), ATTN_UNROLL):
            _interleave(*(unit(j, r) for j, r in units[a:a + ATTN_UNROLL]))
        return carry

    per_iter = max(min(dilation, ATTN_UNROLL // n_win), 1)
    if dilation == per_iter:
        residues(0, 0)
    else:
        lax.fori_loop(0, dilation // per_iter, residues, 0)


def _alibi_slopes(n_heads):
    return [2.0 ** (-8.0 * (h + 1.0) / n_heads) for h in range(n_heads)]


def _dil_attention(qkv, group, batch, seq, weights=()):
    window, dilation = DIL_GROUPS[group]
    assert window // dilation == ATTN_BLK and seq % window == 0
    m = qkv.shape[1]
    tile = max(window, ATTN_TILE)
    assert seq % tile == 0
    n_win = tile // window
    ntile = seq // tile
    slopes = tuple(_alibi_slopes(ATTN_HEADS)[group * HEADS_PER_GROUP:(group + 1) * HEADS_PER_GROUP])
    groups = ATTN_DIM // ATTN_GROUP_DIM
    cur = lambda col: pl.BlockSpec((ATTN_GROUP_SLABS, tile, LANES), lambda b, i: (col, b * ntile + i, 0))
    prev = lambda col: pl.BlockSpec((ATTN_GROUP_SLABS, window, LANES),
                                    lambda b, i: (col, jnp.maximum((b * ntile + i) * n_win - 1, 0), 0))
    out_sds = jax.ShapeDtypeStruct((ATTN_GROUP_SLABS, m, LANES), F32)
    out_spec = pl.BlockSpec((ATTN_GROUP_SLABS, tile, LANES), lambda b, i: (0, b * ntile + i, 0))
    steps = batch * ntile
    w_specs = [pl.BlockSpec((w.shape[0] // steps, w.shape[1]), lambda b, i: (b * ntile + i, 0)) for w in weights]
    return pl.pallas_call(
        functools.partial(_dil_attn_kernel, window=window, dilation=dilation, slopes=slopes),
        out_shape=(out_sds, out_sds, *(jax.ShapeDtypeStruct(w.shape, BF16) for w in weights)),
        grid=(batch, ntile),
        in_specs=[cur(group), prev(groups + group), cur(groups + group),
                  prev(2 * groups + group), cur(2 * groups + group), *w_specs],
        out_specs=(out_spec, out_spec, *w_specs),
        compiler_params=_params("parallel", "arbitrary"),
        name=f"dilated_attn_g{group}",
    )(qkv, qkv, qkv, qkv, qkv, *weights)


def _merge_kernel(y_ref, o0_ref, o1_ref, o2_ref, l0_ref, l1_ref, l2_ref, x_ref,
                  g_ref, wg_ref, pr_ref, pa_ref, wo_ref, h_ref):
    d = x_ref.shape[1]
    rows_per = x_ref.shape[0] // MERGE_SPLIT

    def group(r):
        rows = pl.ds(r * rows_per, rows_per)
        wide = lambda ref: jnp.concatenate([ref[s, rows, :] for s in range(ATTN_GROUP_SLABS)], axis=1)
        x = x_ref[rows, :]
        gates = _dot(_rmsnorm_f32(x, g_ref[...]), wg_ref[...])
        yield
        t_rwkv = jnp.dot(y_ref[rows, :], pr_ref[...], preferred_element_type=F32)
        yield
        l0, l1, l2 = wide(l0_ref), wide(l1_ref), wide(l2_ref)
        mx = jnp.maximum(jnp.maximum(l0, l1), l2)
        e0, e1, e2 = jnp.exp(l0 - mx), jnp.exp(l1 - mx), jnp.exp(l2 - mx)
        y_attn = (e0 * wide(o0_ref) + e1 * wide(o1_ref) + e2 * wide(o2_ref)) / (e0 + e1 + e2)
        t_attn = _dot(y_attn, pa_ref[...])
        yield
        merged = jax.nn.sigmoid(gates[:, :d]) * t_rwkv
        yield
        merged = merged + jax.nn.sigmoid(gates[:, d:]) * t_attn
        yield
        h_ref[rows, :] = x + _dot(merged, wo_ref[...])

    _interleave(*(_delayed(group(r), 2 * r) for r in range(MERGE_SPLIT)))


def _merge(y_rwkv, attn, x, g, wg, p_rwkv, p_attn, w_out, tm):
    m, d = x.shape
    row = lambda w: pl.BlockSpec((tm, w), lambda i: (i, 0))
    slab = pl.BlockSpec((ATTN_GROUP_SLABS, tm, LANES), lambda i: (0, i, 0))
    full = lambda a: pl.BlockSpec(a.shape, lambda i: (0, 0), pipeline_mode=pl.Buffered(1))
    (o0, l0), (o1, l1), (o2, l2) = attn
    return pl.pallas_call(
        _merge_kernel,
        out_shape=jax.ShapeDtypeStruct((m, d), F32),
        grid=(m // tm,),
        in_specs=[row(d)] + [slab] * 6 + [row(d), full(g), full(wg), full(p_rwkv), full(p_attn), full(w_out)],
        out_specs=row(d),
        compiler_params=_params("parallel"),
        name="gated_merge",
    )(y_rwkv, o0, o1, o2, l0, l1, l2, x, g, wg, p_rwkv, p_attn, w_out)


def _mem_kv_kernel(mem_ref, g_ref, w_ref, o_ref):
    memn = _rmsnorm_f32(mem_ref[...], g_ref[...])
    o_ref[...] = _dot(memn, w_ref[...]).astype(o_ref.dtype)


def _mem_kv(mem2d, g, w_kv, mem_len):
    m, d = mem2d.shape
    n = w_kv.shape[1]
    return pl.pallas_call(
        _mem_kv_kernel,
        out_shape=jax.ShapeDtypeStruct((m, n), BF16),
        grid=(m // mem_len,),
        in_specs=[pl.BlockSpec((mem_len, d), lambda i: (i, 0)), pl.BlockSpec((1, d), lambda i: (0, 0)),
                  pl.BlockSpec((d, n), lambda i: (0, 0))],
        out_specs=pl.BlockSpec((mem_len, n), lambda i: (i, 0)),
        compiler_params=_params("parallel"),
        name="mem_kv_proj",
    )(mem2d, g.reshape(1, d), w_kv)


def _xattn_kernel(h_ref, g_ref, wq_ref, kv_ref, wo_ref, gf_ref, o_ref, xn_ref):
    d = h_ref.shape[1]
    hd = d // XATTN_HEADS
    heads = range(XATTN_HEADS)
    cs = lambda hh: slice(hh * hd, (hh + 1) * hd)
    rows_per = h_ref.shape[0] // XATTN_SPLIT

    def group(r):
        rows = pl.ds(r * rows_per, rows_per)
        h = h_ref[rows, :]
        q = (_dot(_rmsnorm_f32(h, g_ref[...]), wq_ref[...]) * (LOG2E * hd ** -0.5)).astype(BF16)
        yield
        s = [_dot_nt(q[:, cs(hh)], kv_ref[:, cs(hh)]) for hh in heads]
        yield
        e = [lax.exp2(s[hh] - jnp.max(s[hh], axis=-1, keepdims=True)) for hh in heads]
        yield
        prob = [e[hh] / jnp.sum(e[hh], axis=-1, keepdims=True) for hh in heads]
        yield
        outs = [_dot(prob[hh], kv_ref[:, d + hh * hd:d + (hh + 1) * hd]) for hh in heads]
        yield
        h2 = h + _dot(jnp.concatenate(outs, axis=-1), wo_ref[...])
        o_ref[rows, :] = h2
        yield
        xn_ref[rows, :] = _rmsnorm_f32(h2, gf_ref[...]).astype(xn_ref.dtype)

    _interleave(*(_delayed(group(r), 2 * r) for r in range(XATTN_SPLIT)))


def _xattn(h, g, wq, kv, wo, g_ffn, seq, mem_len, tm):
    m, d = h.shape
    tiles_per_seq = seq // tm
    full = lambda a: pl.BlockSpec(a.shape, lambda i: (0, 0), pipeline_mode=pl.Buffered(1))
    row = pl.BlockSpec((tm, d), lambda i: (i, 0))
    return pl.pallas_call(
        _xattn_kernel,
        out_shape=(jax.ShapeDtypeStruct((m, d), F32), jax.ShapeDtypeStruct((m, d), BF16)),
        grid=(m // tm,),
        in_specs=[row, full(g), full(wq), pl.BlockSpec((mem_len, 2 * d), lambda i: (i // tiles_per_seq, 0)),
                  full(wo), full(g_ffn)],
        out_specs=(row, row),
        compiler_params=_params("parallel"),
        name="mem_cross_attn",
    )(h, g, wq, kv, wo, g_ffn)


def _ffn_kernel(h_ref, xn_ref, w1_ref, w2_ref, gf_ref, o_ref):
    tm = h_ref.shape[0]
    f = w1_ref.shape[1]
    rows_per = tm // FFN_SPLIT

    def group(r):
        rows = pl.ds(r * rows_per, rows_per)
        xn = xn_ref[rows, :]
        acc = None
        for c in range(f // FFN_TF):
            cs = slice(c * FFN_TF, (c + 1) * FFN_TF)
            u = jnp.dot(xn, w1_ref[:, cs], preferred_element_type=F32)
            act = jnp.square(jnp.maximum(u, 0.0)).astype(BF16)
            part = jnp.dot(act, w2_ref[cs, :], preferred_element_type=F32)
            acc = part if acc is None else acc + part
            yield
        o_ref[rows, :] = _rmsnorm_f32(h_ref[rows, :] + acc, gf_ref[...])

    _interleave(*(_delayed(group(r), r) for r in range(FFN_SPLIT)))


def _ffn(h, xn, w1, w2, g_final, tm):
    m, d = h.shape
    full = lambda a: pl.BlockSpec(a.shape, lambda i: (0, 0), pipeline_mode=pl.Buffered(1))
    row = pl.BlockSpec((tm, d), lambda i: (i, 0))
    return pl.pallas_call(
        _ffn_kernel,
        out_shape=jax.ShapeDtypeStruct((m, d), F32),
        grid=(m // tm,),
        in_specs=[row, row, full(w1), full(w2), full(g_final)],
        out_specs=row,
        compiler_params=_params("parallel"),
        name="relu2_mlp_final_norm",
    )(h, xn, w1, w2, g_final)


def _pad_cols(a, width):
    return jnp.pad(a, ((0, 0), (0, width - a.shape[1])))


def _pad_rows(a, rows):
    return jnp.pad(a, ((0, rows - a.shape[0]), (0, 0)))


def kernel(x, mem, norm_mix_g, w_in, shift_mu, w0, w2, a0, a2, g2, k_k, k_a, r_k, gn_w, gn_b, p_rwkv, p_attn, w_out, norm_x_g, norm_mem_g, xa_wq, xa_wkv, xa_wo, norm_ffn_g, ffn_w1, ffn_w2, norm_final_g):
    batch, seq, d = x.shape
    mem_len = mem.shape[1]
    assert w_in.shape[0] == 1 and d == RWKV_DIM and seq % RW_TM == 0
    m = batch * seq
    x2 = x.reshape(m, d)

    w = w_in[0]
    c_wd, c_ad, c_gd = 3 * RWKV_DIM, 3 * RWKV_DIM + DECAY_LORA, 3 * RWKV_DIM + DECAY_LORA + AAA_LORA
    c_q = c_gd + GATE_LORA
    c_gate = c_q + 3 * ATTN_DIM
    lora_cols = lambda a: jnp.concatenate(
        [_pad_cols(a[:, c_wd:c_ad], 128), _pad_cols(a[:, c_ad:c_gd], 128), _pad_cols(a[:, c_gd:c_q], 256)], axis=1)
    mu = shift_mu[0].reshape(1, -1).astype(F32)
    w_main, w_lora = w[:, :c_wd].astype(BF16), lora_cols(w).astype(BF16)
    mu_main, mu_lora = mu[:, :c_wd], lora_cols(mu)
    w_qkv = w[:, c_q:c_gate].astype(BF16)
    w_gate = w[:, c_gate:].astype(BF16)
    vecs = jnp.stack([w0[0], a0[0], k_k[0], k_a[0], r_k[0].reshape(-1), gn_w[0], gn_b[0],
                      jnp.zeros((RWKV_DIM,), F32)]).astype(F32)
    w2p = _pad_rows(w2[0], 128).astype(BF16)
    a2p = _pad_rows(a2[0], 128).astype(BF16)
    g2p = _pad_rows(g2[0], 256).astype(BF16)

    g_mix = norm_mix_g[0].reshape(1, d)
    y_rwkv, qkv = _rwkv(x2, g_mix, w_main, w_lora, mu_main, mu_lora, w_qkv, vecs, w2p, a2p, g2p, batch, seq)
    later = ((), (p_rwkv[0], p_attn[0], w_out[0], xa_wq[0], xa_wo[0], xa_wkv[0]), (ffn_w1[0], ffn_w2[0]))
    attn = [_dil_attention(qkv, g, batch, seq, later[g]) for g in range(len(DIL_GROUPS))]
    p_rwkv_b, p_attn_b, w_out_b, wq_b, wo_b, wkv_b = attn[1][2:]
    w1_b, w2_b = attn[2][2:]
    h1 = _merge(y_rwkv, [a[:2] for a in attn], x2, g_mix, w_gate, p_rwkv_b, p_attn_b, w_out_b, MERGE_TM)
    kv = _mem_kv(mem.reshape(batch * mem_len, d), norm_mem_g[0], wkv_b, mem_len)
    h2, xn2 = _xattn(h1, norm_x_g[0].reshape(1, d), wq_b, kv, wo_b, norm_ffn_g[0].reshape(1, d), seq, mem_len,
                     XATTN_TM)
    out = _ffn(h2, xn2, w1_b, w2_b, norm_final_g.reshape(1, d), FFN_TM)
    return out.reshape(batch, seq, d)
```

```python
import functools
import math

import jax
import jax.numpy as jnp
from jax import lax
from jax.experimental import pallas as pl
from jax.experimental.pallas import tpu as pltpu

F32 = jnp.float32
BF16 = jnp.bfloat16

LANES = 128
HEAD_DIM = 64
PAIR = 2 * HEAD_DIM
CHUNK = 64
RWKV_DIM = 1024
N_PAIRS = RWKV_DIM // PAIR
DECAY_LORA = 64
AAA_LORA = 64
GATE_LORA = 160
LORA_WD_OFF = 3 * RWKV_DIM
LORA_AD_OFF = LORA_WD_OFF + 128
LORA_GD_OFF = LORA_AD_OFF + 128
RW_WIDTH = LORA_GD_OFF + 256
RW_TM = 512
RW_GROUP = 4
RW_TN = 256
RW_BG_EVERY = (2, 5, 2)
GN_EPS = HEAD_DIM * 1e-5
LOG2E = 1.0 / math.log(2.0)
DECAY_SCALE = math.exp(-0.5)
NORM_EPS = 1e-6
DIL_GROUPS = ((128, 1), (512, 4), (2048, 16))
HEADS_PER_GROUP = 4
ATTN_HEADS = 12
ATTN_GROUP_DIM = HEADS_PER_GROUP * HEAD_DIM
ATTN_GROUP_SLABS = ATTN_GROUP_DIM // LANES
ATTN_DIM = ATTN_HEADS * HEAD_DIM
ATTN_BLK = 128
ATTN_TILE = 2048
ATTN_UNROLL = 4
XATTN_HEADS = 4
MERGE_TM = 1024
MERGE_SPLIT = 2
XATTN_TM = 1024
XATTN_SPLIT = 2
FFN_TM = 1024
FFN_SPLIT = 2
FFN_TF = 1024
VMEM_LIMIT = 56 * 1024 * 1024


def _dot(a, b):
    return jnp.dot(a.astype(BF16), b.astype(BF16), preferred_element_type=F32)


def _dot_nt(a, b):
    return lax.dot_general(a.astype(BF16), b.astype(BF16), (((1,), (1,)), ((), ())),
                           preferred_element_type=F32)


def _params(*sem):
    return pltpu.CompilerParams(dimension_semantics=sem, vmem_limit_bytes=VMEM_LIMIT)


def _rmsnorm_f32(x, g):
    return x * lax.rsqrt(jnp.mean(x * x, axis=-1, keepdims=True) + NORM_EPS) * g


VEC_W0, VEC_A0, VEC_KK, VEC_KA, VEC_RK, VEC_GNW, VEC_GNB = range(7)


def _interleave(*gens, background=None):
    live = list(gens)
    while live:
        for g in list(live):
            try:
                next(g)
            except StopIteration:
                live.remove(g)
        if background is not None:
            next(background, None)


def _every(gen, n):
    for _ in gen:
        for _ in range(n):
            yield


def _delayed(gen, steps):
    for _ in range(steps):
        yield
    yield from gen


def _rwkv_phases(vec_ref, w2_ref, a2_ref, g2_ref, st_ref):
    L = CHUNK
    pairs = range(N_PAIRS)
    cat = jnp.concatenate

    head_a = lax.broadcasted_iota(jnp.int32, (L, PAIR), 1) < HEAD_DIM
    rowh = lax.broadcasted_iota(jnp.int32, (L, PAIR), 0)
    laneh = lax.broadcasted_iota(jnp.int32, (L, PAIR), 1)
    eye_pk = jnp.where((laneh == rowh) | (laneh == rowh + HEAD_DIM), 1.0, 0.0)
    rows_sc = lax.broadcasted_iota(jnp.int32, (2 * L, 2 * PAIR), 0)
    src_sc = lax.broadcasted_iota(jnp.int32, (2 * L, 2 * PAIR), 1) & (L - 1)
    tri_mask = ((rows_sc < L) & (rows_sc > src_sc)) | ((rows_sc >= L) & (rows_sc - L >= src_sc))
    r3 = lax.broadcasted_iota(jnp.int32, (L, 3 * L), 0)
    c3 = lax.broadcasted_iota(jnp.int32, (L, 3 * L), 1)
    c3 = c3 - jnp.where(c3 >= L, L, 0) - jnp.where(c3 >= 2 * L, L, 0)
    tri3 = jnp.where(r3 >= c3, 1.0, 0.0).astype(BF16)
    zb = jnp.zeros((L, PAIR), BF16)
    vec = lambda i: vec_ref[i:i + 1, :]

    def hsum(x):
        sa = jnp.sum(jnp.where(head_a, x, 0.0), axis=-1, keepdims=True)
        sb = jnp.sum(jnp.where(head_a, 0.0, x), axis=-1, keepdims=True)
        return jnp.where(head_a, sa, sb)

    def sl(x, p):
        return x[:, p * PAIR:(p + 1) * PAIR]

    m_a = lambda x: jnp.where(head_a, x, jnp.zeros_like(x))
    m_b = lambda x: jnp.where(head_a, jnp.zeros_like(x), x)

    def heads_rows(*cols):
        return cat([cat([m_a(x) for x in cols], axis=1), cat([m_b(x) for x in cols], axis=1)], axis=0)

    def prep(p_ref, ops):
        pr = p_ref[:, 0:RWKV_DIM].astype(F32)
        pk = p_ref[:, RWKV_DIM:2 * RWKV_DIM].astype(F32)
        pv = p_ref[:, 2 * RWKV_DIM:3 * RWKV_DIM].astype(F32)
        wd = jnp.tanh(p_ref[:, LORA_WD_OFF:LORA_WD_OFF + 128].astype(F32))
        lw = -DECAY_SCALE * jax.nn.sigmoid(vec(VEC_W0) + _dot(wd, w2_ref[...]))
        yield
        a_lr = jax.nn.sigmoid(vec(VEC_A0) + _dot(p_ref[:, LORA_AD_OFF:LORA_AD_OFF + 128], a2_ref[...]))
        yield
        ops["gate"] = _dot(jax.nn.sigmoid(p_ref[:, LORA_GD_OFF:LORA_GD_OFF + 256].astype(F32)), g2_ref[...])
        yield
        kk = pk * vec(VEC_KK)
        kbar = pk * (1.0 + (a_lr - 1.0) * vec(VEC_KA))
        yield
        h1 = lw.astype(BF16)
        r1 = lw - h1.astype(F32)
        h2 = r1.astype(BF16)
        h3 = (r1 - h2.astype(F32)).astype(BF16)
        c = jnp.dot(tri3, cat([h1, h2, h3], axis=0), preferred_element_type=F32)
        yield
        cl = c[L - 1:L, :]
        einv = jnp.exp(-c)
        yield
        edec = jnp.exp(cl - c)
        yield
        rt = pr * jnp.exp(c)
        yield
        eprev = jnp.exp(c - lw)
        ops["gl"] = jnp.exp(cl)
        ops["rkr"] = pr * kbar * vec(VEC_RK)
        ops["rt"] = rt
        ops["pv"] = pv
        yield
        ss = [hsum(sl(kk, p) * sl(kk, p)) for p in pairs]
        yield
        kkn = [sl(kk, p) * lax.rsqrt(jnp.maximum(ss[p], 1e-24)) for p in pairs]
        bvec = [kkn[p] * sl(a_lr, p) for p in pairs]
        yield
        ops["at"] = [(-kkn[p] * sl(eprev, p)).astype(BF16) for p in pairs]
        ops["rtp"] = [sl(rt, p).astype(BF16) for p in pairs]
        yield
        ops["bt"] = [(bvec[p] * sl(einv, p)).astype(BF16) for p in pairs]
        ops["kt"] = [(sl(kbar, p) * sl(einv, p)).astype(BF16) for p in pairs]
        yield
        ops["bh"] = [bvec[p] * sl(edec, p) for p in pairs]
        ops["kh"] = [sl(kbar, p) * sl(edec, p) for p in pairs]
        ops["vp"] = [sl(pv, p).astype(BF16) for p in pairs]

    def main(ops):
        at, rtp, bt, kt, bh, kh, vp = (ops[k] for k in ("at", "rtp", "bt", "kt", "bh", "kh", "vp"))
        sc = [_dot_nt(cat([at[p], rtp[p]], axis=0), cat([m_a(bt[p]), m_b(kt[p]), m_a(kt[p]), m_b(bt[p])], axis=0))
              for p in pairs]
        yield
        sc = [jnp.where(tri_mask, sc[p], 0.0) for p in pairs]
        npk = [jnp.where(head_a, sc[p][:L, :PAIR], sc[p][:L, PAIR:]) for p in pairs]
        aak = [jnp.where(head_a, sc[p][:L, PAIR:], sc[p][:L, :PAIR]) for p in pairs]
        bot_sc = [sc[p][L:] for p in pairs]
        yield
        aakv = [_dot(aak[p], heads_rows(vp[p])).astype(BF16) for p in pairs]
        yield
        tpk = [eye_pk for p in pairs]
        for _ in range(6):
            res = [_dot(npk[p], heads_rows(npk[p].astype(BF16), tpk[p].astype(BF16))) for p in pairs]
            npk = [res[p][:, :PAIR] for p in pairs]
            tpk = [tpk[p] + res[p][:, PAIR:] for p in pairs]
            yield
        pq = [_dot(tpk[p], heads_rows(at[p], aakv[p])) for p in pairs]
        yield
        pb = [pq[p][:, :PAIR].astype(BF16) for p in pairs]
        qb = [pq[p][:, PAIR:].astype(BF16) for p in pairs]
        rhs = [cat([cat([m_a(pb[p]), m_a(qb[p])], axis=1), cat([zb, m_b(vp[p])], axis=1),
                    cat([zb, m_a(vp[p])], axis=1), cat([m_b(pb[p]), m_b(qb[p])], axis=1)], axis=0) for p in pairs]
        zt1 = [cat([bh[p], kh[p]], axis=0).T for p in pairs]
        zt2 = [cat([kh[p], bh[p]], axis=0).T for p in pairs]
        yield
        lhs_st = [cat([jnp.where(head_a, zt1[p][:L], zt1[p][L:]), jnp.where(head_a, zt2[p][:L], zt2[p][L:])],
                      axis=1) for p in pairs]
        tb = [_dot(cat([bot_sc[p], lhs_st[p]], axis=0), rhs[p]) for p in pairs]
        yield
        st = [st_ref[p].astype(BF16) for p in pairs]
        ys = [_dot(cat([sl(ops["rt"], p) + tb[p][:L, :PAIR],
                        jnp.where(eye_pk > 0.0, sl(ops["gl"], p), 0.0) + tb[p][L:, :PAIR]], axis=0),
                   heads_rows(st[p])) for p in pairs]
        yield
        for p in pairs:
            st_ref[p] = ys[p][L:] + tb[p][L:, PAIR:]
        ops["y"] = [ys[p][:L] + tb[p][:L, PAIR:] for p in pairs]

    def norm(ops, y_ref):
        y = ops["y"]
        mean = [hsum(y[p]) * (1.0 / HEAD_DIM) for p in pairs]
        yield
        dev = [y[p] - mean[p] for p in pairs]
        var = [hsum(dev[p] * dev[p]) * (1.0 / HEAD_DIM) for p in pairs]
        yield
        bonus = [hsum(sl(ops["rkr"], p)) * sl(ops["pv"], p) for p in pairs]
        yield
        for p in pairs:
            cs = slice(p * PAIR, (p + 1) * PAIR)
            yn = (dev[p] * lax.rsqrt(var[p] + GN_EPS) * vec_ref[VEC_GNW:VEC_GNW + 1, cs]
                  + vec_ref[VEC_GNB:VEC_GNB + 1, cs])
            y_ref[:, cs] = ((yn + bonus[p]) * sl(ops["gate"], p)).astype(y_ref.dtype)
            if p % 2 == 1:
                yield

    return prep, main, norm


def _rwkv_kernel(x_ref, xn_ref, g_ref, wm_ref, wl_ref, mum_ref, mul_ref, wqkv_ref, vec_ref, w2_ref, a2_ref, g2_ref,
                 y_ref, qkv_ref, p_scr, n_scr, last_ref, st_ref):
    tm = x_ref.shape[0]
    n_chunks = tm // CHUNK
    t = pl.program_id(1)
    slot = lax.rem(t, 2)
    row = lax.broadcasted_iota(jnp.int32, (tm, RW_TN), 0)

    def norm_x(src_ref, dst_slot):
        n_scr[dst_slot] = _rmsnorm_f32(src_ref[...], g_ref[...]).astype(BF16)
        yield

    def rw_proj(src_slot, dst_slot):
        n = n_scr[src_slot]
        blocks = [(wm_ref, mum_ref, j, j) for j in range(wm_ref.shape[1] // RW_TN)]
        blocks += [(wl_ref, mul_ref, j, wm_ref.shape[1] // RW_TN + j) for j in range(wl_ref.shape[1] // RW_TN)]
        for w_ref, mu_ref, j, jd in blocks:
            cs = slice(j * RW_TN, (j + 1) * RW_TN)
            cd = slice(jd * RW_TN, (jd + 1) * RW_TN)
            p = jnp.dot(n, w_ref[:, cs], preferred_element_type=F32)
            prev = jnp.where(row == 0, last_ref[:, cd], pltpu.roll(p, 1, 0))
            last_ref[:, cd] = p[tm - 1:tm, :]
            p_scr[dst_slot, :, cd] = (p + mu_ref[:, cs] * (prev - p)).astype(p_scr.dtype)
            yield

    def qkv_proj():
        n = n_scr[slot]
        per = ATTN_GROUP_SLABS
        for j in range(wqkv_ref.shape[1] // ATTN_GROUP_DIM):
            res = jnp.dot(n, wqkv_ref[:, j * ATTN_GROUP_DIM:(j + 1) * ATTN_GROUP_DIM], preferred_element_type=F32)
            for s in range(per):
                qkv_ref[j * per + s] = res[:, s * LANES:(s + 1) * LANES]
            yield

    @pl.when(t == 0)
    def _init():
        st_ref[...] = jnp.zeros_like(st_ref)
        last_ref[...] = jnp.zeros_like(last_ref)
        _interleave(norm_x(x_ref, 0))
        _interleave(rw_proj(0, 0))

    def background():
        yield from qkv_proj()
        yield from norm_x(xn_ref, 1 - slot)
        yield from rw_proj(1 - slot, 1 - slot)

    prep, main, norm = _rwkv_phases(vec_ref, w2_ref, a2_ref, g2_ref, st_ref)
    rows = [pl.ds(u * CHUNK, CHUNK) for u in range(n_chunks)]
    ops = [{} for _ in range(n_chunks)]
    prep_u = lambda u: prep(p_scr.at[slot, rows[u]], ops[u])
    norm_u = lambda u: norm(ops[u], y_ref.at[rows[u]])
    bg = background()
    for first in range(0, n_chunks, RW_GROUP):
        group = range(first, first + RW_GROUP)
        _interleave(*(prep_u(u) for u in group), background=_every(bg, RW_BG_EVERY[0]))
        _interleave(*(_delayed(main(ops[u]), 2 * (u - first)) for u in group),
                    background=_every(bg, RW_BG_EVERY[1]))
        _interleave(*(norm_u(u) for u in group), background=_every(bg, RW_BG_EVERY[2]))
    _interleave(bg)


def _rwkv(x, g, w_main, w_lora, mu_main, mu_lora, w_qkv, vecs, w2p, a2p, g2p, batch, seq):
    m, d = x.shape
    nt = seq // RW_TM
    slabs = w_qkv.shape[1] // LANES
    assert w_main.shape[1] + w_lora.shape[1] == RW_WIDTH
    full = lambda a: pl.BlockSpec(a.shape, lambda b, t: (0, 0), pipeline_mode=pl.Buffered(1))
    return pl.pallas_call(
        _rwkv_kernel,
        out_shape=(jax.ShapeDtypeStruct((m, RWKV_DIM), BF16), jax.ShapeDtypeStruct((slabs, m, LANES), F32)),
        grid=(batch, nt),
        in_specs=[pl.BlockSpec((RW_TM, d), lambda b, t: (b * nt + t, 0)),
                  pl.BlockSpec((RW_TM, d), lambda b, t: (b * nt + jnp.minimum(t + 1, nt - 1), 0)),
                  full(g), full(w_main), full(w_lora), full(mu_main), full(mu_lora), full(w_qkv), full(vecs),
                  full(w2p), full(a2p), full(g2p)],
        out_specs=(pl.BlockSpec((RW_TM, RWKV_DIM), lambda b, t: (b * nt + t, 0)),
                   pl.BlockSpec((slabs, RW_TM, LANES), lambda b, t: (0, b * nt + t, 0))),
        scratch_shapes=[pltpu.VMEM((2, RW_TM, RW_WIDTH), BF16), pltpu.VMEM((2, RW_TM, d), BF16),
                        pltpu.VMEM((1, RW_WIDTH), F32), pltpu.VMEM((N_PAIRS, HEAD_DIM, PAIR), F32)],
        compiler_params=_params("parallel", "arbitrary"),
        name="rwkv7_proj_chunk_scan",
    )(x, x, g, w_main, w_lora, mu_main, mu_lora, w_qkv, vecs, w2p, a2p, g2p)


def _dil_attn_kernel(q_ref, kp_ref, k_ref, vp_ref, v_ref, *rest, window, dilation, slopes):
    n_casts = (len(rest) - 2) // 2
    o_ref, l_ref = rest[n_casts], rest[n_casts + 1]
    for src, dst in zip(rest[:n_casts], rest[n_casts + 2:]):
        dst[...] = src[...].astype(dst.dtype)
    blk = ATTN_BLK
    width = ATTN_GROUP_DIM
    n_win = q_ref.shape[1] // window
    lane_head = lax.broadcasted_iota(jnp.int32, (blk, width), 1) >> 6
    shape = (HEADS_PER_GROUP * blk, 2 * blk)
    rowi = lax.broadcasted_iota(jnp.int32, shape, 0)
    ki = lax.broadcasted_iota(jnp.int32, shape, 1)
    steps = (rowi & (blk - 1)) + blk - ki
    in_band = (steps >= 0) & (steps <= blk)
    slope = jnp.full(shape, slopes[0], F32)
    for h in range(1, HEADS_PER_GROUP):
        slope = jnp.where(rowi >= h * blk, slopes[h], slope)
    alibi = (-LOG2E) * slope * (steps * dilation).astype(F32)
    bias = jnp.where(in_band, alibi, -jnp.inf)
    first_key = jnp.where(pl.program_id(1) == 0, blk, 0)
    bias_w0 = jnp.where(in_band & (ki >= first_key), alibi, -jnp.inf)

    def load(ref, rows):
        return jnp.concatenate([ref[s, rows, :] for s in range(ATTN_GROUP_SLABS)], axis=1)

    def unit(j, r):
        rows = pl.ds(j * window + r, blk, stride=dilation)
        q = (load(q_ref, rows) * (LOG2E * HEAD_DIM ** -0.5)).astype(BF16)
        if j == 0:
            prev = pl.ds(r, blk, stride=dilation)
            k = jnp.concatenate([load(kp_ref, prev), load(k_ref, rows)], axis=0).astype(BF16)
            v = jnp.concatenate([load(vp_ref, prev), load(v_ref, rows)], axis=0).astype(BF16)
            b = bias_w0
        else:
            both = pl.ds((j - 1) * window + r, 2 * blk, stride=dilation)
            k = load(k_ref, both).astype(BF16)
            v = load(v_ref, both).astype(BF16)
            b = bias
        yield
        qs = jnp.concatenate([jnp.where(lane_head == h, q, jnp.zeros_like(q)) for h in range(HEADS_PER_GROUP)],
                             axis=0)
        s = lax.dot_general(qs, k, (((1,), (1,)), ((), ())), preferred_element_type=F32) + b
        yield
        mx = jnp.max(s, axis=-1, keepdims=True)
        yield
        e = lax.exp2(s - mx)
        yield
        den = jnp.sum(e, axis=-1, keepdims=True)
        lse = mx * (1.0 / LOG2E) + jnp.log(den)
        yield
        prob = (e / den).astype(BF16)
        yield
        o = jnp.dot(prob[:blk], v, preferred_element_type=F32)
        lb = jnp.broadcast_to(lse[:blk], (blk, width))
        for h in range(1, HEADS_PER_GROUP):
            oh = jnp.dot(prob[h * blk:(h + 1) * blk], v, preferred_element_type=F32)
            o = jnp.where(lane_head == h, oh, o)
            lb = jnp.where(lane_head == h, lse[h * blk:(h + 1) * blk], lb)
        yield
        for s_ in range(ATTN_GROUP_SLABS):
            o_ref[s_, rows, :] = o[:, s_ * LANES:(s_ + 1) * LANES]
            l_ref[s_, rows, :] = lb[:, s_ * LANES:(s_ + 1) * LANES]

    def residues(i, carry):
        units = [(j, i * per_iter + dr) for dr in range(per_iter) for j in range(n_win)]
        for a in range(0, len(units), ATTN_UNROLL):
            _interleave(*(unit(j, r) for j, r in units[a:a + ATTN_UNROLL]))
        return carry

    per_iter = max(min(dilation, ATTN_UNROLL // n_win), 1)
    if dilation == per_iter:
        residues(0, 0)
    else:
        lax.fori_loop(0, dilation // per_iter, residues, 0)


def _alibi_slopes(n_heads):
    return [2.0 ** (-8.0 * (h + 1.0) / n_heads) for h in range(n_heads)]


def _dil_attention(qkv, group, batch, seq, weights=()):
    window, dilation = DIL_GROUPS[group]
    assert window // dilation == ATTN_BLK and seq % window == 0
    m = qkv.shape[1]
    tile = max(window, ATTN_TILE)
    assert seq % tile == 0
    n_win = tile // window
    ntile = seq // tile
    slopes = tuple(_alibi_slopes(ATTN_HEADS)[group * HEADS_PER_GROUP:(group + 1) * HEADS_PER_GROUP])
    groups = ATTN_DIM // ATTN_GROUP_DIM
    cur = lambda col: pl.BlockSpec((ATTN_GROUP_SLABS, tile, LANES), lambda b, i: (col, b * ntile + i, 0))
    prev = lambda col: pl.BlockSpec((ATTN_GROUP_SLABS, window, LANES),
                                    lambda b, i: (col, jnp.maximum((b * ntile + i) * n_win - 1, 0), 0))
    out_sds = jax.ShapeDtypeStruct((ATTN_GROUP_SLABS, m, LANES), F32)
    out_spec = pl.BlockSpec((ATTN_GROUP_SLABS, tile, LANES), lambda b, i: (0, b * ntile + i, 0))
    steps = batch * ntile
    w_specs = [pl.BlockSpec((w.shape[0] // steps, w.shape[1]), lambda b, i: (b * ntile + i, 0)) for w in weights]
    return pl.pallas_call(
        functools.partial(_dil_attn_kernel, window=window, dilation=dilation, slopes=slopes),
        out_shape=(out_sds, out_sds, *(jax.ShapeDtypeStruct(w.shape, BF16) for w in weights)),
        grid=(batch, ntile),
        in_specs=[cur(group), prev(groups + group), cur(groups + group),
                  prev(2 * groups + group), cur(2 * groups + group), *w_specs],
        out_specs=(out_spec, out_spec, *w_specs),
        compiler_params=_params("parallel", "arbitrary"),
        name=f"dilated_attn_g{group}",
    )(qkv, qkv, qkv, qkv, qkv, *weights)


def _merge_kernel(y_ref, o0_ref, o1_ref, o2_ref, l0_ref, l1_ref, l2_ref, x_ref,
                  g_ref, wg_ref, pr_ref, pa_ref, wo_ref, h_ref):
    d = x_ref.shape[1]
    rows_per = x_ref.shape[0] // MERGE_SPLIT

    def group(r):
        rows = pl.ds(r * rows_per, rows_per)
        wide = lambda ref: jnp.concatenate([ref[s, rows, :] for s in range(ATTN_GROUP_SLABS)], axis=1)
        x = x_ref[rows, :]
        gates = _dot(_rmsnorm_f32(x, g_ref[...]), wg_ref[...])
        yield
        t_rwkv = jnp.dot(y_ref[rows, :], pr_ref[...], preferred_element_type=F32)
        yield
        l0, l1, l2 = wide(l0_ref), wide(l1_ref), wide(l2_ref)
        mx = jnp.maximum(jnp.maximum(l0, l1), l2)
        e0, e1, e2 = jnp.exp(l0 - mx), jnp.exp(l1 - mx), jnp.exp(l2 - mx)
        y_attn = (e0 * wide(o0_ref) + e1 * wide(o1_ref) + e2 * wide(o2_ref)) / (e0 + e1 + e2)
        t_attn = _dot(y_attn, pa_ref[...])
        yield
        merged = jax.nn.sigmoid(gates[:, :d]) * t_rwkv
        yield
        merged = merged + jax.nn.sigmoid(gates[:, d:]) * t_attn
        yield
        h_ref[rows, :] = x + _dot(merged, wo_ref[...])

    _interleave(*(_delayed(group(r), 2 * r) for r in range(MERGE_SPLIT)))


def _merge(y_rwkv, attn, x, g, wg, p_rwkv, p_attn, w_out, tm):
    m, d = x.shape
    row = lambda w: pl.BlockSpec((tm, w), lambda i: (i, 0))
    slab = pl.BlockSpec((ATTN_GROUP_SLABS, tm, LANES), lambda i: (0, i, 0))
    full = lambda a: pl.BlockSpec(a.shape, lambda i: (0, 0), pipeline_mode=pl.Buffered(1))
    (o0, l0), (o1, l1), (o2, l2) = attn
    return pl.pallas_call(
        _merge_kernel,
        out_shape=jax.ShapeDtypeStruct((m, d), F32),
        grid=(m // tm,),
        in_specs=[row(d)] + [slab] * 6 + [row(d), full(g), full(wg), full(p_rwkv), full(p_attn), full(w_out)],
        out_specs=row(d),
        compiler_params=_params("parallel"),
        name="gated_merge",
    )(y_rwkv, o0, o1, o2, l0, l1, l2, x, g, wg, p_rwkv, p_attn, w_out)


def _mem_kv_kernel(mem_ref, g_ref, w_ref, o_ref):
    memn = _rmsnorm_f32(mem_ref[...], g_ref[...])
    o_ref[...] = _dot(memn, w_ref[...]).astype(o_ref.dtype)


def _mem_kv(mem2d, g, w_kv, mem_len):
    m, d = mem2d.shape
    n = w_kv.shape[1]
    return pl.pallas_call(
        _mem_kv_kernel,
        out_shape=jax.ShapeDtypeStruct((m, n), BF16),
        grid=(m // mem_len,),
        in_specs=[pl.BlockSpec((mem_len, d), lambda i: (i, 0)), pl.BlockSpec((1, d), lambda i: (0, 0)),
                  pl.BlockSpec((d, n), lambda i: (0, 0))],
        out_specs=pl.BlockSpec((mem_len, n), lambda i: (i, 0)),
        compiler_params=_params("parallel"),
        name="mem_kv_proj",
    )(mem2d, g.reshape(1, d), w_kv)


def _xattn_kernel(h_ref, g_ref, wq_ref, kv_ref, wo_ref, gf_ref, o_ref, xn_ref):
    d = h_ref.shape[1]
    hd = d // XATTN_HEADS
    heads = range(XATTN_HEADS)
    cs = lambda hh: slice(hh * hd, (hh + 1) * hd)
    rows_per = h_ref.shape[0] // XATTN_SPLIT

    def group(r):
        rows = pl.ds(r * rows_per, rows_per)
        h = h_ref[rows, :]
        q = (_dot(_rmsnorm_f32(h, g_ref[...]), wq_ref[...]) * (LOG2E * hd ** -0.5)).astype(BF16)
        yield
        s = [_dot_nt(q[:, cs(hh)], kv_ref[:, cs(hh)]) for hh in heads]
        yield
        e = [lax.exp2(s[hh] - jnp.max(s[hh], axis=-1, keepdims=True)) for hh in heads]
        yield
        prob = [e[hh] / jnp.sum(e[hh], axis=-1, keepdims=True) for hh in heads]
        yield
        outs = [_dot(prob[hh], kv_ref[:, d + hh * hd:d + (hh + 1) * hd]) for hh in heads]
        yield
        h2 = h + _dot(jnp.concatenate(outs, axis=-1), wo_ref[...])
        o_ref[rows, :] = h2
        yield
        xn_ref[rows, :] = _rmsnorm_f32(h2, gf_ref[...]).astype(xn_ref.dtype)

    _interleave(*(_delayed(group(r), 2 * r) for r in range(XATTN_SPLIT)))


def _xattn(h, g, wq, kv, wo, g_ffn, seq, mem_len, tm):
    m, d = h.shape
    tiles_per_seq = seq // tm
    full = lambda a: pl.BlockSpec(a.shape, lambda i: (0, 0), pipeline_mode=pl.Buffered(1))
    row = pl.BlockSpec((tm, d), lambda i: (i, 0))
    return pl.pallas_call(
        _xattn_kernel,
        out_shape=(jax.ShapeDtypeStruct((m, d), F32), jax.ShapeDtypeStruct((m, d), BF16)),
        grid=(m // tm,),
        in_specs=[row, full(g), full(wq), pl.BlockSpec((mem_len, 2 * d), lambda i: (i // tiles_per_seq, 0)),
                  full(wo), full(g_ffn)],
        out_specs=(row, row),
        compiler_params=_params("parallel"),
        name="mem_cross_attn",
    )(h, g, wq, kv, wo, g_ffn)


def _ffn_kernel(h_ref, xn_ref, w1_ref, w2_ref, gf_ref, o_ref):
    tm = h_ref.shape[0]
    f = w1_ref.shape[1]
    rows_per = tm // FFN_SPLIT

    def group(r):
        rows = pl.ds(r * rows_per, rows_per)
        xn = xn_ref[rows, :]
        acc = None
        for c in range(f // FFN_TF):
            cs = slice(c * FFN_TF, (c + 1) * FFN_TF)
            u = jnp.dot(xn, w1_ref[:, cs], preferred_element_type=F32)
            act = jnp.square(jnp.maximum(u, 0.0)).astype(BF16)
            part = jnp.dot(act, w2_ref[cs, :], preferred_element_type=F32)
            acc = part if acc is None else acc + part
            yield
        o_ref[rows, :] = _rmsnorm_f32(h_ref[rows, :] + acc, gf_ref[...])

    _interleave(*(_delayed(group(r), r) for r in range(FFN_SPLIT)))


def _ffn(h, xn, w1, w2, g_final, tm):
    m, d = h.shape
    full = lambda a: pl.BlockSpec(a.shape, lambda i: (0, 0), pipeline_mode=pl.Buffered(1))
    row = pl.BlockSpec((tm, d), lambda i: (i, 0))
    return pl.pallas_call(
        _ffn_kernel,
        out_shape=jax.ShapeDtypeStruct((m, d), F32),
        grid=(m // tm,),
        in_specs=[row, row, full(w1), full(w2), full(g_final)],
        out_specs=row,
        compiler_params=_params("parallel"),
        name="relu2_mlp_final_norm",
    )(h, xn, w1, w2, g_final)


def _pad_cols(a, width):
    return jnp.pad(a, ((0, 0), (0, width - a.shape[1])))


def _pad_rows(a, rows):
    return jnp.pad(a, ((0, rows - a.shape[0]), (0, 0)))


def kernel(x, mem, norm_mix_g, w_in, shift_mu, w0, w2, a0, a2, g2, k_k, k_a, r_k, gn_w, gn_b, p_rwkv, p_attn, w_out, norm_x_g, norm_mem_g, xa_wq, xa_wkv, xa_wo, norm_ffn_g, ffn_w1, ffn_w2, norm_final_g):
    batch, seq, d = x.shape
    mem_len = mem.shape[1]
    assert w_in.shape[0] == 1 and d == RWKV_DIM and seq % RW_TM == 0
    m = batch * seq
    x2 = x.reshape(m, d)

    w = w_in[0]
    c_wd, c_ad, c_gd = 3 * RWKV_DIM, 3 * RWKV_DIM + DECAY_LORA, 3 * RWKV_DIM + DECAY_LORA + AAA_LORA
    c_q = c_gd + GATE_LORA
    c_gate = c_q + 3 * ATTN_DIM
    lora_cols = lambda a: jnp.concatenate(
        [_pad_cols(a[:, c_wd:c_ad], 128), _pad_cols(a[:, c_ad:c_gd], 128), _pad_cols(a[:, c_gd:c_q], 256)], axis=1)
    mu = shift_mu[0].reshape(1, -1).astype(F32)
    w_main, w_lora = w[:, :c_wd].astype(BF16), lora_cols(w).astype(BF16)
    mu_main, mu_lora = mu[:, :c_wd], lora_cols(mu)
    w_qkv = w[:, c_q:c_gate].astype(BF16)
    w_gate = w[:, c_gate:].astype(BF16)
    vecs = jnp.stack([w0[0], a0[0], k_k[0], k_a[0], r_k[0].reshape(-1), gn_w[0], gn_b[0],
                      jnp.zeros((RWKV_DIM,), F32)]).astype(F32)
    w2p = _pad_rows(w2[0], 128).astype(BF16)
    a2p = _pad_rows(a2[0], 128).astype(BF16)
    g2p = _pad_rows(g2[0], 256).astype(BF16)

    g_mix = norm_mix_g[0].reshape(1, d)
    y_rwkv, qkv = _rwkv(x2, g_mix, w_main, w_lora, mu_main, mu_lora, w_qkv, vecs, w2p, a2p, g2p, batch, seq)
    later = ((), (p_rwkv[0], p_attn[0], w_out[0], xa_wq[0], xa_wo[0], xa_wkv[0]), (ffn_w1[0], ffn_w2[0]))
    attn = [_dil_attention(qkv, g, batch, seq, later[g]) for g in range(len(DIL_GROUPS))]
    p_rwkv_b, p_attn_b, w_out_b, wq_b, wo_b, wkv_b = attn[1][2:]
    w1_b, w2_b = attn[2][2:]
    h1 = _merge(y_rwkv, [a[:2] for a in attn], x2, g_mix, w_gate, p_rwkv_b, p_attn_b, w_out_b, MERGE_TM)
    kv = _mem_kv(mem.reshape(batch * mem_len, d), norm_mem_g[0], wkv_b, mem_len)
    h2, xn2 = _xattn(h1, norm_x_g[0].reshape(1, d), wq_b, kv, wo_b, norm_ffn_g[0].reshape(1, d), seq, mem_len,
                     XATTN_TM)
    out = _ffn(h2, xn2, w1_b, w2_b, norm_final_g.reshape(1, d), FFN_TM)
    return out.reshape(batch, seq, d)
```

```python
import functools
import math

import jax
import jax.numpy as jnp
from jax import lax
from jax.experimental import pallas as pl
from jax.experimental.pallas import tpu as pltpu

F32 = jnp.float32
BF16 = jnp.bfloat16

LANES = 128
HEAD_DIM = 64
PAIR = 2 * HEAD_DIM
CHUNK = 64
RWKV_DIM = 1024
N_PAIRS = RWKV_DIM // PAIR
DECAY_LORA = 64
AAA_LORA = 64
GATE_LORA = 160
LORA_WD_OFF = 3 * RWKV_DIM
LORA_AD_OFF = LORA_WD_OFF + 128
LORA_GD_OFF = LORA_AD_OFF + 128
RW_WIDTH = LORA_GD_OFF + 256
RW_TM = 256
RW_GROUP = 4
RW_TN = 256
RW_BG_EVERY = (2, 2, 1)
GN_EPS = HEAD_DIM * 1e-5
LOG2E = 1.0 / math.log(2.0)
DECAY_SCALE = math.exp(-0.5)
NORM_EPS = 1e-6
DIL_GROUPS = ((128, 1), (512, 4), (2048, 16))
HEADS_PER_GROUP = 4
ATTN_HEADS = 12
ATTN_GROUP_DIM = HEADS_PER_GROUP * HEAD_DIM
ATTN_GROUP_SLABS = ATTN_GROUP_DIM // LANES
ATTN_DIM = ATTN_HEADS * HEAD_DIM
ATTN_BLK = 128
ATTN_TILE = 2048
ATTN_UNROLL = 8
XATTN_HEADS = 4
MERGE_TM = 1024
MERGE_SPLIT = 2
XATTN_TM = 1024
XATTN_SPLIT = 2
FFN_TM = 1024
FFN_SPLIT = 2
FFN_TF = 1024
VMEM_LIMIT = 56 * 1024 * 1024


def _dot(a, b):
    return jnp.dot(a.astype(BF16), b.astype(BF16), preferred_element_type=F32)


def _dot_nt(a, b):
    return lax.dot_general(a.astype(BF16), b.astype(BF16), (((1,), (1,)), ((), ())),
                           preferred_element_type=F32)


def _params(*sem):
    return pltpu.CompilerParams(dimension_semantics=sem, vmem_limit_bytes=VMEM_LIMIT)


def _rmsnorm_f32(x, g):
    return x * lax.rsqrt(jnp.mean(x * x, axis=-1, keepdims=True) + NORM_EPS) * g


VEC_W0, VEC_A0, VEC_KK, VEC_KA, VEC_RK, VEC_GNW, VEC_GNB = range(7)


def _interleave(*gens, background=None):
    live = list(gens)
    while live:
        for g in list(live):
            try:
                next(g)
            except StopIteration:
                live.remove(g)
        if background is not None:
            next(background, None)


def _every(gen, n):
    for _ in gen:
        for _ in range(n):
            yield


def _delayed(gen, steps):
    for _ in range(steps):
        yield
    yield from gen


def _rwkv_phases(vec_ref, w2_ref, a2_ref, g2_ref, st_ref):
    L = CHUNK
    pairs = range(N_PAIRS)
    cat = jnp.concatenate

    head_a = lax.broadcasted_iota(jnp.int32, (L, PAIR), 1) < HEAD_DIM
    rowh = lax.broadcasted_iota(jnp.int32, (L, PAIR), 0)
    laneh = lax.broadcasted_iota(jnp.int32, (L, PAIR), 1)
    eye_pk = jnp.where((laneh == rowh) | (laneh == rowh + HEAD_DIM), 1.0, 0.0)
    rows_sc = lax.broadcasted_iota(jnp.int32, (2 * L, 2 * PAIR), 0)
    src_sc = lax.broadcasted_iota(jnp.int32, (2 * L, 2 * PAIR), 1) & (L - 1)
    tri_mask = ((rows_sc < L) & (rows_sc > src_sc)) | ((rows_sc >= L) & (rows_sc - L >= src_sc))
    r3 = lax.broadcasted_iota(jnp.int32, (L, 3 * L), 0)
    c3 = lax.broadcasted_iota(jnp.int32, (L, 3 * L), 1)
    c3 = c3 - jnp.where(c3 >= L, L, 0) - jnp.where(c3 >= 2 * L, L, 0)
    tri3 = jnp.where(r3 >= c3, 1.0, 0.0).astype(BF16)
    zb = jnp.zeros((L, PAIR), BF16)
    vec = lambda i: vec_ref[i:i + 1, :]

    def hsum(x):
        sa = jnp.sum(jnp.where(head_a, x, 0.0), axis=-1, keepdims=True)
        sb = jnp.sum(jnp.where(head_a, 0.0, x), axis=-1, keepdims=True)
        return jnp.where(head_a, sa, sb)

    def sl(x, p):
        return x[:, p * PAIR:(p + 1) * PAIR]

    m_a = lambda x: jnp.where(head_a, x, jnp.zeros_like(x))
    m_b = lambda x: jnp.where(head_a, jnp.zeros_like(x), x)

    def heads_rows(*cols):
        return cat([cat([m_a(x) for x in cols], axis=1), cat([m_b(x) for x in cols], axis=1)], axis=0)

    def prep(p_ref, ops):
        pr = p_ref[:, 0:RWKV_DIM]
        pk = p_ref[:, RWKV_DIM:2 * RWKV_DIM]
        pv = p_ref[:, 2 * RWKV_DIM:3 * RWKV_DIM]
        wd = jnp.tanh(p_ref[:, LORA_WD_OFF:LORA_WD_OFF + 128])
        lw = -DECAY_SCALE * jax.nn.sigmoid(vec(VEC_W0) + _dot(wd, w2_ref[...]))
        yield
        a_lr = jax.nn.sigmoid(vec(VEC_A0) + _dot(p_ref[:, LORA_AD_OFF:LORA_AD_OFF + 128], a2_ref[...]))
        yield
        ops["gate"] = _dot(jax.nn.sigmoid(p_ref[:, LORA_GD_OFF:LORA_GD_OFF + 256]), g2_ref[...])
        yield
        kk = pk * vec(VEC_KK)
        kbar = pk * (1.0 + (a_lr - 1.0) * vec(VEC_KA))
        yield
        h1 = lw.astype(BF16)
        r1 = lw - h1.astype(F32)
        h2 = r1.astype(BF16)
        h3 = (r1 - h2.astype(F32)).astype(BF16)
        c = jnp.dot(tri3, cat([h1, h2, h3], axis=0), preferred_element_type=F32)
        yield
        cl = c[L - 1:L, :]
        einv = jnp.exp(-c)
        yield
        edec = jnp.exp(cl - c)
        yield
        rt = pr * jnp.exp(c)
        yield
        eprev = jnp.exp(c - lw)
        ops["gl"] = jnp.exp(cl)
        ops["rkr"] = pr * kbar * vec(VEC_RK)
        ops["rt"] = rt
        ops["pv"] = pv
        yield
        ss = [hsum(sl(kk, p) * sl(kk, p)) for p in pairs]
        yield
        kkn = [sl(kk, p) * lax.rsqrt(jnp.maximum(ss[p], 1e-24)) for p in pairs]
        bvec = [kkn[p] * sl(a_lr, p) for p in pairs]
        yield
        ops["at"] = [(-kkn[p] * sl(eprev, p)).astype(BF16) for p in pairs]
        ops["rtp"] = [sl(rt, p).astype(BF16) for p in pairs]
        yield
        ops["bt"] = [(bvec[p] * sl(einv, p)).astype(BF16) for p in pairs]
        ops["kt"] = [(sl(kbar, p) * sl(einv, p)).astype(BF16) for p in pairs]
        yield
        ops["bh"] = [bvec[p] * sl(edec, p) for p in pairs]
        ops["kh"] = [sl(kbar, p) * sl(edec, p) for p in pairs]
        ops["vp"] = [sl(pv, p).astype(BF16) for p in pairs]

    def main(ops):
        at, rtp, bt, kt, bh, kh, vp = (ops[k] for k in ("at", "rtp", "bt", "kt", "bh", "kh", "vp"))
        sc = [_dot_nt(cat([at[p], rtp[p]], axis=0), cat([m_a(bt[p]), m_b(kt[p]), m_a(kt[p]), m_b(bt[p])], axis=0))
              for p in pairs]
        yield
        sc = [jnp.where(tri_mask, sc[p], 0.0) for p in pairs]
        npk = [jnp.where(head_a, sc[p][:L, :PAIR], sc[p][:L, PAIR:]) for p in pairs]
        aak = [jnp.where(head_a, sc[p][:L, PAIR:], sc[p][:L, :PAIR]) for p in pairs]
        bot_sc = [sc[p][L:] for p in pairs]
        yield
        aakv = [_dot(aak[p], heads_rows(vp[p])).astype(BF16) for p in pairs]
        yield
        tpk = [eye_pk for p in pairs]
        for _ in range(6):
            res = [_dot(npk[p], heads_rows(npk[p].astype(BF16), tpk[p].astype(BF16))) for p in pairs]
            npk = [res[p][:, :PAIR] for p in pairs]
            tpk = [tpk[p] + res[p][:, PAIR:] for p in pairs]
            yield
        pq = [_dot(tpk[p], heads_rows(at[p], aakv[p])) for p in pairs]
        yield
        pb = [pq[p][:, :PAIR].astype(BF16) for p in pairs]
        qb = [pq[p][:, PAIR:].astype(BF16) for p in pairs]
        rhs = [cat([cat([m_a(pb[p]), m_a(qb[p])], axis=1), cat([zb, m_b(vp[p])], axis=1),
                    cat([zb, m_a(vp[p])], axis=1), cat([m_b(pb[p]), m_b(qb[p])], axis=1)], axis=0) for p in pairs]
        zt1 = [cat([bh[p], kh[p]], axis=0).T for p in pairs]
        zt2 = [cat([kh[p], bh[p]], axis=0).T for p in pairs]
        yield
        lhs_st = [cat([jnp.where(head_a, zt1[p][:L], zt1[p][L:]), jnp.where(head_a, zt2[p][:L], zt2[p][L:])],
                      axis=1) for p in pairs]
        tb = [_dot(cat([bot_sc[p], lhs_st[p]], axis=0), rhs[p]) for p in pairs]
        yield
        st = [st_ref[p].astype(BF16) for p in pairs]
        ys = [_dot(cat([sl(ops["rt"], p) + tb[p][:L, :PAIR],
                        jnp.where(eye_pk > 0.0, sl(ops["gl"], p), 0.0) + tb[p][L:, :PAIR]], axis=0),
                   heads_rows(st[p])) for p in pairs]
        yield
        for p in pairs:
            st_ref[p] = ys[p][L:] + tb[p][L:, PAIR:]
        ops["y"] = [ys[p][:L] + tb[p][:L, PAIR:] for p in pairs]

    def norm(ops, y_ref):
        y = ops["y"]
        mean = [hsum(y[p]) * (1.0 / HEAD_DIM) for p in pairs]
        yield
        dev = [y[p] - mean[p] for p in pairs]
        var = [hsum(dev[p] * dev[p]) * (1.0 / HEAD_DIM) for p in pairs]
        yield
        bonus = [hsum(sl(ops["rkr"], p)) * sl(ops["pv"], p) for p in pairs]
        yield
        for p in pairs:
            cs = slice(p * PAIR, (p + 1) * PAIR)
            yn = (dev[p] * lax.rsqrt(var[p] + GN_EPS) * vec_ref[VEC_GNW:VEC_GNW + 1, cs]
                  + vec_ref[VEC_GNB:VEC_GNB + 1, cs])
            y_ref[:, cs] = ((yn + bonus[p]) * sl(ops["gate"], p)).astype(y_ref.dtype)
            if p % 2 == 1:
                yield

    return prep, main, norm


def _rwkv_kernel(x_ref, xn_ref, g_ref, wm_ref, wl_ref, mum_ref, mul_ref, wqkv_ref, vec_ref, w2_ref, a2_ref, g2_ref,
                 y_ref, qkv_ref, p_scr, n_scr, last_ref, st_ref):
    tm = x_ref.shape[0]
    n_chunks = tm // CHUNK
    t = pl.program_id(1)
    slot = lax.rem(t, 2)
    row = lax.broadcasted_iota(jnp.int32, (tm, RW_TN), 0)

    def norm_x(src_ref, dst_slot):
        n_scr[dst_slot] = _rmsnorm_f32(src_ref[...], g_ref[...]).astype(BF16)
        yield

    def rw_proj(src_slot, dst_slot):
        n = n_scr[src_slot]
        blocks = [(wm_ref, mum_ref, j, j) for j in range(wm_ref.shape[1] // RW_TN)]
        blocks += [(wl_ref, mul_ref, j, wm_ref.shape[1] // RW_TN + j) for j in range(wl_ref.shape[1] // RW_TN)]
        for w_ref, mu_ref, j, jd in blocks:
            cs = slice(j * RW_TN, (j + 1) * RW_TN)
            cd = slice(jd * RW_TN, (jd + 1) * RW_TN)
            p = jnp.dot(n, w_ref[:, cs], preferred_element_type=F32)
            prev = jnp.where(row == 0, last_ref[:, cd], pltpu.roll(p, 1, 0))
            last_ref[:, cd] = p[tm - 1:tm, :]
            p_scr[dst_slot, :, cd] = p + mu_ref[:, cs] * (prev - p)
            yield

    def qkv_proj():
        n = n_scr[slot]
        per = ATTN_GROUP_SLABS
        for j in range(wqkv_ref.shape[1] // ATTN_GROUP_DIM):
            res = jnp.dot(n, wqkv_ref[:, j * ATTN_GROUP_DIM:(j + 1) * ATTN_GROUP_DIM], preferred_element_type=F32)
            for s in range(per):
                qkv_ref[j * per + s] = res[:, s * LANES:(s + 1) * LANES]
            yield

    @pl.when(t == 0)
    def _init():
        st_ref[...] = jnp.zeros_like(st_ref)
        last_ref[...] = jnp.zeros_like(last_ref)
        _interleave(norm_x(x_ref, 0))
        _interleave(rw_proj(0, 0))

    def background():
        yield from qkv_proj()
        yield from norm_x(xn_ref, 1 - slot)
        yield from rw_proj(1 - slot, 1 - slot)

    prep, main, norm = _rwkv_phases(vec_ref, w2_ref, a2_ref, g2_ref, st_ref)
    rows = [pl.ds(u * CHUNK, CHUNK) for u in range(n_chunks)]
    ops = [{} for _ in range(n_chunks)]
    prep_u = lambda u: prep(p_scr.at[slot, rows[u]], ops[u])
    norm_u = lambda u: norm(ops[u], y_ref.at[rows[u]])
    bg = background()
    for first in range(0, n_chunks, RW_GROUP):
        group = range(first, first + RW_GROUP)
        _interleave(*(prep_u(u) for u in group), background=_every(bg, RW_BG_EVERY[0]))
        _interleave(*(_delayed(main(ops[u]), 2 * (u - first)) for u in group),
                    background=_every(bg, RW_BG_EVERY[1]))
        _interleave(*(norm_u(u) for u in group), background=_every(bg, RW_BG_EVERY[2]))
    _interleave(bg)


def _rwkv(x, g, w_main, w_lora, mu_main, mu_lora, w_qkv, vecs, w2p, a2p, g2p, batch, seq):
    m, d = x.shape
    nt = seq // RW_TM
    slabs = w_qkv.shape[1] // LANES
    assert w_main.shape[1] + w_lora.shape[1] == RW_WIDTH
    full = lambda a: pl.BlockSpec(a.shape, lambda b, t: (0, 0), pipeline_mode=pl.Buffered(1))
    return pl.pallas_call(
        _rwkv_kernel,
        out_shape=(jax.ShapeDtypeStruct((m, RWKV_DIM), BF16), jax.ShapeDtypeStruct((slabs, m, LANES), F32)),
        grid=(batch, nt),
        in_specs=[pl.BlockSpec((RW_TM, d), lambda b, t: (b * nt + t, 0)),
                  pl.BlockSpec((RW_TM, d), lambda b, t: (b * nt + jnp.minimum(t + 1, nt - 1), 0)),
                  full(g), full(w_main), full(w_lora), full(mu_main), full(mu_lora), full(w_qkv), full(vecs),
                  full(w2p), full(a2p), full(g2p)],
        out_specs=(pl.BlockSpec((RW_TM, RWKV_DIM), lambda b, t: (b * nt + t, 0)),
                   pl.BlockSpec((slabs, RW_TM, LANES), lambda b, t: (0, b * nt + t, 0))),
        scratch_shapes=[pltpu.VMEM((2, RW_TM, RW_WIDTH), F32), pltpu.VMEM((2, RW_TM, d), BF16),
                        pltpu.VMEM((1, RW_WIDTH), F32), pltpu.VMEM((N_PAIRS, HEAD_DIM, PAIR), F32)],
        compiler_params=_params("parallel", "arbitrary"),
        name="rwkv7_proj_chunk_scan",
    )(x, x, g, w_main, w_lora, mu_main, mu_lora, w_qkv, vecs, w2p, a2p, g2p)


def _dil_attn_kernel(q_ref, kp_ref, k_ref, vp_ref, v_ref, *rest, window, dilation, slopes):
    n_casts = (len(rest) - 2) // 2
    o_ref, l_ref = rest[n_casts], rest[n_casts + 1]
    for src, dst in zip(rest[:n_casts], rest[n_casts + 2:]):
        dst[...] = src[...].astype(dst.dtype)
    blk = ATTN_BLK
    width = ATTN_GROUP_DIM
    n_win = q_ref.shape[1] // window
    lane_head = lax.broadcasted_iota(jnp.int32, (blk, width), 1) >> 6
    shape = (HEADS_PER_GROUP * blk, 2 * blk)
    rowi = lax.broadcasted_iota(jnp.int32, shape, 0)
    ki = lax.broadcasted_iota(jnp.int32, shape, 1)
    steps = (rowi & (blk - 1)) + blk - ki
    in_band = (steps >= 0) & (steps <= blk)
    slope = jnp.full(shape, slopes[0], F32)
    for h in range(1, HEADS_PER_GROUP):
        slope = jnp.where(rowi >= h * blk, slopes[h], slope)
    alibi = (-LOG2E) * slope * (steps * dilation).astype(F32)
    bias = jnp.where(in_band, alibi, -jnp.inf)
    first_key = jnp.where(pl.program_id(1) == 0, blk, 0)
    bias_w0 = jnp.where(in_band & (ki >= first_key), alibi, -jnp.inf)
    packed_pv = n_win == 1
    lane_kv = lax.broadcasted_iota(jnp.int32, (2 * blk, width), 1) >> 6
    kshape = (HEADS_PER_GROUP * 2 * blk, width)
    ones_head = jnp.where((lax.broadcasted_iota(jnp.int32, kshape, 0) >> 8)
                          == (lax.broadcasted_iota(jnp.int32, kshape, 1) >> 6), 1.0, 0.0).astype(BF16)

    def load(ref, rows):
        return jnp.concatenate([ref[s, rows, :] for s in range(ATTN_GROUP_SLABS)], axis=1)

    def unit(j, r):
        rows = pl.ds(j * window + r, blk, stride=dilation)
        q = (load(q_ref, rows) * (LOG2E * HEAD_DIM ** -0.5)).astype(BF16)
        if j == 0:
            prev = pl.ds(r, blk, stride=dilation)
            k = jnp.concatenate([load(kp_ref, prev), load(k_ref, rows)], axis=0).astype(BF16)
            v = jnp.concatenate([load(vp_ref, prev), load(v_ref, rows)], axis=0).astype(BF16)
            b = bias_w0
        else:
            both = pl.ds((j - 1) * window + r, 2 * blk, stride=dilation)
            k = load(k_ref, both).astype(BF16)
            v = load(v_ref, both).astype(BF16)
            b = bias
        yield
        qs = jnp.concatenate([jnp.where(lane_head == h, q, jnp.zeros_like(q)) for h in range(HEADS_PER_GROUP)],
                             axis=0)
        s = lax.dot_general(qs, k, (((1,), (1,)), ((), ())), preferred_element_type=F32) + b
        yield
        mx = jnp.max(s, axis=-1, keepdims=True)
        yield
        e = lax.exp2(s - mx).astype(BF16)
        yield
        e_all = jnp.concatenate([e[h * blk:(h + 1) * blk] for h in range(HEADS_PER_GROUP)], axis=1)
        den = jnp.dot(e_all, ones_head, preferred_element_type=F32)
        yield
        mb = jnp.broadcast_to(mx[:blk], (blk, width))
        if packed_pv:
            v_heads = jnp.concatenate([jnp.where(lane_kv == h, v, jnp.zeros_like(v))
                                       for h in range(HEADS_PER_GROUP)], axis=0)
            o = jnp.dot(e_all, v_heads, preferred_element_type=F32)
        else:
            o = jnp.dot(e[:blk], v, preferred_element_type=F32)
        for h in range(1, HEADS_PER_GROUP):
            if not packed_pv:
                oh = jnp.dot(e[h * blk:(h + 1) * blk], v, preferred_element_type=F32)
                o = jnp.where(lane_head == h, oh, o)
            mb = jnp.where(lane_head == h, mx[h * blk:(h + 1) * blk], mb)
        yield
        o = o / den
        lb = mb * (1.0 / LOG2E) + jnp.log(den)
        yield
        for s_ in range(ATTN_GROUP_SLABS):
            o_ref[s_, rows, :] = o[:, s_ * LANES:(s_ + 1) * LANES]
            l_ref[s_, rows, :] = lb[:, s_ * LANES:(s_ + 1) * LANES]

    def residues(i, carry):
        units = [(j, i * per_iter + dr) for dr in range(per_iter) for j in range(n_win)]
        for a in range(0, len(units), ATTN_UNROLL):
            _interleave(*(unit(j, r) for j, r in units[a:a + ATTN_UNROLL]))
        return carry

    per_iter = max(min(dilation, ATTN_UNROLL // n_win), 1)
    if dilation == per_iter:
        residues(0, 0)
    else:
        lax.fori_loop(0, dilation // per_iter, residues, 0)


def _alibi_slopes(n_heads):
    return [2.0 ** (-8.0 * (h + 1.0) / n_heads) for h in range(n_heads)]


def _dil_attention(qkv, group, batch, seq, weights=()):
    window, dilation = DIL_GROUPS[group]
    assert window // dilation == ATTN_BLK and seq % window == 0
    m = qkv.shape[1]
    tile = max(window, ATTN_TILE)
    assert seq % tile == 0
    n_win = tile // window
    ntile = seq // tile
    slopes = tuple(_alibi_slopes(ATTN_HEADS)[group * HEADS_PER_GROUP:(group + 1) * HEADS_PER_GROUP])
    groups = ATTN_DIM // ATTN_GROUP_DIM
    cur = lambda col: pl.BlockSpec((ATTN_GROUP_SLABS, tile, LANES), lambda b, i: (col, b * ntile + i, 0))
    prev = lambda col: pl.BlockSpec((ATTN_GROUP_SLABS, window, LANES),
                                    lambda b, i: (col, jnp.maximum((b * ntile + i) * n_win - 1, 0), 0))
    out_sds = jax.ShapeDtypeStruct((ATTN_GROUP_SLABS, m, LANES), F32)
    out_spec = pl.BlockSpec((ATTN_GROUP_SLABS, tile, LANES), lambda b, i: (0, b * ntile + i, 0))
    steps = batch * ntile
    w_specs = [pl.BlockSpec((w.shape[0] // steps, w.shape[1]), lambda b, i: (b * ntile + i, 0)) for w in weights]
    return pl.pallas_call(
        functools.partial(_dil_attn_kernel, window=window, dilation=dilation, slopes=slopes),
        out_shape=(out_sds, out_sds, *(jax.ShapeDtypeStruct(w.shape, BF16) for w in weights)),
        grid=(batch, ntile),
        in_specs=[cur(group), prev(groups + group), cur(groups + group),
                  prev(2 * groups + group), cur(2 * groups + group), *w_specs],
        out_specs=(out_spec, out_spec, *w_specs),
        compiler_params=_params("parallel", "arbitrary"),
        name=f"dilated_attn_g{group}",
    )(qkv, qkv, qkv, qkv, qkv, *weights)


def _merge_kernel(y_ref, o0_ref, o1_ref, o2_ref, l0_ref, l1_ref, l2_ref, x_ref,
                  g_ref, wg_ref, pr_ref, pa_ref, wo_ref, h_ref):
    d = x_ref.shape[1]
    rows_per = x_ref.shape[0] // MERGE_SPLIT

    def group(r):
        rows = pl.ds(r * rows_per, rows_per)
        wide = lambda ref: jnp.concatenate([ref[s, rows, :] for s in range(ATTN_GROUP_SLABS)], axis=1)
        x = x_ref[rows, :]
        gates = _dot(_rmsnorm_f32(x, g_ref[...]), wg_ref[...])
        yield
        t_rwkv = jnp.dot(y_ref[rows, :], pr_ref[...], preferred_element_type=F32)
        yield
        l0, l1, l2 = wide(l0_ref), wide(l1_ref), wide(l2_ref)
        mx = jnp.maximum(jnp.maximum(l0, l1), l2)
        e0, e1, e2 = jnp.exp(l0 - mx), jnp.exp(l1 - mx), jnp.exp(l2 - mx)
        y_attn = (e0 * wide(o0_ref) + e1 * wide(o1_ref) + e2 * wide(o2_ref)) / (e0 + e1 + e2)
        t_attn = _dot(y_attn, pa_ref[...])
        yield
        merged = jax.nn.sigmoid(gates[:, :d]) * t_rwkv
        yield
        merged = merged + jax.nn.sigmoid(gates[:, d:]) * t_attn
        yield
        h_ref[rows, :] = x + _dot(merged, wo_ref[...])

    _interleave(*(_delayed(group(r), 2 * r) for r in range(MERGE_SPLIT)))


def _merge(y_rwkv, attn, x, g, wg, p_rwkv, p_attn, w_out, tm):
    m, d = x.shape
    row = lambda w: pl.BlockSpec((tm, w), lambda i: (i, 0))
    slab = pl.BlockSpec((ATTN_GROUP_SLABS, tm, LANES), lambda i: (0, i, 0))
    full = lambda a: pl.BlockSpec(a.shape, lambda i: (0, 0), pipeline_mode=pl.Buffered(1))
    (o0, l0), (o1, l1), (o2, l2) = attn
    return pl.pallas_call(
        _merge_kernel,
        out_shape=jax.ShapeDtypeStruct((m, d), F32),
        grid=(m // tm,),
        in_specs=[row(d)] + [slab] * 6 + [row(d), full(g), full(wg), full(p_rwkv), full(p_attn), full(w_out)],
        out_specs=row(d),
        compiler_params=_params("parallel"),
        name="gated_merge",
    )(y_rwkv, o0, o1, o2, l0, l1, l2, x, g, wg, p_rwkv, p_attn, w_out)


def _mem_kv_kernel(mem_ref, g_ref, w_ref, o_ref):
    memn = _rmsnorm_f32(mem_ref[...], g_ref[...])
    o_ref[...] = _dot(memn, w_ref[...]).astype(o_ref.dtype)


def _mem_kv(mem2d, g, w_kv, mem_len):
    m, d = mem2d.shape
    n = w_kv.shape[1]
    return pl.pallas_call(
        _mem_kv_kernel,
        out_shape=jax.ShapeDtypeStruct((m, n), BF16),
        grid=(m // mem_len,),
        in_specs=[pl.BlockSpec((mem_len, d), lambda i: (i, 0)), pl.BlockSpec((1, d), lambda i: (0, 0)),
                  pl.BlockSpec((d, n), lambda i: (0, 0))],
        out_specs=pl.BlockSpec((mem_len, n), lambda i: (i, 0)),
        compiler_params=_params("parallel"),
        name="mem_kv_proj",
    )(mem2d, g.reshape(1, d), w_kv)


def _xattn_kernel(h_ref, g_ref, wq_ref, kv_ref, wo_ref, gf_ref, o_ref, xn_ref):
    d = h_ref.shape[1]
    hd = d // XATTN_HEADS
    heads = range(XATTN_HEADS)
    cs = lambda hh: slice(hh * hd, (hh + 1) * hd)
    rows_per = h_ref.shape[0] // XATTN_SPLIT

    def group(r):
        rows = pl.ds(r * rows_per, rows_per)
        h = h_ref[rows, :]
        q = (_dot(_rmsnorm_f32(h, g_ref[...]), wq_ref[...]) * (LOG2E * hd ** -0.5)).astype(BF16)
        yield
        s = [_dot_nt(q[:, cs(hh)], kv_ref[:, cs(hh)]) for hh in heads]
        yield
        e = [lax.exp2(s[hh] - jnp.max(s[hh], axis=-1, keepdims=True)) for hh in heads]
        yield
        prob = [e[hh] / jnp.sum(e[hh], axis=-1, keepdims=True) for hh in heads]
        yield
        outs = [_dot(prob[hh], kv_ref[:, d + hh * hd:d + (hh + 1) * hd]) for hh in heads]
        yield
        h2 = h + _dot(jnp.concatenate(outs, axis=-1), wo_ref[...])
        o_ref[rows, :] = h2
        yield
        xn_ref[rows, :] = _rmsnorm_f32(h2, gf_ref[...]).astype(xn_ref.dtype)

    _interleave(*(_delayed(group(r), 2 * r) for r in range(XATTN_SPLIT)))


def _xattn(h, g, wq, kv, wo, g_ffn, seq, mem_len, tm):
    m, d = h.shape
    tiles_per_seq = seq // tm
    full = lambda a: pl.BlockSpec(a.shape, lambda i: (0, 0), pipeline_mode=pl.Buffered(1))
    row = pl.BlockSpec((tm, d), lambda i: (i, 0))
    return pl.pallas_call(
        _xattn_kernel,
        out_shape=(jax.ShapeDtypeStruct((m, d), F32), jax.ShapeDtypeStruct((m, d), BF16)),
        grid=(m // tm,),
        in_specs=[row, full(g), full(wq), pl.BlockSpec((mem_len, 2 * d), lambda i: (i // tiles_per_seq, 0)),
                  full(wo), full(g_ffn)],
        out_specs=(row, row),
        compiler_params=_params("parallel"),
        name="mem_cross_attn",
    )(h, g, wq, kv, wo, g_ffn)


def _ffn_kernel(h_ref, xn_ref, w1_ref, w2_ref, gf_ref, o_ref):
    tm = h_ref.shape[0]
    f = w1_ref.shape[1]
    rows_per = tm // FFN_SPLIT

    def group(r):
        rows = pl.ds(r * rows_per, rows_per)
        xn = xn_ref[rows, :]
        acc = None
        for c in range(f // FFN_TF):
            cs = slice(c * FFN_TF, (c + 1) * FFN_TF)
            u = jnp.dot(xn, w1_ref[:, cs], preferred_element_type=F32)
            act = jnp.square(jnp.maximum(u, 0.0)).astype(BF16)
            part = jnp.dot(act, w2_ref[cs, :], preferred_element_type=F32)
            acc = part if acc is None else acc + part
            yield
        o_ref[rows, :] = _rmsnorm_f32(h_ref[rows, :] + acc, gf_ref[...])

    _interleave(*(_delayed(group(r), r) for r in range(FFN_SPLIT)))


def _ffn(h, xn, w1, w2, g_final, tm):
    m, d = h.shape
    full = lambda a: pl.BlockSpec(a.shape, lambda i: (0, 0), pipeline_mode=pl.Buffered(1))
    row = pl.BlockSpec((tm, d), lambda i: (i, 0))
    return pl.pallas_call(
        _ffn_kernel,
        out_shape=jax.ShapeDtypeStruct((m, d), F32),
        grid=(m // tm,),
        in_specs=[row, row, full(w1), full(w2), full(g_final)],
        out_specs=row,
        compiler_params=_params("parallel"),
        name="relu2_mlp_final_norm",
    )(h, xn, w1, w2, g_final)


def _pad_cols(a, width):
    return jnp.pad(a, ((0, 0), (0, width - a.shape[1])))


def _pad_rows(a, rows):
    return jnp.pad(a, ((0, rows - a.shape[0]), (0, 0)))


def kernel(x, mem, norm_mix_g, w_in, shift_mu, w0, w2, a0, a2, g2, k_k, k_a, r_k, gn_w, gn_b, p_rwkv, p_attn, w_out, norm_x_g, norm_mem_g, xa_wq, xa_wkv, xa_wo, norm_ffn_g, ffn_w1, ffn_w2, norm_final_g):
    batch, seq, d = x.shape
    mem_len = mem.shape[1]
    assert w_in.shape[0] == 1 and d == RWKV_DIM and seq % RW_TM == 0
    m = batch * seq
    x2 = x.reshape(m, d)

    w = w_in[0]
    c_wd, c_ad, c_gd = 3 * RWKV_DIM, 3 * RWKV_DIM + DECAY_LORA, 3 * RWKV_DIM + DECAY_LORA + AAA_LORA
    c_q = c_gd + GATE_LORA
    c_gate = c_q + 3 * ATTN_DIM
    lora_cols = lambda a: jnp.concatenate(
        [_pad_cols(a[:, c_wd:c_ad], 128), _pad_cols(a[:, c_ad:c_gd], 128), _pad_cols(a[:, c_gd:c_q], 256)], axis=1)
    mu = shift_mu[0].reshape(1, -1).astype(F32)
    w_main, w_lora = w[:, :c_wd].astype(BF16), lora_cols(w).astype(BF16)
    mu_main, mu_lora = mu[:, :c_wd], lora_cols(mu)
    w_qkv = w[:, c_q:c_gate].astype(BF16)
    w_gate = w[:, c_gate:].astype(BF16)
    vecs = jnp.stack([w0[0], a0[0], k_k[0], k_a[0], r_k[0].reshape(-1), gn_w[0], gn_b[0],
                      jnp.zeros((RWKV_DIM,), F32)]).astype(F32)
    w2p = _pad_rows(w2[0], 128).astype(BF16)
    a2p = _pad_rows(a2[0], 128).astype(BF16)
    g2p = _pad_rows(g2[0], 256).astype(BF16)

    g_mix = norm_mix_g[0].reshape(1, d)
    y_rwkv, qkv = _rwkv(x2, g_mix, w_main, w_lora, mu_main, mu_lora, w_qkv, vecs, w2p, a2p, g2p, batch, seq)
    later = ((), (p_rwkv[0], p_attn[0], w_out[0], xa_wq[0], xa_wo[0], xa_wkv[0]), (ffn_w1[0], ffn_w2[0]))
    attn = [_dil_attention(qkv, g, batch, seq, later[g]) for g in range(len(DIL_GROUPS))]
    p_rwkv_b, p_attn_b, w_out_b, wq_b, wo_b, wkv_b = attn[1][2:]
    w1_b, w2_b = attn[2][2:]
    h1 = _merge(y_rwkv, [a[:2] for a in attn], x2, g_mix, w_gate, p_rwkv_b, p_attn_b, w_out_b, MERGE_TM)
    kv = _mem_kv(mem.reshape(batch * mem_len, d), norm_mem_g[0], wkv_b, mem_len)
    h2, xn2 = _xattn(h1, norm_x_g[0].reshape(1, d), wq_b, kv, wo_b, norm_ffn_g[0].reshape(1, d), seq, mem_len,
                     XATTN_TM)
    out = _ffn(h2, xn2, w1_b, w2_b, norm_final_g.reshape(1, d), FFN_TM)
    return out.reshape(batch, seq, d)
```
